```python
import jax, jax.numpy as jnp
from jax import lax
import numpy as np


D_MODEL = 2048
BATCH = 4
SEQ = 2048
DEPTH = 2
DEC_BATCH = 128
DEC_SEQ = 1
PAST_LEN = 16384
PAGE_SIZE = 128

N_EVEN = (DEPTH + 1) // 2
N_ODD = DEPTH // 2
D_A = D_MODEL // 2
H_A = 4
DK_A = D_A // 2
DK_H = DK_A // H_A
DV_H = D_A // H_A
GATE_RANK = 16
GATE_TAU = 16.0
GLA_CHUNK = 16
D_B = D_MODEL - D_A
CONV_B_W = 31
D_C = D_MODEL
CONV_C_W = 3
W_IN_EVEN = 2 * DK_A + 2 * D_A + GATE_RANK + 2 * D_B
W_IN_ODD = 3 * D_C
SPLIT_EVEN = (DK_A, 2 * DK_A, 2 * DK_A + D_A, 2 * DK_A + 2 * D_A, 2 * DK_A + 2 * D_A + GATE_RANK)
P_HEADS = 8
N_KEYS = 128
N_EXPERTS = N_KEYS * N_KEYS
DQ = 256
DQ_HALF = DQ // 2
TOPK = 16
PEER_BLOCK = 128
EPS = 1e-6

kernel_name = 'hybrid_gla_conformer_shortconv_peer_step'


def rmsnorm(x, g):
    xf = x.astype(jnp.float32)
    y = xf * lax.rsqrt(jnp.mean(xf * xf, axis=-1, keepdims=True) + EPS)
    return (y * g.astype(jnp.float32)).astype(x.dtype)


def layernorm(x, g, b):
    xf = x.astype(jnp.float32)
    xc = xf - jnp.mean(xf, axis=-1, keepdims=True)
    y = xc * lax.rsqrt(jnp.mean(xc * xc, axis=-1, keepdims=True) + EPS)
    return (y * g.astype(jnp.float32) + b.astype(jnp.float32)).astype(x.dtype)


def causal_dwconv(u, buf, w):
    full = jnp.concatenate([buf.astype(u.dtype), u], axis=1)
    y = lax.conv_general_dilated(full, w.astype(u.dtype)[:, None, :], window_strides=(1,), padding='VALID',
                                 dimension_numbers=('NWC', 'WIO', 'NWC'), feature_group_count=u.shape[-1])
    return y, full[:, -(w.shape[0] - 1):]


def gla_recurrence(q, k, v, log_a, s0):
    bsz, t = q.shape[0], q.shape[1]
    pad = (-t) % GLA_CHUNK
    n = (t + pad) // GLA_CHUNK

    def chunks(a):
        a = jnp.pad(a.astype(jnp.float32), ((0, 0), (0, pad), (0, 0), (0, 0)))
        return a.reshape(bsz, n, GLA_CHUNK, H_A, a.shape[-1]).transpose(1, 0, 3, 2, 4)

    causal = jnp.tril(jnp.ones((GLA_CHUNK, GLA_CHUNK), dtype=bool))[:, :, None]

    def step(s, inp):
        qc, kc, vc, ac = inp
        b = jnp.cumsum(ac, axis=2)
        o_inter = jnp.einsum('bhtd,bhde->bhte', qc * jnp.exp(b), s)
        diff = b[:, :, :, None, :] - b[:, :, None, :, :]
        decay = jnp.where(causal, jnp.exp(jnp.where(causal, diff, 0.0)), 0.0)
        scores = jnp.einsum('bhtd,bhsd,bhtsd->bhts', qc, kc, decay)
        o = o_inter + jnp.einsum('bhts,bhse->bhte', scores, vc)
        b_last = b[:, :, -1:, :]
        s_new = jnp.exp(b_last[:, :, 0, :])[..., None] * s + jnp.einsum('bhsd,bhse->bhde', kc * jnp.exp(b_last - b), vc)
        return s_new, o

    s_fin, o = lax.scan(step, s0.astype(jnp.float32), (chunks(q), chunks(k), chunks(v), chunks(log_a)))
    o = o.transpose(1, 0, 3, 2, 4).reshape(bsz, n * GLA_CHUNK, H_A, o.shape[-1])[:, :t]
    return o, s_fin


def peer_ffn(h, wq, keys, table_u, table_v):
    bsz, t, d = h.shape
    xt = h.reshape(bsz * t, d)
    n = xt.shape[0]
    pad = (-n) % PEER_BLOCK
    xt = jnp.pad(xt, ((0, pad), (0, 0))).reshape(-1, PEER_BLOCK, d)

    def block(xb):
        q = (xb @ wq).reshape(PEER_BLOCK, P_HEADS, 2, DQ_HALF)
        s = jnp.einsum('thpd,hpkd->thpk', q, keys).astype(jnp.float32)
        sv, si = lax.top_k(s, TOPK)
        cand = (sv[:, :, 0, :, None] + sv[:, :, 1, None, :]).reshape(PEER_BLOCK, P_HEADS, TOPK * TOPK)
        cv, ci = lax.top_k(cand, TOPK)
        i1 = jnp.take_along_axis(si[:, :, 0], ci // TOPK, axis=-1)
        i2 = jnp.take_along_axis(si[:, :, 1], ci % TOPK, axis=-1)
        e = i1 * N_KEYS + i2
        g = jax.nn.softmax(cv, axis=-1).astype(xb.dtype)
        act = jax.nn.gelu(jnp.einsum('thkd,td->thk', table_u[e], xb))
        return jnp.einsum('thk,thkd->td', g * act, table_v[e])

    out = lax.map(block, xt).reshape(-1, d)[:n]
    return out.reshape(bsz, t, d)


def trunk(x, st_gla, st_cb, st_cc, norm_mix, norm_ffn, norm_final, w_in_even, w_a2, b_a2, gla_norm_g,
          conv_b_w, conv_b_bias, conv_b_ln_g, conv_b_ln_b, w_out_even, w_in_odd, conv_c_w, w_out_odd,
          peer_wq, peer_keys, peer_u, peer_v):
    bsz, t, _ = x.shape
    new_gla, new_cb, new_cc = [], [], []
    for i in range(DEPTH):
        j = i // 2
        h = rmsnorm(x, norm_mix[i])
        if i % 2 == 0:
            z = h @ w_in_even[j]
            q, k, v, g, a_lr, glu_in = jnp.split(z, SPLIT_EVEN, axis=-1)
            q = q.reshape(bsz, t, H_A, DK_H) * DK_H ** -0.5
            k = k.reshape(bsz, t, H_A, DK_H)
            v = v.reshape(bsz, t, H_A, DV_H)
            log_a = jax.nn.log_sigmoid((a_lr @ w_a2[j] + b_a2[j]).astype(jnp.float32)) / GATE_TAU
            o, s_new = gla_recurrence(q, k, v, log_a.reshape(bsz, t, H_A, DK_H), st_gla[j])
            o_a = rmsnorm(o, gla_norm_g[j].reshape(H_A, DV_H)).reshape(bsz, t, D_A).astype(x.dtype) * jax.nn.silu(g)
            ub, buf_b = causal_dwconv(jax.nn.glu(glu_in, axis=-1), st_cb[j], conv_b_w[j])
            o_b = jax.nn.silu(layernorm(ub + conv_b_bias[j], conv_b_ln_g[j], conv_b_ln_b[j]))
            x = x + jnp.concatenate([o_a, o_b], axis=-1) @ w_out_even[j]
            new_gla.append(s_new.astype(st_gla.dtype))
            new_cb.append(buf_b.astype(st_cb.dtype))
        else:
            gate_b, gate_c, hv = jnp.split(h @ w_in_odd[j], 3, axis=-1)
            c, buf_c = causal_dwconv(gate_c * hv, st_cc[j], conv_c_w[j])
            x = x + (gate_b * c) @ w_out_odd[j]
            new_cc.append(buf_c.astype(st_cc.dtype))
        x = x + peer_ffn(rmsnorm(x, norm_ffn[i]), peer_wq[i], peer_keys[i], peer_u[i], peer_v[i])
    return rmsnorm(x, norm_final), jnp.stack(new_gla), jnp.stack(new_cb), jnp.stack(new_cc)


def setup_inputs(seed: int = 0) -> dict:
    key = jax.random.key(seed)
    ks = jax.random.split(key, 24)

    def nrm(k, shape, scale):
        return jax.random.normal(k, shape, jnp.float32) * scale

    return {
        'x_prompt': nrm(ks[0], (BATCH, SEQ, D_MODEL), 1.0),
        'x_sample': nrm(ks[1], (DEC_BATCH, DEC_SEQ, D_MODEL), 1.0),
        'state_gla': nrm(ks[2], (N_EVEN, DEC_BATCH, H_A, DK_H, DV_H), 0.5),
        'state_conv_b': nrm(ks[3], (N_EVEN, DEC_BATCH, CONV_B_W - 1, D_B), 0.5),
        'state_conv_c': nrm(ks[4], (N_ODD, DEC_BATCH, CONV_C_W - 1, D_C), 0.5),
        'norm_mix': 1.0 + nrm(ks[5], (DEPTH, D_MODEL), 0.02),
        'norm_ffn': 1.0 + nrm(ks[6], (DEPTH, D_MODEL), 0.02),
        'norm_final': 1.0 + nrm(ks[7], (D_MODEL,), 0.02),
        'w_in_even': nrm(ks[8], (N_EVEN, D_MODEL, W_IN_EVEN), D_MODEL ** -0.5),
        'w_a2': nrm(ks[9], (N_EVEN, GATE_RANK, DK_A), GATE_RANK ** -0.5),
        'b_a2': nrm(ks[10], (N_EVEN, DK_A), 0.1),
        'gla_norm_g': 1.0 + nrm(ks[11], (N_EVEN, D_A), 0.02),
        'conv_b_w': nrm(ks[12], (N_EVEN, CONV_B_W, D_B), CONV_B_W ** -0.5),
        'conv_b_bias': nrm(ks[13], (N_EVEN, D_B), 0.02),
        'conv_b_ln_g': 1.0 + nrm(ks[14], (N_EVEN, D_B), 0.02),
        'conv_b_ln_b': nrm(ks[15], (N_EVEN, D_B), 0.02),
        'w_out_even': nrm(ks[16], (N_EVEN, D_A + D_B, D_MODEL), (D_A + D_B) ** -0.5),
        'w_in_odd': nrm(ks[17], (N_ODD, D_MODEL, W_IN_ODD), D_MODEL ** -0.5),
        'conv_c_w': nrm(ks[18], (N_ODD, CONV_C_W, D_C), CONV_C_W ** -0.5),
        'w_out_odd': nrm(ks[19], (N_ODD, D_C, D_MODEL), D_C ** -0.5),
        'peer_wq': nrm(ks[20], (DEPTH, D_MODEL, P_HEADS * DQ), D_MODEL ** -0.5),
        'peer_keys': nrm(ks[21], (DEPTH, P_HEADS, 2, N_KEYS, DQ_HALF), DQ_HALF ** -0.5),
        'peer_u': nrm(ks[22], (DEPTH, N_EXPERTS, D_MODEL), D_MODEL ** -0.5),
        'peer_v': nrm(ks[23], (DEPTH, N_EXPERTS, D_MODEL), P_HEADS ** -0.5),
    }


def reference(x_prompt, x_sample, state_gla, state_conv_b, state_conv_c, norm_mix, norm_ffn, norm_final,
              w_in_even, w_a2, b_a2, gla_norm_g, conv_b_w, conv_b_bias, conv_b_ln_g, conv_b_ln_b, w_out_even,
              w_in_odd, conv_c_w, w_out_odd, peer_wq, peer_keys, peer_u, peer_v):
    weights = (norm_mix, norm_ffn, norm_final, w_in_even, w_a2, b_a2, gla_norm_g, conv_b_w, conv_b_bias,
               conv_b_ln_g, conv_b_ln_b, w_out_even, w_in_odd, conv_c_w, w_out_odd, peer_wq, peer_keys, peer_u, peer_v)
    bp = x_prompt.shape[0]
    z_gla = jnp.zeros((N_EVEN, bp, H_A, DK_H, DV_H), state_gla.dtype)
    z_cb = jnp.zeros((N_EVEN, bp, CONV_B_W - 1, D_B), state_conv_b.dtype)
    z_cc = jnp.zeros((N_ODD, bp, CONV_C_W - 1, D_C), state_conv_c.dtype)
    y_prompt, gla_p, cb_p, cc_p = trunk(x_prompt, z_gla, z_cb, z_cc, *weights)
    y_sample, gla_s, cb_s, cc_s = trunk(x_sample, state_gla, state_conv_b, state_conv_c, *weights)
    return (y_prompt, y_sample, gla_p, cb_p, cc_p, gla_s, cb_s, cc_s)
```

```python
import functools

import numpy as np
import jax
import jax.numpy as jnp
from jax import lax
from jax.experimental import pallas as pl
from jax.experimental.pallas import tpu as pltpu

F32 = jnp.float32
BF16 = jnp.bfloat16
HIGHEST = lax.Precision.HIGHEST

EPS = 1e-6
GATE_TAU = 16.0
H_A = 4
P_HEADS = 8
N_KEYS = 128
TOPK = 16
CONV_B_W = 31
CONV_C_W = 3

LANES = 128
SUBLANES = 8
VMEM_LIMIT = 56 * 1024 * 1024

TOK_BLOCK = 768
EXPERT_BLOCK = 1024
GLA_CHUNK = 128
GLA_SUB = 16
EXP_CLAMP = 80.0
CONV_ROWS = 64
SAMPLE_BLOCK = 8


def _cparams(sem):
    return pltpu.CompilerParams(dimension_semantics=sem, vmem_limit_bytes=VMEM_LIMIT)


def _rmsnorm(x, g):
    return x * lax.rsqrt(jnp.mean(x * x, axis=-1, keepdims=True) + EPS) * g


def _sigmoid(x):
    return jax.nn.sigmoid(x)


def _log_sigmoid(x):
    return jnp.minimum(x, 0.0) - jnp.log(1.0 + jnp.exp(-jnp.abs(x)))


def _norm_matmul_body(x_ref, g_ref, w_ref, o_ref, xn_ref):
    @pl.when(pl.program_id(1) == 0)
    def _():
        xn_ref[...] = _rmsnorm(x_ref[...], g_ref[...]).astype(BF16)

    o_ref[...] = jnp.dot(xn_ref[...], w_ref[...], preferred_element_type=F32)


def _norm_matmul(x, g, w, tn):
    n, d = x.shape
    m = w.shape[1]
    return pl.pallas_call(
        _norm_matmul_body,
        grid=(n // TOK_BLOCK, m // tn),
        in_specs=[pl.BlockSpec((TOK_BLOCK, d), lambda i, j: (i, 0)),
                  pl.BlockSpec((1, d), lambda i, j: (0, 0)),
                  pl.BlockSpec((d, tn), lambda i, j: (0, j))],
        out_specs=pl.BlockSpec((TOK_BLOCK, tn), lambda i, j: (i, j)),
        out_shape=jax.ShapeDtypeStruct((n, m), F32),
        scratch_shapes=[pltpu.VMEM((TOK_BLOCK, d), BF16)],
        compiler_params=_cparams(("parallel", "arbitrary")),
        name="norm_matmul",
    )(x, g, w)


def _norm_matmul_gate_body(x_ref, g_ref, w_ref, wa_ref, wa2_ref, ba2_ref, o_ref, la_ref, xn_ref):
    @pl.when(pl.program_id(1) == 0)
    def _():
        xn = _rmsnorm(x_ref[...], g_ref[...]).astype(BF16)
        xn_ref[...] = xn
        a_lr = jnp.dot(xn, wa_ref[...], preferred_element_type=F32)
        pre = jnp.dot(a_lr, wa2_ref[...], precision=HIGHEST, preferred_element_type=F32) + ba2_ref[...]
        la_ref[...] = _log_sigmoid(pre) / GATE_TAU

    o_ref[...] = jnp.dot(xn_ref[...], w_ref[...], preferred_element_type=F32)


def _norm_matmul_gate(x, g, w, wa, wa2, ba2, tn):
    n, d = x.shape
    m = w.shape[1]
    dk = wa2.shape[1]
    return pl.pallas_call(
        _norm_matmul_gate_body,
        grid=(n // TOK_BLOCK, m // tn),
        in_specs=[pl.BlockSpec((TOK_BLOCK, d), lambda i, j: (i, 0)),
                  pl.BlockSpec((1, d), lambda i, j: (0, 0)),
                  pl.BlockSpec((d, tn), lambda i, j: (0, j)),
                  pl.BlockSpec((d, LANES), lambda i, j: (0, 0)),
                  pl.BlockSpec((LANES, dk), lambda i, j: (0, 0)),
                  pl.BlockSpec((1, dk), lambda i, j: (0, 0))],
        out_specs=[pl.BlockSpec((TOK_BLOCK, tn), lambda i, j: (i, j)),
                   pl.BlockSpec((TOK_BLOCK, dk), lambda i, j: (i, 0))],
        out_shape=[jax.ShapeDtypeStruct((n, m), F32), jax.ShapeDtypeStruct((n, dk), F32)],
        scratch_shapes=[pltpu.VMEM((TOK_BLOCK, d), BF16)],
        compiler_params=_cparams(("parallel", "arbitrary")),
        name="norm_matmul_gate",
    )(x, g, w, wa, wa2, ba2)


def _norm_matmul_t_body(x_ref, g_ref, w_ref, o_ref, xt_ref, xn_ref):
    @pl.when(pl.program_id(1) == 0)
    def _():
        xn = _rmsnorm(x_ref[...], g_ref[...])
        xn_ref[...] = xn.astype(BF16)
        xt_ref[...] = xn.T.astype(BF16)

    o_ref[...] = jnp.dot(xn_ref[...], w_ref[...], preferred_element_type=F32)


def _norm_matmul_t(x, g, w, tn):
    n, d = x.shape
    m = w.shape[1]
    return pl.pallas_call(
        _norm_matmul_t_body,
        grid=(n // TOK_BLOCK, m // tn),
        in_specs=[pl.BlockSpec((TOK_BLOCK, d), lambda i, j: (i, 0)),
                  pl.BlockSpec((1, d), lambda i, j: (0, 0)),
                  pl.BlockSpec((d, tn), lambda i, j: (0, j))],
        out_specs=[pl.BlockSpec((TOK_BLOCK, tn), lambda i, j: (i, j)),
                   pl.BlockSpec((d, TOK_BLOCK), lambda i, j: (0, i))],
        out_shape=[jax.ShapeDtypeStruct((n, m), F32), jax.ShapeDtypeStruct((d, n), BF16)],
        scratch_shapes=[pltpu.VMEM((TOK_BLOCK, d), BF16)],
        compiler_params=_cparams(("parallel", "arbitrary")),
        name="norm_matmul_t",
    )(x, g, w)


def _res_matmul_body(a_ref, w_ref, x_ref, o_ref, ab_ref):
    @pl.when(pl.program_id(1) == 0)
    def _():
        ab_ref[...] = a_ref[...].astype(BF16)

    o_ref[...] = x_ref[...] + jnp.dot(ab_ref[...], w_ref[...], preferred_element_type=F32)


def _res_matmul(a, w, x, tn):
    n, k = a.shape
    m = w.shape[1]
    return pl.pallas_call(
        _res_matmul_body,
        grid=(n // TOK_BLOCK, m // tn),
        in_specs=[pl.BlockSpec((TOK_BLOCK, k), lambda i, j: (i, 0)),
                  pl.BlockSpec((k, tn), lambda i, j: (0, j)),
                  pl.BlockSpec((TOK_BLOCK, tn), lambda i, j: (i, j))],
        out_specs=pl.BlockSpec((TOK_BLOCK, tn), lambda i, j: (i, j)),
        out_shape=jax.ShapeDtypeStruct((n, m), F32),
        scratch_shapes=[pltpu.VMEM((TOK_BLOCK, k), BF16)],
        compiler_params=_cparams(("parallel", "arbitrary")),
        name="res_matmul",
    )(a, w, x)


def _res_matmul_ln_body(oa_ref, ub_ref, lg_ref, lb_ref, w_ref, x_ref, o_ref, ab_ref):
    da = oa_ref.shape[1]

    @pl.when(pl.program_id(1) == 0)
    def _():
        u = ub_ref[...]
        uc = u - jnp.mean(u, axis=-1, keepdims=True)
        y = uc * lax.rsqrt(jnp.mean(uc * uc, axis=-1, keepdims=True) + EPS) * lg_ref[...] + lb_ref[...]
        ab_ref[:, :da] = oa_ref[...].astype(BF16)
        ab_ref[:, da:] = (y * _sigmoid(y)).astype(BF16)

    o_ref[...] = x_ref[...] + jnp.dot(ab_ref[...], w_ref[...], preferred_element_type=F32)


def _res_matmul_ln(oa, ub, lg, lb, w, x, tn):
    n, da = oa.shape
    db = ub.shape[1]
    m = w.shape[1]
    return pl.pallas_call(
        _res_matmul_ln_body,
        grid=(n // TOK_BLOCK, m // tn),
        in_specs=[pl.BlockSpec((TOK_BLOCK, da), lambda i, j: (i, 0)),
                  pl.BlockSpec((TOK_BLOCK, db), lambda i, j: (i, 0)),
                  pl.BlockSpec((1, db), lambda i, j: (0, 0)),
                  pl.BlockSpec((1, db), lambda i, j: (0, 0)),
                  pl.BlockSpec((da + db, tn), lambda i, j: (0, j)),
                  pl.BlockSpec((TOK_BLOCK, tn), lambda i, j: (i, j))],
        out_specs=pl.BlockSpec((TOK_BLOCK, tn), lambda i, j: (i, j)),
        out_shape=jax.ShapeDtypeStruct((n, m), F32),
        scratch_shapes=[pltpu.VMEM((TOK_BLOCK, da + db), BF16)],
        compiler_params=_cparams(("parallel", "arbitrary")),
        name="res_matmul_ln",
    )(oa, ub, lg, lb, w, x)


def _dot_f32(a, b, dims):
    return lax.dot_general(a, b, (dims, ((), ())), precision=HIGHEST, preferred_element_type=F32)


def _head_out(o, gn, g):
    y = o * lax.rsqrt(jnp.mean(o * o, axis=-1, keepdims=True) + EPS) * gn
    return y * (g * _sigmoid(g))


def _gla_prompt_body(q_ref, k_ref, v_ref, g_ref, la_ref, gn_ref, o_ref, s_out_ref, s_ref, acc_ref):
    c = pl.program_id(2)
    dk = q_ref.shape[1]

    @pl.when(c == 0)
    def _():
        s_ref[...] = jnp.zeros_like(s_ref)

    la = la_ref[...]
    row = lax.broadcasted_iota(jnp.int32, (GLA_CHUNK, GLA_CHUNK), 0)
    col = lax.broadcasted_iota(jnp.int32, (GLA_CHUNK, GLA_CHUNK), 1)
    tri = (row >= col).astype(F32)
    b = _dot_f32(tri, la, ((1,), (0,)))
    b_last = b[GLA_CHUNK - 1:GLA_CHUNK, :]
    q = q_ref[...] * (dk ** -0.5)
    k = k_ref[...]
    v = v_ref[...]
    s = s_ref[...]

    acc_ref[...] = _dot_f32(q * jnp.exp(b), s, ((1,), (0,)))

    sub_row = lax.broadcasted_iota(jnp.int32, (GLA_SUB, GLA_SUB), 0)
    sub_col = lax.broadcasted_iota(jnp.int32, (GLA_SUB, GLA_SUB), 1)
    causal = sub_row >= sub_col
    for j in range(GLA_CHUNK // GLA_SUB):
        lo, hi = j * GLA_SUB, (j + 1) * GLA_SUB
        b_j = b[lo:hi, :]
        b_first = b[lo:lo + 1, :]
        b_end = b[hi - 1:hi, :]
        q_d = q[lo:hi, :] * jnp.exp(b_j - b_first)
        k_d = k[lo:hi, :] * jnp.exp(jnp.minimum(b_first - b_j, EXP_CLAMP))
        sc = jnp.where(causal, _dot_f32(q_d, k_d, ((1,), (1,))), 0.0)
        acc_ref[lo:hi, :] += _dot_f32(sc, v[lo:hi, :], ((1,), (0,)))
        if hi < GLA_CHUNK:
            k_p = k[lo:hi, :] * jnp.exp(b_end - b_j)
            q_p = q[hi:, :] * jnp.exp(b[hi:, :] - b_end)
            sc = _dot_f32(q_p, k_p, ((1,), (1,)))
            acc_ref[hi:, :] += _dot_f32(sc, v[lo:hi, :], ((1,), (0,)))

    decay = jnp.exp(jnp.broadcast_to(b_last, (dk, dk))).T
    k_e = k * jnp.exp(b_last - b)
    s_new = jnp.concatenate([decay] * (s.shape[1] // dk), axis=1) * s + _dot_f32(k_e.T, v, ((1,), (0,)))
    s_ref[...] = s_new

    @pl.when(c == pl.num_programs(2) - 1)
    def _():
        s_out_ref[0, 0, 0] = s_new

    o_ref[...] = _head_out(acc_ref[...], gn_ref[...], g_ref[...])


def _gla_prompt(z, la, gn, bsz, t):
    dk, dv = GLA_CHUNK, 2 * GLA_CHUNK
    nc = t // GLA_CHUNK
    rows = lambda b, h, c: b * nc + c
    return pl.pallas_call(
        _gla_prompt_body,
        grid=(bsz, H_A, nc),
        in_specs=[pl.BlockSpec((GLA_CHUNK, dk), lambda b, h, c: (rows(b, h, c), h)),
                  pl.BlockSpec((GLA_CHUNK, dk), lambda b, h, c: (rows(b, h, c), H_A + h)),
                  pl.BlockSpec((GLA_CHUNK, dv), lambda b, h, c: (rows(b, h, c), H_A + h)),
                  pl.BlockSpec((GLA_CHUNK, dv), lambda b, h, c: (rows(b, h, c), 2 * H_A + h)),
                  pl.BlockSpec((GLA_CHUNK, dk), lambda b, h, c: (rows(b, h, c), h)),
                  pl.BlockSpec((1, dv), lambda b, h, c: (0, h))],
        out_specs=[pl.BlockSpec((GLA_CHUNK, dv), lambda b, h, c: (rows(b, h, c), h)),
                   pl.BlockSpec((1, 1, 1, dk, dv), lambda b, h, c: (0, b, h, 0, 0))],
        out_shape=[jax.ShapeDtypeStruct((bsz * t, H_A * dv), F32),
                   jax.ShapeDtypeStruct((1, bsz, H_A, dk, dv), F32)],
        scratch_shapes=[pltpu.VMEM((dk, dv), F32), pltpu.VMEM((GLA_CHUNK, dv), F32)],
        compiler_params=_cparams(("parallel", "parallel", "arbitrary")),
        name="gla_prompt",
    )(z, z, z, z, la, gn)


def _gla_sample_body(st_ref, lat_ref, kt_ref, qt_ref, v_ref, g_ref, gn_ref, o_ref, s_out_ref):
    dv = st_ref.shape[-1]
    for i in range(SAMPLE_BLOCK):
        for h in range(H_A):
            decay = jnp.exp(lat_ref[0, h, :, i:i + 1])
            v_row = v_ref[i:i + 1, h * dv:(h + 1) * dv]
            s_new = decay * st_ref[0, i, h] + kt_ref[0, h, :, i:i + 1] * v_row
            s_out_ref[0, i, h] = s_new
            q_col = qt_ref[0, h, :, i:i + 1] * (st_ref.shape[-2] ** -0.5)
            o = jnp.sum(q_col * s_new, axis=0, keepdims=True)
            o_ref[i:i + 1, h * dv:(h + 1) * dv] = _head_out(
                o, gn_ref[:, h * dv:(h + 1) * dv], g_ref[i:i + 1, h * dv:(h + 1) * dv])


def _gla_sample(state, lat, kt, qt, v, g, gn):
    _, ns, _, dk, dv = state.shape
    steps = ns // SAMPLE_BLOCK
    col_spec = pl.BlockSpec((1, H_A, dk, SAMPLE_BLOCK), lambda i: (i, 0, 0, 0))
    row_spec = pl.BlockSpec((SAMPLE_BLOCK, H_A * dv), lambda i: (i, 0))
    st_spec = pl.BlockSpec((1, SAMPLE_BLOCK, H_A, dk, dv), lambda i: (0, i, 0, 0, 0))
    return pl.pallas_call(
        _gla_sample_body,
        grid=(steps,),
        in_specs=[st_spec, col_spec, col_spec, col_spec, row_spec, row_spec,
                  pl.BlockSpec((1, H_A * dv), lambda i: (0, 0))],
        out_specs=[row_spec, st_spec],
        out_shape=[jax.ShapeDtypeStruct((ns, H_A * dv), F32), jax.ShapeDtypeStruct(state.shape, F32)],
        compiler_params=_cparams(("parallel",)),
        name="gla_sample",
    )(state, lat, kt, qt, v, g, gn)


def _fill_padded(pad_ref, hist, t, value):
    cols = pad_ref.shape[1]
    pad_ref[0:hist, :] = jnp.zeros((hist, cols), F32)
    pad_ref[hist:hist + t, :] = value
    pad_ref[hist + t:, :] = jnp.zeros((SUBLANES, cols), F32)


def _causal_taps(pad_ref, w_ref, base, hist, n_taps, acc):
    first = hist - (n_taps - 1)
    win = pad_ref[pl.ds(base, CONV_ROWS + hist + SUBLANES), :]
    for r in range(SUBLANES):
        taps = [w for w in range(n_taps) if (first + w) % SUBLANES == r]
        if not taps:
            continue
        shifted = win[r:r + CONV_ROWS + hist, :]
        for w in taps:
            lo = (first + w) // SUBLANES * SUBLANES
            acc = acc + shifted[lo:lo + CONV_ROWS, :] * w_ref[w:w + 1, :]
    return acc


def _conv_b_prompt_body(a_ref, b_ref, w_ref, bias_ref, o_ref, buf_ref, pad_ref):
    t = a_ref.shape[0]
    hist = 4 * SUBLANES
    _fill_padded(pad_ref, hist, t, a_ref[...] * _sigmoid(b_ref[...]))
    buf_ref[0, 0] = pad_ref[hist + t - (CONV_B_W - 1):hist + t, :]

    def step(i, carry):
        base = pl.multiple_of(i * CONV_ROWS, CONV_ROWS)
        acc = jnp.zeros((CONV_ROWS, pad_ref.shape[1]), F32) + bias_ref[...]
        o_ref[pl.ds(base, CONV_ROWS), :] = _causal_taps(pad_ref, w_ref, base, hist, CONV_B_W, acc)
        return carry

    lax.fori_loop(0, t // CONV_ROWS, step, 0)


def _conv_b_prompt(z, w, bias, bsz, t, col0, d_b):
    cb = 256
    nb = d_b // cb
    a0, b0 = col0 // cb, (col0 + d_b) // cb
    return pl.pallas_call(
        _conv_b_prompt_body,
        grid=(bsz, nb),
        in_specs=[pl.BlockSpec((t, cb), lambda b, j: (b, a0 + j)),
                  pl.BlockSpec((t, cb), lambda b, j: (b, b0 + j)),
                  pl.BlockSpec((CONV_B_W, cb), lambda b, j: (0, j)),
                  pl.BlockSpec((1, cb), lambda b, j: (0, j))],
        out_specs=[pl.BlockSpec((t, cb), lambda b, j: (b, j)),
                   pl.BlockSpec((1, 1, CONV_B_W - 1, cb), lambda b, j: (0, b, 0, j))],
        out_shape=[jax.ShapeDtypeStruct((bsz * t, d_b), F32),
                   jax.ShapeDtypeStruct((1, bsz, CONV_B_W - 1, d_b), F32)],
        scratch_shapes=[pltpu.VMEM((t + 5 * SUBLANES, cb), F32)],
        compiler_params=_cparams(("parallel", "parallel")),
        name="conv_b_prompt",
    )(z, z, w, bias)


def _conv_b_sample_body(buf_ref, a_ref, b_ref, w_ref, bias_ref, o_ref, nbuf_ref):
    d_b = a_ref.shape[1]
    glu = a_ref[...] * _sigmoid(b_ref[...])
    acc = bias_ref[...] + glu * w_ref[CONV_B_W - 1:CONV_B_W, :]
    for w in range(CONV_B_W - 1):
        acc = acc + buf_ref[:, w * d_b:(w + 1) * d_b] * w_ref[w:w + 1, :]
    o_ref[...] = acc
    nbuf_ref[:, :(CONV_B_W - 2) * d_b] = buf_ref[:, d_b:]
    nbuf_ref[:, (CONV_B_W - 2) * d_b:] = glu


def _conv_b_sample(buf, z, w, bias, row0, col0, d_b):
    ns = buf.shape[0]
    blk = 2 * SAMPLE_BLOCK
    r0 = row0 // blk
    return pl.pallas_call(
        _conv_b_sample_body,
        grid=(ns // blk,),
        in_specs=[pl.BlockSpec((blk, buf.shape[1]), lambda i: (i, 0)),
                  pl.BlockSpec((blk, d_b), lambda i: (r0 + i, col0 // d_b)),
                  pl.BlockSpec((blk, d_b), lambda i: (r0 + i, col0 // d_b + 1)),
                  pl.BlockSpec((CONV_B_W, d_b), lambda i: (0, 0)),
                  pl.BlockSpec((1, d_b), lambda i: (0, 0))],
        out_specs=[pl.BlockSpec((blk, d_b), lambda i: (i, 0)),
                   pl.BlockSpec((blk, buf.shape[1]), lambda i: (i, 0))],
        out_shape=[jax.ShapeDtypeStruct((ns, d_b), F32), jax.ShapeDtypeStruct(buf.shape, F32)],
        compiler_params=_cparams(("parallel",)),
        name="conv_b_sample",
    )(buf, z, z, w, bias)


def _conv_c_prompt_body(gb_ref, gc_ref, hv_ref, w_ref, o_ref, buf_ref, pad_ref):
    t = gb_ref.shape[0]
    hist = SUBLANES
    _fill_padded(pad_ref, hist, t, gc_ref[...] * hv_ref[...])
    buf_ref[0, 0] = pad_ref[hist + t - (CONV_C_W - 1):hist + t, :]

    def step(i, carry):
        base = pl.multiple_of(i * CONV_ROWS, CONV_ROWS)
        acc = jnp.zeros((CONV_ROWS, pad_ref.shape[1]), F32)
        acc = _causal_taps(pad_ref, w_ref, base, hist, CONV_C_W, acc)
        o_ref[pl.ds(base, CONV_ROWS), :] = acc * gb_ref[pl.ds(base, CONV_ROWS), :]
        return carry

    lax.fori_loop(0, t // CONV_ROWS, step, 0)


def _conv_c_prompt(z, w, bsz, t, d_c):
    cb = 512
    nb = d_c // cb
    return pl.pallas_call(
        _conv_c_prompt_body,
        grid=(bsz, nb),
        in_specs=[pl.BlockSpec((t, cb), lambda b, j: (b, j)),
                  pl.BlockSpec((t, cb), lambda b, j: (b, nb + j)),
                  pl.BlockSpec((t, cb), lambda b, j: (b, 2 * nb + j)),
                  pl.BlockSpec((CONV_C_W, cb), lambda b, j: (0, j))],
        out_specs=[pl.BlockSpec((t, cb), lambda b, j: (b, j)),
                   pl.BlockSpec((1, 1, CONV_C_W - 1, cb), lambda b, j: (0, b, 0, j))],
        out_shape=[jax.ShapeDtypeStruct((bsz * t, d_c), F32),
                   jax.ShapeDtypeStruct((1, bsz, CONV_C_W - 1, d_c), F32)],
        scratch_shapes=[pltpu.VMEM((t + 2 * SUBLANES, cb), F32)],
        compiler_params=_cparams(("parallel", "parallel")),
        name="conv_c_prompt",
    )(z, z, z, w)


def _conv_c_sample_body(buf_ref, gb_ref, gc_ref, hv_ref, w_ref, o_ref, nbuf_ref):
    d_c = gb_ref.shape[1]
    u = gc_ref[...] * hv_ref[...]
    c = buf_ref[:, :d_c] * w_ref[0:1, :] + buf_ref[:, d_c:] * w_ref[1:2, :] + u * w_ref[2:3, :]
    o_ref[...] = gb_ref[...] * c
    nbuf_ref[:, :d_c] = buf_ref[:, d_c:]
    nbuf_ref[:, d_c:] = u


def _conv_c_sample(buf, z, w, row0, d_c):
    ns = buf.shape[0]
    blk = 2 * SAMPLE_BLOCK
    r0 = row0 // blk
    return pl.pallas_call(
        _conv_c_sample_body,
        grid=(ns // blk,),
        in_specs=[pl.BlockSpec((blk, buf.shape[1]), lambda i: (i, 0)),
                  pl.BlockSpec((blk, d_c), lambda i: (r0 + i, 0)),
                  pl.BlockSpec((blk, d_c), lambda i: (r0 + i, 1)),
                  pl.BlockSpec((blk, d_c), lambda i: (r0 + i, 2)),
                  pl.BlockSpec((CONV_C_W, d_c), lambda i: (0, 0))],
        out_specs=[pl.BlockSpec((blk, d_c), lambda i: (i, 0)),
                   pl.BlockSpec((blk, buf.shape[1]), lambda i: (i, 0))],
        out_shape=[jax.ShapeDtypeStruct((ns, d_c), F32), jax.ShapeDtypeStruct(buf.shape, F32)],
        compiler_params=_cparams(("parallel",)),
        name="conv_c_sample",
    )(buf, z, z, z, w)


def _candidate_tables():
    pairs = [(a, b) for a in range(TOPK) for b in range(TOPK) if (a + 1) * (b + 1) <= TOPK]
    rows = -(-len(pairs) // SUBLANES) * SUBLANES
    p1 = np.zeros((rows, TOPK), np.float32)
    p2 = np.zeros((rows, TOPK), np.float32)
    neg = np.zeros((rows, LANES), np.float32)
    for r, (a, b) in enumerate(pairs):
        p1[r, a] = 1.0
        p2[r, b] = 1.0
    neg[len(pairs):] = -np.inf
    return p1, p2, neg, p1.T.copy()


def _take_top(work, index, count):
    rows = work.shape[0]
    slot = lax.broadcasted_iota(jnp.int32, (count, work.shape[1]), 0)
    vals = jnp.zeros((count, work.shape[1]), F32)
    rank = jnp.full(work.shape, float(count), F32)
    for a in range(count):
        m = jnp.max(work, axis=0, keepdims=True)
        first = jnp.min(jnp.where(work == m, index, float(rows)), axis=0, keepdims=True)
        sel = index == first
        rank = jnp.where(sel, float(a), rank)
        work = jnp.where(sel, -jnp.inf, work)
        vals = jnp.where(slot == a, m, vals)
    return vals, rank


def _route_body(q_ref, keys_ref, p1_ref, p2_ref, neg_ref, p1t_ref, r2_ref, e2_ref, c1_ref, b1_ref, q3_ref):
    for hp in range(2 * P_HEADS):
        q3_ref[hp] = q_ref[:, hp * LANES:(hp + 1) * LANES]
    key_idx = lax.broadcasted_iota(jnp.int32, (N_KEYS, LANES), 0).astype(F32)
    cand_idx = lax.broadcasted_iota(jnp.int32, (p1_ref.shape[0], LANES), 0).astype(F32)

    def head(h, carry):
        s1 = _dot_f32(keys_ref[h, 0], q3_ref[2 * h], ((1,), (1,)))
        s2 = _dot_f32(keys_ref[h, 1], q3_ref[2 * h + 1], ((1,), (1,)))
        v1, rank1 = _take_top(s1, key_idx, TOPK)
        v2, rank2 = _take_top(s2, key_idx, TOPK)
        cand = (_dot_f32(p1_ref[...], v1, ((1,), (0,))) + _dot_f32(p2_ref[...], v2, ((1,), (0,)))
                + neg_ref[...])
        _, crank = _take_top(cand, cand_idx, TOPK)
        chosen = crank < float(TOPK)
        cmax = v1[0:1, :] + v2[0:1, :]
        z = jnp.sum(jnp.where(chosen, jnp.exp(cand - cmax), 0.0), axis=0, keepdims=True)
        width = jnp.dot(p1t_ref[...], chosen.astype(F32), preferred_element_type=F32)
        b1 = jnp.zeros_like(rank1)
        for a in range(TOPK):
            b1 = jnp.where(rank1 == float(a), width[a:a + 1, :], b1)
        in1 = rank1 < float(TOPK)
        in2 = rank2 < float(TOPK)
        c1_ref[h] = jnp.where(in1, jnp.exp(jnp.where(in1, s1 - v1[0:1, :], 0.0)) / z, 0.0)
        b1_ref[h] = b1
        r2_ref[h] = rank2
        e2_ref[h] = jnp.where(in2, jnp.exp(jnp.where(in2, s2 - v2[0:1, :], 0.0)), 0.0)
        return carry

    lax.fori_loop(0, P_HEADS, head, 0)


def _route(q, keys):
    n = q.shape[0]
    p1, p2, neg, p1t = _candidate_tables()
    whole = lambda a: pl.BlockSpec(a.shape, lambda i: (0,) * a.ndim)
    out_spec = pl.BlockSpec((P_HEADS, N_KEYS, LANES), lambda i: (0, 0, i))
    out_shape = jax.ShapeDtypeStruct((P_HEADS, N_KEYS, n), F32)
    return pl.pallas_call(
        _route_body,
        grid=(n // LANES,),
        in_specs=[pl.BlockSpec((LANES, q.shape[1]), lambda i: (i, 0)), whole(keys),
                  whole(p1), whole(p2), whole(neg), whole(p1t)],
        out_specs=[out_spec] * 4,
        out_shape=[out_shape] * 4,
        scratch_shapes=[pltpu.VMEM((2 * P_HEADS, LANES, LANES), F32)],
        compiler_params=_cparams(("parallel",)),
        name="peer_route",
    )(q, keys, jnp.asarray(p1), jnp.asarray(p2), jnp.asarray(neg), jnp.asarray(p1t))


def _gelu_tanh(x):
    return 0.5 * x * (1.0 + jnp.tanh(np.float32(np.sqrt(2.0 / np.pi)) * (x + 0.044715 * (x * x * x))))


def _peer_dense_body(xt_ref, u_ref, vt_ref, r2_ref, e2_ref, c1_ref, b1_ref, o_ref):
    @pl.when(pl.program_id(1) == 0)
    def _():
        o_ref[...] = jnp.zeros_like(o_ref)

    act = _gelu_tanh(jnp.dot(u_ref[...], xt_ref[...], preferred_element_type=F32))
    parts = []
    for r in range(EXPERT_BLOCK // N_KEYS):
        gate = jnp.zeros((N_KEYS, xt_ref.shape[1]), F32)
        for h in range(P_HEADS):
            hit = r2_ref[h] < b1_ref[h, r:r + 1, :]
            gate = gate + c1_ref[h, r:r + 1, :] * jnp.where(hit, e2_ref[h], 0.0)
        parts.append((gate * act[r * N_KEYS:(r + 1) * N_KEYS, :]).astype(BF16))
    weighted = jnp.concatenate(parts, axis=0)
    o_ref[...] += jnp.dot(vt_ref[...], weighted, preferred_element_type=F32)


def _peer_dense(xt, u_b, vt_b, r2, e2, c1, b1):
    d, n = xt.shape
    n_exp = u_b.shape[0]
    keys_per_block = EXPERT_BLOCK // N_KEYS
    once = pl.Buffered(1)
    return pl.pallas_call(
        _peer_dense_body,
        grid=(n // TOK_BLOCK, n_exp // EXPERT_BLOCK),
        in_specs=[pl.BlockSpec((d, TOK_BLOCK), lambda i, e: (0, i), pipeline_mode=once),
                  pl.BlockSpec((EXPERT_BLOCK, d), lambda i, e: (e, 0)),
                  pl.BlockSpec((d, EXPERT_BLOCK), lambda i, e: (0, e)),
                  pl.BlockSpec((P_HEADS, N_KEYS, TOK_BLOCK), lambda i, e: (0, 0, i), pipeline_mode=once),
                  pl.BlockSpec((P_HEADS, N_KEYS, TOK_BLOCK), lambda i, e: (0, 0, i), pipeline_mode=once),
                  pl.BlockSpec((P_HEADS, keys_per_block, TOK_BLOCK), lambda i, e: (0, e, i)),
                  pl.BlockSpec((P_HEADS, keys_per_block, TOK_BLOCK), lambda i, e: (0, e, i))],
        out_specs=pl.BlockSpec((d, TOK_BLOCK), lambda i, e: (0, i)),
        out_shape=jax.ShapeDtypeStruct((d, n), F32),
        compiler_params=_cparams(("parallel", "arbitrary")),
        name="peer_dense",
    )(xt, u_b, vt_b, r2, e2, c1, b1)


def _peer_finish_body(final, acc_ref, x_ref, g_ref, o_ref):
    out = x_ref[...] + acc_ref[...].T
    o_ref[...] = _rmsnorm(out, g_ref[...]) if final else out


def _peer_finish(acc_t, x, g_final, final):
    d, n = acc_t.shape
    return pl.pallas_call(
        functools.partial(_peer_finish_body, final),
        grid=(n // TOK_BLOCK,),
        in_specs=[pl.BlockSpec((d, TOK_BLOCK), lambda i: (0, i)),
                  pl.BlockSpec((TOK_BLOCK, d), lambda i: (i, 0)),
                  pl.BlockSpec((1, d), lambda i: (0, 0))],
        out_specs=pl.BlockSpec((TOK_BLOCK, d), lambda i: (i, 0)),
        out_shape=jax.ShapeDtypeStruct((n, d), F32),
        compiler_params=_cparams(("parallel",)),
        name="peer_finish",
    )(acc_t, x, g_final)


def _peer(x, g_norm, wq, keys, table_u, table_v, g_final, final):
    q, xt = _norm_matmul_t(x, g_norm, wq.astype(BF16), 1024)
    r2, e2, c1, b1 = _route(q, keys)
    acc_t = _peer_dense(xt, table_u.astype(BF16), table_v.astype(BF16).T, r2, e2, c1, b1)
    return _peer_finish(acc_t, x, g_final, final)


def _col_form(a, n_dk):
    ns = a.shape[0]
    a = a.reshape(ns // SAMPLE_BLOCK, SAMPLE_BLOCK, H_A, n_dk)
    return a.transpose(0, 2, 3, 1)


def kernel(x_prompt, x_sample, state_gla, state_conv_b, state_conv_c, norm_mix, norm_ffn, norm_final, w_in_even, w_a2, b_a2, gla_norm_g, conv_b_w, conv_b_bias, conv_b_ln_g, conv_b_ln_b, w_out_even, w_in_odd, conv_c_w, w_out_odd, peer_wq, peer_keys, peer_u, peer_v):
    bsz, t, d = x_prompt.shape
    ns = x_sample.shape[0]
    n_prompt = bsz * t
    n_real = n_prompt + ns
    n_tok = -(-n_real // TOK_BLOCK) * TOK_BLOCK
    assert t % GLA_CHUNK == 0 and t % CONV_ROWS == 0 and ns % (2 * SAMPLE_BLOCK) == 0
    assert x_sample.shape[1] == 1 and n_prompt % (2 * SAMPLE_BLOCK) == 0

    d_a = d // 2
    dk_a = d_a // 2
    d_b = d - d_a
    rank = w_a2.shape[1]
    col_a = 2 * dk_a + 2 * d_a
    row = lambda a: a.reshape(1, -1)

    x = jnp.concatenate([x_prompt.reshape(n_prompt, d), x_sample.reshape(ns, d),
                         jnp.zeros((n_tok - n_real, d), F32)], axis=0)

    w_in = w_in_even[0]
    w_main = jnp.concatenate([w_in[:, :col_a], w_in[:, col_a + rank:]], axis=1).astype(BF16)
    w_gate = jnp.pad(w_in[:, col_a:col_a + rank], ((0, 0), (0, LANES - rank))).astype(BF16)
    w_a2p = jnp.pad(w_a2[0], ((0, LANES - rank), (0, 0)))
    z, la = _norm_matmul_gate(x, row(norm_mix[0]), w_main, w_gate, w_a2p, row(b_a2[0]), 1024)

    gn = row(gla_norm_g[0])
    oa_p, gla_p = _gla_prompt(z, la, gn, bsz, t)
    zs = z[n_prompt:n_real]
    las = la[n_prompt:n_real]
    oa_s, gla_s = _gla_sample(state_gla, _col_form(las, dk_a // H_A), _col_form(zs[:, dk_a:2 * dk_a], dk_a // H_A),
                              _col_form(zs[:, :dk_a], dk_a // H_A),
                              zs[:, 2 * dk_a:2 * dk_a + d_a], zs[:, 2 * dk_a + d_a:col_a], gn)

    ub_p, cb_p = _conv_b_prompt(z, conv_b_w[0], row(conv_b_bias[0]), bsz, t, col_a, d_b)
    ub_s, cb_s = _conv_b_sample(state_conv_b[0].reshape(ns, -1), z, conv_b_w[0], row(conv_b_bias[0]),
                                n_prompt, col_a, d_b)
    tail = jnp.zeros((n_tok - n_real, d_a), F32)
    oa = jnp.concatenate([oa_p, oa_s, tail], axis=0)
    ub = jnp.concatenate([ub_p, ub_s, tail], axis=0)
    x = _res_matmul_ln(oa, ub, row(conv_b_ln_g[0]), row(conv_b_ln_b[0]), w_out_even[0].astype(BF16), x, 1024)
    x = _peer(x, row(norm_ffn[0]), peer_wq[0], peer_keys[0], peer_u[0], peer_v[0], row(norm_final), False)

    z = _norm_matmul(x, row(norm_mix[1]), w_in_odd[0].astype(BF16), 1024)
    gc_p, cc_p = _conv_c_prompt(z, conv_c_w[0], bsz, t, d)
    gc_s, cc_s = _conv_c_sample(state_conv_c[0].reshape(ns, -1), z, conv_c_w[0], n_prompt, d)
    gc = jnp.concatenate([gc_p, gc_s, jnp.zeros((n_tok - n_real, d), F32)], axis=0)
    x = _res_matmul(gc, w_out_odd[0].astype(BF16), x, 1024)
    y = _peer(x, row(norm_ffn[1]), peer_wq[1], peer_keys[1], peer_u[1], peer_v[1], row(norm_final), True)

    y_prompt = y[:n_prompt].reshape(bsz, t, d)
    y_sample = y[n_prompt:n_real].reshape(ns, 1, d)
    return (y_prompt, y_sample, gla_p, cb_p, cc_p,
            gla_s, cb_s.reshape(state_conv_b.shape), cc_s.reshape(state_conv_c.shape))
```

```python
import functools

import numpy as np
import jax
import jax.numpy as jnp
from jax import lax
from jax.experimental import pallas as pl
from jax.experimental.pallas import tpu as pltpu

F32 = jnp.float32
BF16 = jnp.bfloat16
HIGHEST = lax.Precision.HIGHEST

EPS = 1e-6
GATE_TAU = 16.0
H_A = 4
P_HEADS = 8
N_KEYS = 128
TOPK = 16
CONV_B_W = 31
CONV_C_W = 3

LANES = 128
SUBLANES = 8
VMEM_LIMIT = 56 * 1024 * 1024

TOK_BLOCK = 768
EXPERT_BLOCK = 1024
PEER_LANES = 256
GLA_CHUNK = 128
GLA_SUB = 16
EXP_CLAMP = 80.0
CONV_ROWS = 64
SAMPLE_BLOCK = 8


def _cparams(sem):
    return pltpu.CompilerParams(dimension_semantics=sem, vmem_limit_bytes=VMEM_LIMIT)


def _rmsnorm(x, g):
    return x * lax.rsqrt(jnp.mean(x * x, axis=-1, keepdims=True) + EPS) * g


def _sigmoid(x):
    return jax.nn.sigmoid(x)


def _log_sigmoid(x):
    return jnp.minimum(x, 0.0) - jnp.log(1.0 + jnp.exp(-jnp.abs(x)))


def _norm_matmul_body(x_ref, g_ref, w_ref, o_ref, xn_ref):
    @pl.when(pl.program_id(1) == 0)
    def _():
        xn_ref[...] = _rmsnorm(x_ref[...], g_ref[...]).astype(BF16)

    o_ref[...] = jnp.dot(xn_ref[...], w_ref[...], preferred_element_type=F32)


def _norm_matmul(x, g, w, tn):
    n, d = x.shape
    m = w.shape[1]
    return pl.pallas_call(
        _norm_matmul_body,
        grid=(n // TOK_BLOCK, m // tn),
        in_specs=[pl.BlockSpec((TOK_BLOCK, d), lambda i, j: (i, 0)),
                  pl.BlockSpec((1, d), lambda i, j: (0, 0)),
                  pl.BlockSpec((d, tn), lambda i, j: (0, j))],
        out_specs=pl.BlockSpec((TOK_BLOCK, tn), lambda i, j: (i, j)),
        out_shape=jax.ShapeDtypeStruct((n, m), F32),
        scratch_shapes=[pltpu.VMEM((TOK_BLOCK, d), BF16)],
        compiler_params=_cparams(("parallel", "arbitrary")),
        name="norm_matmul",
    )(x, g, w)


def _norm_matmul_gate_body(x_ref, g_ref, w_ref, wa_ref, wa2_ref, ba2_ref, o_ref, la_ref, xn_ref):
    @pl.when(pl.program_id(1) == 0)
    def _():
        xn = _rmsnorm(x_ref[...], g_ref[...]).astype(BF16)
        xn_ref[...] = xn
        a_lr = jnp.dot(xn, wa_ref[...], preferred_element_type=F32)
        pre = jnp.dot(a_lr, wa2_ref[...], precision=HIGHEST, preferred_element_type=F32) + ba2_ref[...]
        la_ref[...] = _log_sigmoid(pre) / GATE_TAU

    o_ref[...] = jnp.dot(xn_ref[...], w_ref[...], preferred_element_type=F32)


def _norm_matmul_gate(x, g, w, wa, wa2, ba2, tn):
    n, d = x.shape
    m = w.shape[1]
    dk = wa2.shape[1]
    return pl.pallas_call(
        _norm_matmul_gate_body,
        grid=(n // TOK_BLOCK, m // tn),
        in_specs=[pl.BlockSpec((TOK_BLOCK, d), lambda i, j: (i, 0)),
                  pl.BlockSpec((1, d), lambda i, j: (0, 0)),
                  pl.BlockSpec((d, tn), lambda i, j: (0, j)),
                  pl.BlockSpec((d, LANES), lambda i, j: (0, 0)),
                  pl.BlockSpec((LANES, dk), lambda i, j: (0, 0)),
                  pl.BlockSpec((1, dk), lambda i, j: (0, 0))],
        out_specs=[pl.BlockSpec((TOK_BLOCK, tn), lambda i, j: (i, j)),
                   pl.BlockSpec((TOK_BLOCK, dk), lambda i, j: (i, 0))],
        out_shape=[jax.ShapeDtypeStruct((n, m), F32), jax.ShapeDtypeStruct((n, dk), F32)],
        scratch_shapes=[pltpu.VMEM((TOK_BLOCK, d), BF16)],
        compiler_params=_cparams(("parallel", "arbitrary")),
        name="norm_matmul_gate",
    )(x, g, w, wa, wa2, ba2)


def _norm_matmul_t_body(x_ref, g_ref, w_ref, o_ref, xt_ref, xn_ref):
    @pl.when(pl.program_id(1) == 0)
    def _():
        xn = _rmsnorm(x_ref[...], g_ref[...])
        xn_ref[...] = xn.astype(BF16)
        xt_ref[...] = xn.T.astype(BF16)

    o_ref[...] = jnp.dot(xn_ref[...], w_ref[...], preferred_element_type=F32)


def _norm_matmul_t(x, g, w, tn):
    n, d = x.shape
    m = w.shape[1]
    return pl.pallas_call(
        _norm_matmul_t_body,
        grid=(n // TOK_BLOCK, m // tn),
        in_specs=[pl.BlockSpec((TOK_BLOCK, d), lambda i, j: (i, 0)),
                  pl.BlockSpec((1, d), lambda i, j: (0, 0)),
                  pl.BlockSpec((d, tn), lambda i, j: (0, j))],
        out_specs=[pl.BlockSpec((TOK_BLOCK, tn), lambda i, j: (i, j)),
                   pl.BlockSpec((d, TOK_BLOCK), lambda i, j: (0, i))],
        out_shape=[jax.ShapeDtypeStruct((n, m), F32), jax.ShapeDtypeStruct((d, n), BF16)],
        scratch_shapes=[pltpu.VMEM((TOK_BLOCK, d), BF16)],
        compiler_params=_cparams(("parallel", "arbitrary")),
        name="norm_matmul_t",
    )(x, g, w)


def _res_matmul_body(a_ref, w_ref, x_ref, o_ref, ab_ref):
    @pl.when(pl.program_id(1) == 0)
    def _():
        ab_ref[...] = a_ref[...].astype(BF16)

    o_ref[...] = x_ref[...] + jnp.dot(ab_ref[...], w_ref[...], preferred_element_type=F32)


def _res_matmul(a, w, x, tn):
    n, k = a.shape
    m = w.shape[1]
    return pl.pallas_call(
        _res_matmul_body,
        grid=(n // TOK_BLOCK, m // tn),
        in_specs=[pl.BlockSpec((TOK_BLOCK, k), lambda i, j: (i, 0)),
                  pl.BlockSpec((k, tn), lambda i, j: (0, j)),
                  pl.BlockSpec((TOK_BLOCK, tn), lambda i, j: (i, j))],
        out_specs=pl.BlockSpec((TOK_BLOCK, tn), lambda i, j: (i, j)),
        out_shape=jax.ShapeDtypeStruct((n, m), F32),
        scratch_shapes=[pltpu.VMEM((TOK_BLOCK, k), BF16)],
        compiler_params=_cparams(("parallel", "arbitrary")),
        name="res_matmul",
    )(a, w, x)


def _res_matmul_ln_body(oa_ref, ub_ref, lg_ref, lb_ref, w_ref, x_ref, o_ref, ab_ref):
    da = oa_ref.shape[1]

    @pl.when(pl.program_id(1) == 0)
    def _():
        u = ub_ref[...]
        uc = u - jnp.mean(u, axis=-1, keepdims=True)
        y = uc * lax.rsqrt(jnp.mean(uc * uc, axis=-1, keepdims=True) + EPS) * lg_ref[...] + lb_ref[...]
        ab_ref[:, :da] = oa_ref[...].astype(BF16)
        ab_ref[:, da:] = (y * _sigmoid(y)).astype(BF16)

    o_ref[...] = x_ref[...] + jnp.dot(ab_ref[...], w_ref[...], preferred_element_type=F32)


def _res_matmul_ln(oa, ub, lg, lb, w, x, tn):
    n, da = oa.shape
    db = ub.shape[1]
    m = w.shape[1]
    return pl.pallas_call(
        _res_matmul_ln_body,
        grid=(n // TOK_BLOCK, m // tn),
        in_specs=[pl.BlockSpec((TOK_BLOCK, da), lambda i, j: (i, 0)),
                  pl.BlockSpec((TOK_BLOCK, db), lambda i, j: (i, 0)),
                  pl.BlockSpec((1, db), lambda i, j: (0, 0)),
                  pl.BlockSpec((1, db), lambda i, j: (0, 0)),
                  pl.BlockSpec((da + db, tn), lambda i, j: (0, j)),
                  pl.BlockSpec((TOK_BLOCK, tn), lambda i, j: (i, j))],
        out_specs=pl.BlockSpec((TOK_BLOCK, tn), lambda i, j: (i, j)),
        out_shape=jax.ShapeDtypeStruct((n, m), F32),
        scratch_shapes=[pltpu.VMEM((TOK_BLOCK, da + db), BF16)],
        compiler_params=_cparams(("parallel", "arbitrary")),
        name="res_matmul_ln",
    )(oa, ub, lg, lb, w, x)


def _dot_f32(a, b, dims):
    return lax.dot_general(a, b, (dims, ((), ())), precision=HIGHEST, preferred_element_type=F32)


def _head_out(o, gn, g):
    y = o * lax.rsqrt(jnp.mean(o * o, axis=-1, keepdims=True) + EPS) * gn
    return y * (g * _sigmoid(g))


def _gla_prompt_body(q_ref, k_ref, v_ref, g_ref, la_ref, gn_ref, o_ref, s_out_ref, s_ref, acc_ref):
    c = pl.program_id(2)
    dk = q_ref.shape[1]

    @pl.when(c == 0)
    def _():
        s_ref[...] = jnp.zeros_like(s_ref)

    la = la_ref[...]
    row = lax.broadcasted_iota(jnp.int32, (GLA_CHUNK, GLA_CHUNK), 0)
    col = lax.broadcasted_iota(jnp.int32, (GLA_CHUNK, GLA_CHUNK), 1)
    tri = (row >= col).astype(F32)
    b = _dot_f32(tri, la, ((1,), (0,)))
    b_last = b[GLA_CHUNK - 1:GLA_CHUNK, :]
    q = q_ref[...] * (dk ** -0.5)
    k = k_ref[...]
    v = v_ref[...]
    s = s_ref[...]

    acc_ref[...] = _dot_f32(q * jnp.exp(b), s, ((1,), (0,)))

    sub_row = lax.broadcasted_iota(jnp.int32, (GLA_SUB, GLA_SUB), 0)
    sub_col = lax.broadcasted_iota(jnp.int32, (GLA_SUB, GLA_SUB), 1)
    causal = sub_row >= sub_col
    for j in range(GLA_CHUNK // GLA_SUB):
        lo, hi = j * GLA_SUB, (j + 1) * GLA_SUB
        b_j = b[lo:hi, :]
        b_first = b[lo:lo + 1, :]
        b_end = b[hi - 1:hi, :]
        q_d = q[lo:hi, :] * jnp.exp(b_j - b_first)
        k_d = k[lo:hi, :] * jnp.exp(jnp.minimum(b_first - b_j, EXP_CLAMP))
        sc = jnp.where(causal, _dot_f32(q_d, k_d, ((1,), (1,))), 0.0)
        acc_ref[lo:hi, :] += _dot_f32(sc, v[lo:hi, :], ((1,), (0,)))
        if hi < GLA_CHUNK:
            k_p = k[lo:hi, :] * jnp.exp(b_end - b_j)
            q_p = q[hi:, :] * jnp.exp(b[hi:, :] - b_end)
            sc = _dot_f32(q_p, k_p, ((1,), (1,)))
            acc_ref[hi:, :] += _dot_f32(sc, v[lo:hi, :], ((1,), (0,)))

    decay = jnp.exp(jnp.broadcast_to(b_last, (dk, dk))).T
    k_e = k * jnp.exp(b_last - b)
    s_new = jnp.concatenate([decay] * (s.shape[1] // dk), axis=1) * s + _dot_f32(k_e.T, v, ((1,), (0,)))
    s_ref[...] = s_new

    @pl.when(c == pl.num_programs(2) - 1)
    def _():
        s_out_ref[0, 0, 0] = s_new

    o_ref[...] = _head_out(acc_ref[...], gn_ref[...], g_ref[...])


def _gla_prompt(z, la, gn, bsz, t):
    dk, dv = GLA_CHUNK, 2 * GLA_CHUNK
    nc = t // GLA_CHUNK
    rows = lambda b, h, c: b * nc + c
    return pl.pallas_call(
        _gla_prompt_body,
        grid=(bsz, H_A, nc),
        in_specs=[pl.BlockSpec((GLA_CHUNK, dk), lambda b, h, c: (rows(b, h, c), h)),
                  pl.BlockSpec((GLA_CHUNK, dk), lambda b, h, c: (rows(b, h, c), H_A + h)),
                  pl.BlockSpec((GLA_CHUNK, dv), lambda b, h, c: (rows(b, h, c), H_A + h)),
                  pl.BlockSpec((GLA_CHUNK, dv), lambda b, h, c: (rows(b, h, c), 2 * H_A + h)),
                  pl.BlockSpec((GLA_CHUNK, dk), lambda b, h, c: (rows(b, h, c), h)),
                  pl.BlockSpec((1, dv), lambda b, h, c: (0, h))],
        out_specs=[pl.BlockSpec((GLA_CHUNK, dv), lambda b, h, c: (rows(b, h, c), h)),
                   pl.BlockSpec((1, 1, 1, dk, dv), lambda b, h, c: (0, b, h, 0, 0))],
        out_shape=[jax.ShapeDtypeStruct((bsz * t, H_A * dv), F32),
                   jax.ShapeDtypeStruct((1, bsz, H_A, dk, dv), F32)],
        scratch_shapes=[pltpu.VMEM((dk, dv), F32), pltpu.VMEM((GLA_CHUNK, dv), F32)],
        compiler_params=_cparams(("parallel", "parallel", "arbitrary")),
        name="gla_prompt",
    )(z, z, z, z, la, gn)


def _gla_sample_body(st_ref, lat_ref, kt_ref, qt_ref, v_ref, g_ref, gn_ref, o_ref, s_out_ref):
    dv = st_ref.shape[-1]
    for i in range(SAMPLE_BLOCK):
        for h in range(H_A):
            decay = jnp.exp(lat_ref[0, h, :, i:i + 1])
            v_row = v_ref[i:i + 1, h * dv:(h + 1) * dv]
            s_new = decay * st_ref[0, i, h] + kt_ref[0, h, :, i:i + 1] * v_row
            s_out_ref[0, i, h] = s_new
            q_col = qt_ref[0, h, :, i:i + 1] * (st_ref.shape[-2] ** -0.5)
            o = jnp.sum(q_col * s_new, axis=0, keepdims=True)
            o_ref[i:i + 1, h * dv:(h + 1) * dv] = _head_out(
                o, gn_ref[:, h * dv:(h + 1) * dv], g_ref[i:i + 1, h * dv:(h + 1) * dv])


def _gla_sample(state, lat, kt, qt, v, g, gn):
    _, ns, _, dk, dv = state.shape
    steps = ns // SAMPLE_BLOCK
    col_spec = pl.BlockSpec((1, H_A, dk, SAMPLE_BLOCK), lambda i: (i, 0, 0, 0))
    row_spec = pl.BlockSpec((SAMPLE_BLOCK, H_A * dv), lambda i: (i, 0))
    st_spec = pl.BlockSpec((1, SAMPLE_BLOCK, H_A, dk, dv), lambda i: (0, i, 0, 0, 0))
    return pl.pallas_call(
        _gla_sample_body,
        grid=(steps,),
        in_specs=[st_spec, col_spec, col_spec, col_spec, row_spec, row_spec,
                  pl.BlockSpec((1, H_A * dv), lambda i: (0, 0))],
        out_specs=[row_spec, st_spec],
        out_shape=[jax.ShapeDtypeStruct((ns, H_A * dv), F32), jax.ShapeDtypeStruct(state.shape, F32)],
        compiler_params=_cparams(("parallel",)),
        name="gla_sample",
    )(state, lat, kt, qt, v, g, gn)


def _fill_padded(pad_ref, hist, t, value):
    cols = pad_ref.shape[1]
    pad_ref[0:hist, :] = jnp.zeros((hist, cols), F32)
    pad_ref[hist:hist + t, :] = value
    pad_ref[hist + t:, :] = jnp.zeros((SUBLANES, cols), F32)


def _causal_taps(pad_ref, w_ref, base, hist, n_taps, acc):
    first = hist - (n_taps - 1)
    win = pad_ref[pl.ds(base, CONV_ROWS + hist + SUBLANES), :]
    for r in range(SUBLANES):
        taps = [w for w in range(n_taps) if (first + w) % SUBLANES == r]
        if not taps:
            continue
        shifted = win[r:r + CONV_ROWS + hist, :]
        for w in taps:
            lo = (first + w) // SUBLANES * SUBLANES
            acc = acc + shifted[lo:lo + CONV_ROWS, :] * w_ref[w:w + 1, :]
    return acc


def _conv_b_prompt_body(a_ref, b_ref, w_ref, bias_ref, o_ref, buf_ref, pad_ref):
    t = a_ref.shape[0]
    hist = 4 * SUBLANES
    _fill_padded(pad_ref, hist, t, a_ref[...] * _sigmoid(b_ref[...]))
    buf_ref[0, 0] = pad_ref[hist + t - (CONV_B_W - 1):hist + t, :]

    def step(i, carry):
        base = pl.multiple_of(i * CONV_ROWS, CONV_ROWS)
        acc = jnp.zeros((CONV_ROWS, pad_ref.shape[1]), F32) + bias_ref[...]
        o_ref[pl.ds(base, CONV_ROWS), :] = _causal_taps(pad_ref, w_ref, base, hist, CONV_B_W, acc)
        return carry

    lax.fori_loop(0, t // CONV_ROWS, step, 0)


def _conv_b_prompt(z, w, bias, bsz, t, col0, d_b):
    cb = 256
    nb = d_b // cb
    a0, b0 = col0 // cb, (col0 + d_b) // cb
    return pl.pallas_call(
        _conv_b_prompt_body,
        grid=(bsz, nb),
        in_specs=[pl.BlockSpec((t, cb), lambda b, j: (b, a0 + j)),
                  pl.BlockSpec((t, cb), lambda b, j: (b, b0 + j)),
                  pl.BlockSpec((CONV_B_W, cb), lambda b, j: (0, j)),
                  pl.BlockSpec((1, cb), lambda b, j: (0, j))],
        out_specs=[pl.BlockSpec((t, cb), lambda b, j: (b, j)),
                   pl.BlockSpec((1, 1, CONV_B_W - 1, cb), lambda b, j: (0, b, 0, j))],
        out_shape=[jax.ShapeDtypeStruct((bsz * t, d_b), F32),
                   jax.ShapeDtypeStruct((1, bsz, CONV_B_W - 1, d_b), F32)],
        scratch_shapes=[pltpu.VMEM((t + 5 * SUBLANES, cb), F32)],
        compiler_params=_cparams(("parallel", "parallel")),
        name="conv_b_prompt",
    )(z, z, w, bias)


def _conv_b_sample_body(buf_ref, a_ref, b_ref, w_ref, bias_ref, o_ref, nbuf_ref):
    d_b = a_ref.shape[1]
    glu = a_ref[...] * _sigmoid(b_ref[...])
    acc = bias_ref[...] + glu * w_ref[CONV_B_W - 1:CONV_B_W, :]
    for w in range(CONV_B_W - 1):
        acc = acc + buf_ref[:, w * d_b:(w + 1) * d_b] * w_ref[w:w + 1, :]
    o_ref[...] = acc
    nbuf_ref[:, :(CONV_B_W - 2) * d_b] = buf_ref[:, d_b:]
    nbuf_ref[:, (CONV_B_W - 2) * d_b:] = glu


def _conv_b_sample(buf, z, w, bias, row0, col0, d_b):
    ns = buf.shape[0]
    blk = 2 * SAMPLE_BLOCK
    r0 = row0 // blk
    return pl.pallas_call(
        _conv_b_sample_body,
        grid=(ns // blk,),
        in_specs=[pl.BlockSpec((blk, buf.shape[1]), lambda i: (i, 0)),
                  pl.BlockSpec((blk, d_b), lambda i: (r0 + i, col0 // d_b)),
                  pl.BlockSpec((blk, d_b), lambda i: (r0 + i, col0 // d_b + 1)),
                  pl.BlockSpec((CONV_B_W, d_b), lambda i: (0, 0)),
                  pl.BlockSpec((1, d_b), lambda i: (0, 0))],
        out_specs=[pl.BlockSpec((blk, d_b), lambda i: (i, 0)),
                   pl.BlockSpec((blk, buf.shape[1]), lambda i: (i, 0))],
        out_shape=[jax.ShapeDtypeStruct((ns, d_b), F32), jax.ShapeDtypeStruct(buf.shape, F32)],
        compiler_params=_cparams(("parallel",)),
        name="conv_b_sample",
    )(buf, z, z, w, bias)


def _conv_c_prompt_body(gb_ref, gc_ref, hv_ref, w_ref, o_ref, buf_ref, pad_ref):
    t = gb_ref.shape[0]
    hist = SUBLANES
    _fill_padded(pad_ref, hist, t, gc_ref[...] * hv_ref[...])
    buf_ref[0, 0] = pad_ref[hist + t - (CONV_C_W - 1):hist + t, :]

    def step(i, carry):
        base = pl.multiple_of(i * CONV_ROWS, CONV_ROWS)
        acc = jnp.zeros((CONV_ROWS, pad_ref.shape[1]), F32)
        acc = _causal_taps(pad_ref, w_ref, base, hist, CONV_C_W, acc)
        o_ref[pl.ds(base, CONV_ROWS), :] = acc * gb_ref[pl.ds(base, CONV_ROWS), :]
        return carry

    lax.fori_loop(0, t // CONV_ROWS, step, 0)


def _conv_c_prompt(z, w, bsz, t, d_c):
    cb = 512
    nb = d_c // cb
    return pl.pallas_call(
        _conv_c_prompt_body,
        grid=(bsz, nb),
        in_specs=[pl.BlockSpec((t, cb), lambda b, j: (b, j)),
                  pl.BlockSpec((t, cb), lambda b, j: (b, nb + j)),
                  pl.BlockSpec((t, cb), lambda b, j: (b, 2 * nb + j)),
                  pl.BlockSpec((CONV_C_W, cb), lambda b, j: (0, j))],
        out_specs=[pl.BlockSpec((t, cb), lambda b, j: (b, j)),
                   pl.BlockSpec((1, 1, CONV_C_W - 1, cb), lambda b, j: (0, b, 0, j))],
        out_shape=[jax.ShapeDtypeStruct((bsz * t, d_c), F32),
                   jax.ShapeDtypeStruct((1, bsz, CONV_C_W - 1, d_c), F32)],
        scratch_shapes=[pltpu.VMEM((t + 2 * SUBLANES, cb), F32)],
        compiler_params=_cparams(("parallel", "parallel")),
        name="conv_c_prompt",
    )(z, z, z, w)


def _conv_c_sample_body(buf_ref, gb_ref, gc_ref, hv_ref, w_ref, o_ref, nbuf_ref):
    d_c = gb_ref.shape[1]
    u = gc_ref[...] * hv_ref[...]
    c = buf_ref[:, :d_c] * w_ref[0:1, :] + buf_ref[:, d_c:] * w_ref[1:2, :] + u * w_ref[2:3, :]
    o_ref[...] = gb_ref[...] * c
    nbuf_ref[:, :d_c] = buf_ref[:, d_c:]
    nbuf_ref[:, d_c:] = u


def _conv_c_sample(buf, z, w, row0, d_c):
    ns = buf.shape[0]
    blk = 2 * SAMPLE_BLOCK
    r0 = row0 // blk
    return pl.pallas_call(
        _conv_c_sample_body,
        grid=(ns // blk,),
        in_specs=[pl.BlockSpec((blk, buf.shape[1]), lambda i: (i, 0)),
                  pl.BlockSpec((blk, d_c), lambda i: (r0 + i, 0)),
                  pl.BlockSpec((blk, d_c), lambda i: (r0 + i, 1)),
                  pl.BlockSpec((blk, d_c), lambda i: (r0 + i, 2)),
                  pl.BlockSpec((CONV_C_W, d_c), lambda i: (0, 0))],
        out_specs=[pl.BlockSpec((blk, d_c), lambda i: (i, 0)),
                   pl.BlockSpec((blk, buf.shape[1]), lambda i: (i, 0))],
        out_shape=[jax.ShapeDtypeStruct((ns, d_c), F32), jax.ShapeDtypeStruct(buf.shape, F32)],
        compiler_params=_cparams(("parallel",)),
        name="conv_c_sample",
    )(buf, z, z, z, w)


def _candidate_tables():
    pairs = [(a, b) for a in range(TOPK) for b in range(TOPK) if (a + 1) * (b + 1) <= TOPK]
    rows = -(-len(pairs) // SUBLANES) * SUBLANES
    p1 = np.zeros((rows, TOPK), np.float32)
    p2 = np.zeros((rows, TOPK), np.float32)
    neg = np.zeros((rows, LANES), np.float32)
    for r, (a, b) in enumerate(pairs):
        p1[r, a] = 1.0
        p2[r, b] = 1.0
    neg[len(pairs):] = -np.inf
    return p1, p2, neg, p1.T.copy()


def _take_top(works, index, count, tie_break):
    rows, lanes = works[0].shape
    slot = lax.broadcasted_iota(jnp.int32, (count, lanes), 0)
    works = list(works)
    vals = [jnp.zeros((count, lanes), F32) for _ in works]
    ranks = [jnp.full((rows, lanes), float(count), F32) for _ in works]
    for a in range(count):
        for i, work in enumerate(works):
            m = jnp.max(work, axis=0, keepdims=True)
            sel = work == m
            if tie_break:
                first = jnp.min(jnp.where(sel, index, float(rows)), axis=0, keepdims=True)
                sel = index == first
            ranks[i] = jnp.where(sel, float(a), ranks[i])
            works[i] = jnp.where(sel, -jnp.inf, work)
            vals[i] = jnp.where(slot == a, m, vals[i])
    return vals, ranks


def _count(mask):
    return jnp.sum(mask.astype(F32), axis=0, keepdims=True)


def _route_heads(s1, s2, p1, p2, neg, p1t, key_idx, cand_idx, tie_break):
    n = len(s1)
    vals, ranks = _take_top(list(s1) + list(s2), key_idx, TOPK, tie_break)
    v1, v2, rank1, rank2 = vals[:n], vals[n:], ranks[:n], ranks[n:]
    cand = [_dot_f32(p1, v1[i], ((1,), (0,))) + _dot_f32(p2, v2[i], ((1,), (0,))) + neg for i in range(n)]
    _, crank = _take_top(cand, cand_idx, TOPK, tie_break)
    out = []
    for i in range(n):
        chosen = crank[i] < float(TOPK)
        cmax = v1[i][0:1, :] + v2[i][0:1, :]
        z = jnp.sum(jnp.where(chosen, jnp.exp(cand[i] - cmax), 0.0), axis=0, keepdims=True)
        width = jnp.dot(p1t, chosen.astype(F32), preferred_element_type=F32)
        b1 = jnp.zeros_like(rank1[i])
        for a in range(TOPK):
            b1 = jnp.where(rank1[i] == float(a), width[a:a + 1, :], b1)
        in1 = rank1[i] < float(TOPK)
        in2 = rank2[i] < float(TOPK)
        c1 = jnp.where(in1, jnp.exp(jnp.where(in1, s1[i] - v1[i][0:1, :], 0.0)) / z, 0.0)
        e2 = jnp.where(in2, jnp.exp(jnp.where(in2, s2[i] - v2[i][0:1, :], 0.0)), 0.0)
        clean = ((_count(in1) == float(TOPK)) & (_count(in2) == float(TOPK))
                 & (_count(chosen) == float(TOPK)))
        out.append((c1, b1, rank2[i], e2, clean))
    return out


ROUTE_HEADS = 4


def _route_body(q_ref, keys_ref, p1_ref, p2_ref, neg_ref, p1t_ref, r2_ref, e2_ref, c1_ref, b1_ref, q3_ref):
    for hp in range(2 * P_HEADS):
        q3_ref[hp] = q_ref[:, hp * LANES:(hp + 1) * LANES]
    key_idx = lax.broadcasted_iota(jnp.int32, (N_KEYS, LANES), 0).astype(F32)
    cand_idx = lax.broadcasted_iota(jnp.int32, (p1_ref.shape[0], LANES), 0).astype(F32)

    def emit(heads, tie_break):
        s1 = [_dot_f32(keys_ref[h, 0], q3_ref[2 * h], ((1,), (1,))) for h in heads]
        s2 = [_dot_f32(keys_ref[h, 1], q3_ref[2 * h + 1], ((1,), (1,))) for h in heads]
        rows = _route_heads(s1, s2, p1_ref[...], p2_ref[...], neg_ref[...], p1t_ref[...],
                            key_idx, cand_idx, tie_break)
        all_clean = None
        for h, (c1, b1, r2, e2, clean) in zip(heads, rows):
            c1_ref[h] = c1
            b1_ref[h] = b1
            r2_ref[h] = r2.astype(r2_ref.dtype)
            e2_ref[h] = e2.astype(e2_ref.dtype)
            all_clean = clean if all_clean is None else all_clean & clean
        return all_clean

    def step(i, carry):
        heads = [i * ROUTE_HEADS + j for j in range(ROUTE_HEADS)]
        all_clean = emit(heads, False)

        @pl.when(jnp.min(all_clean.astype(F32)) < 0.5)
        def _():
            emit(heads, True)

        return carry

    lax.fori_loop(0, P_HEADS // ROUTE_HEADS, step, 0)


def _route(q, keys):
    n = q.shape[0]
    p1, p2, neg, p1t = _candidate_tables()
    whole = lambda a: pl.BlockSpec(a.shape, lambda i: (0,) * a.ndim)
    out_spec = pl.BlockSpec((P_HEADS, N_KEYS, LANES), lambda i: (0, 0, i))
    shape = lambda dt: jax.ShapeDtypeStruct((P_HEADS, N_KEYS, n), dt)
    return pl.pallas_call(
        _route_body,
        grid=(n // LANES,),
        in_specs=[pl.BlockSpec((LANES, q.shape[1]), lambda i: (i, 0)), whole(keys),
                  whole(p1), whole(p2), whole(neg), whole(p1t)],
        out_specs=[out_spec] * 4,
        out_shape=[shape(BF16), shape(BF16), shape(F32), shape(F32)],
        scratch_shapes=[pltpu.VMEM((2 * P_HEADS, LANES, LANES), F32)],
        compiler_params=_cparams(("parallel",)),
        name="peer_route",
    )(q, keys, jnp.asarray(p1), jnp.asarray(p2), jnp.asarray(neg), jnp.asarray(p1t))


def _bf16_pair(c):
    hi = float(np.asarray(c, dtype=BF16))
    return hi, float(np.asarray(c - hi, dtype=BF16))


def _gelu_tanh(x):
    c0 = float(np.sqrt(2.0 / np.pi))
    c0_hi, c0_lo = _bf16_pair(c0)
    c1_hi, c1_lo = _bf16_pair(c0 * 0.044715)
    x2 = x * x
    inner = x * ((c1_hi * x2 + c0_hi) + (c1_lo * x2 + c0_lo))
    half = 0.5 * x
    return half + half * jnp.tanh(inner)


def _peer_dense_body(final, xt_ref, u_ref, vt_ref, r2_ref, e2_ref, c1_ref, b1_ref, x_ref, gf_ref, o_ref,
                     acc_ref):
    e = pl.program_id(1)
    n_groups = xt_ref.shape[1] // PEER_LANES

    @pl.when(e == 0)
    def _():
        acc_ref[...] = jnp.zeros_like(acc_ref)

    def scores(j):
        cols = slice(j * PEER_LANES, (j + 1) * PEER_LANES)
        return jnp.dot(u_ref[0], xt_ref[:, cols], preferred_element_type=F32)

    def weigh(s, j):
        cols = slice(j * PEER_LANES, (j + 1) * PEER_LANES)
        act = _gelu_tanh(s.astype(BF16))
        parts = []
        for r in range(EXPERT_BLOCK // N_KEYS):
            gate = jnp.zeros((N_KEYS, PEER_LANES), BF16)
            for h in range(P_HEADS):
                hit = r2_ref[h, :, cols] < b1_ref[h, r:r + 1, cols].astype(BF16)
                gate = gate + c1_ref[h, r:r + 1, cols].astype(BF16) * jnp.where(
                    hit, e2_ref[h, :, cols], jnp.zeros((), BF16))
            parts.append(gate * act[r * N_KEYS:(r + 1) * N_KEYS, :])
        return jnp.concatenate(parts, axis=0)

    s_next = scores(0)
    for j in range(n_groups):
        s = s_next
        if j + 1 < n_groups:
            s_next = scores(j + 1)
        cols = slice(j * PEER_LANES, (j + 1) * PEER_LANES)
        acc_ref[:, cols] += jnp.dot(vt_ref[0], weigh(s, j), preferred_element_type=F32)

    @pl.when(e == pl.num_programs(1) - 1)
    def _():
        for j in range(n_groups):
            rows = slice(j * PEER_LANES, (j + 1) * PEER_LANES)
            out = x_ref[rows, :] + acc_ref[:, rows].T
            o_ref[rows, :] = _rmsnorm(out, gf_ref[...]) if final else out


def _peer_dense(xt, u_b, vt_b, layer, r2, e2, c1, b1, x, g_final, final):
    d, n = xt.shape
    n_exp = u_b.shape[1]
    keys_per_block = EXPERT_BLOCK // N_KEYS
    once = pl.Buffered(1)
    rank_spec = pl.BlockSpec((P_HEADS, N_KEYS, TOK_BLOCK), lambda i, e: (0, 0, i), pipeline_mode=once)
    first_spec = pl.BlockSpec((P_HEADS, keys_per_block, TOK_BLOCK), lambda i, e: (0, e, i))
    return pl.pallas_call(
        functools.partial(_peer_dense_body, final),
        grid=(n // TOK_BLOCK, n_exp // EXPERT_BLOCK),
        in_specs=[pl.BlockSpec((d, TOK_BLOCK), lambda i, e: (0, i), pipeline_mode=once),
                  pl.BlockSpec((1, EXPERT_BLOCK, d), lambda i, e: (layer, e, 0)),
                  pl.BlockSpec((1, d, EXPERT_BLOCK), lambda i, e: (layer, 0, e)),
                  rank_spec, rank_spec, first_spec, first_spec,
                  pl.BlockSpec((TOK_BLOCK, d), lambda i, e: (i, 0), pipeline_mode=once),
                  pl.BlockSpec((1, d), lambda i, e: (0, 0))],
        out_specs=pl.BlockSpec((TOK_BLOCK, d), lambda i, e: (i, 0), pipeline_mode=once),
        out_shape=jax.ShapeDtypeStruct((n, d), F32),
        scratch_shapes=[pltpu.VMEM((d, TOK_BLOCK), F32)],
        compiler_params=_cparams(("parallel", "arbitrary")),
        name="peer_dense_final" if final else "peer_dense",
    )(xt, u_b, vt_b, r2, e2, c1, b1, x, g_final)


def _peer(x, g_norm, wq, keys, u_b, v_b, layer, g_final, final):
    q, xt = _norm_matmul_t(x, g_norm, wq, 1024)
    r2, e2, c1, b1 = _route(q, keys)
    return _peer_dense(xt, u_b, v_b, layer, r2, e2, c1, b1, x, g_final, final)


def _col_form(a, n_dk):
    ns = a.shape[0]
    a = a.reshape(ns // SAMPLE_BLOCK, SAMPLE_BLOCK, H_A, n_dk)
    return a.transpose(0, 2, 3, 1)


def kernel(x_prompt, x_sample, state_gla, state_conv_b, state_conv_c, norm_mix, norm_ffn, norm_final, w_in_even, w_a2, b_a2, gla_norm_g, conv_b_w, conv_b_bias, conv_b_ln_g, conv_b_ln_b, w_out_even, w_in_odd, conv_c_w, w_out_odd, peer_wq, peer_keys, peer_u, peer_v):
    bsz, t, d = x_prompt.shape
    ns = x_sample.shape[0]
    n_prompt = bsz * t
    n_real = n_prompt + ns
    n_tok = -(-n_real // TOK_BLOCK) * TOK_BLOCK
    assert t % GLA_CHUNK == 0 and t % CONV_ROWS == 0 and ns % (2 * SAMPLE_BLOCK) == 0
    assert x_sample.shape[1] == 1 and n_prompt % (2 * SAMPLE_BLOCK) == 0

    d_a = d // 2
    dk_a = d_a // 2
    d_b = d - d_a
    rank = w_a2.shape[1]
    col_a = 2 * dk_a + 2 * d_a
    row = lambda a: a.reshape(1, -1)

    x = jnp.concatenate([x_prompt.reshape(n_prompt, d), x_sample.reshape(ns, d),
                         jnp.zeros((n_tok - n_real, d), F32)], axis=0)

    w_in = w_in_even[0]
    w_main = jnp.concatenate([w_in[:, :col_a], w_in[:, col_a + rank:]], axis=1).astype(BF16)
    w_gate = jnp.pad(w_in[:, col_a:col_a + rank], ((0, 0), (0, LANES - rank))).astype(BF16)
    w_a2p = jnp.pad(w_a2[0], ((0, LANES - rank), (0, 0)))
    z, la = _norm_matmul_gate(x, row(norm_mix[0]), w_main, w_gate, w_a2p, row(b_a2[0]), 1024)

    gn = row(gla_norm_g[0])
    oa_p, gla_p = _gla_prompt(z, la, gn, bsz, t)
    zs = z[n_prompt:n_real]
    las = la[n_prompt:n_real]
    oa_s, gla_s = _gla_sample(state_gla, _col_form(las, dk_a // H_A), _col_form(zs[:, dk_a:2 * dk_a], dk_a // H_A),
                              _col_form(zs[:, :dk_a], dk_a // H_A),
                              zs[:, 2 * dk_a:2 * dk_a + d_a], zs[:, 2 * dk_a + d_a:col_a], gn)

    ub_p, cb_p = _conv_b_prompt(z, conv_b_w[0], row(conv_b_bias[0]), bsz, t, col_a, d_b)
    ub_s, cb_s = _conv_b_sample(state_conv_b[0].reshape(ns, -1), z, conv_b_w[0], row(conv_b_bias[0]),
                                n_prompt, col_a, d_b)
    tail = jnp.zeros((n_tok - n_real, d_a), F32)
    oa = jnp.concatenate([oa_p, oa_s, tail], axis=0)
    ub = jnp.concatenate([ub_p, ub_s, tail], axis=0)
    x = _res_matmul_ln(oa, ub, row(conv_b_ln_g[0]), row(conv_b_ln_b[0]), w_out_even[0].astype(BF16), x, 1024)
    wq_b = peer_wq.astype(BF16)
    u_b = peer_u.astype(BF16)
    v_b = peer_v.astype(BF16).transpose(0, 2, 1)
    x = _peer(x, row(norm_ffn[0]), wq_b[0], peer_keys[0], u_b, v_b, 0, row(norm_final), False)

    z = _norm_matmul(x, row(norm_mix[1]), w_in_odd[0].astype(BF16), 1024)
    gc_p, cc_p = _conv_c_prompt(z, conv_c_w[0], bsz, t, d)
    gc_s, cc_s = _conv_c_sample(state_conv_c[0].reshape(ns, -1), z, conv_c_w[0], n_prompt, d)
    gc = jnp.concatenate([gc_p, gc_s, jnp.zeros((n_tok - n_real, d), F32)], axis=0)
    x = _res_matmul(gc, w_out_odd[0].astype(BF16), x, 1024)
    y = _peer(x, row(norm_ffn[1]), wq_b[1], peer_keys[1], u_b, v_b, 1, row(norm_final), True)

    y_prompt = y[:n_prompt].reshape(bsz, t, d)
    y_sample = y[n_prompt:n_real].reshape(ns, 1, d)
    return (y_prompt, y_sample, gla_p, cb_p, cc_p,
            gla_s, cb_s.reshape(state_conv_b.shape), cc_s.reshape(state_conv_c.shape))
```

```python
import functools

import numpy as np
import jax
import jax.numpy as jnp
from jax import lax
from jax.experimental import pallas as pl
from jax.experimental.pallas import tpu as pltpu

F32 = jnp.float32
BF16 = jnp.bfloat16
HIGHEST = lax.Precision.HIGHEST

EPS = 1e-6
GATE_TAU = 16.0
H_A = 4
P_HEADS = 8
N_KEYS = 128
TOPK = 16
CONV_B_W = 31
CONV_C_W = 3

LANES = 128
SUBLANES = 8
VMEM_LIMIT = 56 * 1024 * 1024

TOK_BLOCK = 768
EXPERT_BLOCK = 1024
PEER_LANES = 256
TABLE_SLAB = 256
GLA_CHUNK = 128
GLA_SUB = 16
EXP_CLAMP = 80.0
CONV_ROWS = 64
SAMPLE_BLOCK = 8


def _cparams(sem):
    return pltpu.CompilerParams(dimension_semantics=sem, vmem_limit_bytes=VMEM_LIMIT)


def _rmsnorm(x, g):
    return x * lax.rsqrt(jnp.mean(x * x, axis=-1, keepdims=True) + EPS) * g


def _sigmoid(x):
    return jax.nn.sigmoid(x)


def _log_sigmoid(x):
    return jnp.minimum(x, 0.0) - jnp.log(1.0 + jnp.exp(-jnp.abs(x)))


def _norm_matmul_body(x_ref, g_ref, w_ref, o_ref, xn_ref):
    @pl.when(pl.program_id(1) == 0)
    def _():
        xn_ref[...] = _rmsnorm(x_ref[...], g_ref[...]).astype(BF16)

    o_ref[...] = jnp.dot(xn_ref[...], w_ref[...], preferred_element_type=F32)


def _norm_matmul(x, g, w, tn):
    n, d = x.shape
    m = w.shape[1]
    return pl.pallas_call(
        _norm_matmul_body,
        grid=(n // TOK_BLOCK, m // tn),
        in_specs=[pl.BlockSpec((TOK_BLOCK, d), lambda i, j: (i, 0)),
                  pl.BlockSpec((1, d), lambda i, j: (0, 0)),
                  pl.BlockSpec((d, tn), lambda i, j: (0, j))],
        out_specs=pl.BlockSpec((TOK_BLOCK, tn), lambda i, j: (i, j)),
        out_shape=jax.ShapeDtypeStruct((n, m), F32),
        scratch_shapes=[pltpu.VMEM((TOK_BLOCK, d), BF16)],
        compiler_params=_cparams(("parallel", "arbitrary")),
        name="norm_matmul",
    )(x, g, w)


def _norm_matmul_gate_body(x_ref, g_ref, w_ref, wa_ref, wa2_ref, ba2_ref, o_ref, la_ref, xn_ref):
    @pl.when(pl.program_id(1) == 0)
    def _():
        xn = _rmsnorm(x_ref[...], g_ref[...]).astype(BF16)
        xn_ref[...] = xn
        a_lr = jnp.dot(xn, wa_ref[...], preferred_element_type=F32)
        pre = jnp.dot(a_lr, wa2_ref[...], precision=HIGHEST, preferred_element_type=F32) + ba2_ref[...]
        la_ref[...] = _log_sigmoid(pre) / GATE_TAU

    o_ref[...] = jnp.dot(xn_ref[...], w_ref[...], preferred_element_type=F32)


def _norm_matmul_gate(x, g, w, wa, wa2, ba2, tn):
    n, d = x.shape
    m = w.shape[1]
    dk = wa2.shape[1]
    return pl.pallas_call(
        _norm_matmul_gate_body,
        grid=(n // TOK_BLOCK, m // tn),
        in_specs=[pl.BlockSpec((TOK_BLOCK, d), lambda i, j: (i, 0)),
                  pl.BlockSpec((1, d), lambda i, j: (0, 0)),
                  pl.BlockSpec((d, tn), lambda i, j: (0, j)),
                  pl.BlockSpec((d, LANES), lambda i, j: (0, 0)),
                  pl.BlockSpec((LANES, dk), lambda i, j: (0, 0)),
                  pl.BlockSpec((1, dk), lambda i, j: (0, 0))],
        out_specs=[pl.BlockSpec((TOK_BLOCK, tn), lambda i, j: (i, j)),
                   pl.BlockSpec((TOK_BLOCK, dk), lambda i, j: (i, 0))],
        out_shape=[jax.ShapeDtypeStruct((n, m), F32), jax.ShapeDtypeStruct((n, dk), F32)],
        scratch_shapes=[pltpu.VMEM((TOK_BLOCK, d), BF16)],
        compiler_params=_cparams(("parallel", "arbitrary")),
        name="norm_matmul_gate",
    )(x, g, w, wa, wa2, ba2)


def _norm_matmul_t_body(x_ref, g_ref, w_ref, o_ref, xt_ref, xn_ref):
    @pl.when(pl.program_id(1) == 0)
    def _():
        xn = _rmsnorm(x_ref[...], g_ref[...])
        xn_ref[...] = xn.astype(BF16)
        xt_ref[...] = xn.T.astype(BF16)

    o_ref[...] = jnp.dot(xn_ref[...], w_ref[...], preferred_element_type=F32)


def _norm_matmul_t(x, g, w, tn):
    n, d = x.shape
    m = w.shape[1]
    return pl.pallas_call(
        _norm_matmul_t_body,
        grid=(n // TOK_BLOCK, m // tn),
        in_specs=[pl.BlockSpec((TOK_BLOCK, d), lambda i, j: (i, 0)),
                  pl.BlockSpec((1, d), lambda i, j: (0, 0)),
                  pl.BlockSpec((d, tn), lambda i, j: (0, j))],
        out_specs=[pl.BlockSpec((TOK_BLOCK, tn), lambda i, j: (i, j)),
                   pl.BlockSpec((d, TOK_BLOCK), lambda i, j: (0, i))],
        out_shape=[jax.ShapeDtypeStruct((n, m), F32), jax.ShapeDtypeStruct((d, n), BF16)],
        scratch_shapes=[pltpu.VMEM((TOK_BLOCK, d), BF16)],
        compiler_params=_cparams(("parallel", "arbitrary")),
        name="norm_matmul_t",
    )(x, g, w)


def _res_matmul_body(a_ref, w_ref, x_ref, o_ref, ab_ref):
    @pl.when(pl.program_id(1) == 0)
    def _():
        ab_ref[...] = a_ref[...].astype(BF16)

    o_ref[...] = x_ref[...] + jnp.dot(ab_ref[...], w_ref[...], preferred_element_type=F32)


def _res_matmul(a, w, x, tn):
    n, k = a.shape
    m = w.shape[1]
    return pl.pallas_call(
        _res_matmul_body,
        grid=(n // TOK_BLOCK, m // tn),
        in_specs=[pl.BlockSpec((TOK_BLOCK, k), lambda i, j: (i, 0)),
                  pl.BlockSpec((k, tn), lambda i, j: (0, j)),
                  pl.BlockSpec((TOK_BLOCK, tn), lambda i, j: (i, j))],
        out_specs=pl.BlockSpec((TOK_BLOCK, tn), lambda i, j: (i, j)),
        out_shape=jax.ShapeDtypeStruct((n, m), F32),
        scratch_shapes=[pltpu.VMEM((TOK_BLOCK, k), BF16)],
        compiler_params=_cparams(("parallel", "arbitrary")),
        name="res_matmul",
    )(a, w, x)


def _res_matmul_ln_body(oa_ref, ub_ref, lg_ref, lb_ref, w_ref, x_ref, o_ref, ab_ref):
    da = oa_ref.shape[1]

    @pl.when(pl.program_id(1) == 0)
    def _():
        u = ub_ref[...]
        uc = u - jnp.mean(u, axis=-1, keepdims=True)
        y = uc * lax.rsqrt(jnp.mean(uc * uc, axis=-1, keepdims=True) + EPS) * lg_ref[...] + lb_ref[...]
        ab_ref[:, :da] = oa_ref[...].astype(BF16)
        ab_ref[:, da:] = (y * _sigmoid(y)).astype(BF16)

    o_ref[...] = x_ref[...] + jnp.dot(ab_ref[...], w_ref[...], preferred_element_type=F32)


def _res_matmul_ln(oa, ub, lg, lb, w, x, tn):
    n, da = oa.shape
    db = ub.shape[1]
    m = w.shape[1]
    return pl.pallas_call(
        _res_matmul_ln_body,
        grid=(n // TOK_BLOCK, m // tn),
        in_specs=[pl.BlockSpec((TOK_BLOCK, da), lambda i, j: (i, 0)),
                  pl.BlockSpec((TOK_BLOCK, db), lambda i, j: (i, 0)),
                  pl.BlockSpec((1, db), lambda i, j: (0, 0)),
                  pl.BlockSpec((1, db), lambda i, j: (0, 0)),
                  pl.BlockSpec((da + db, tn), lambda i, j: (0, j)),
                  pl.BlockSpec((TOK_BLOCK, tn), lambda i, j: (i, j))],
        out_specs=pl.BlockSpec((TOK_BLOCK, tn), lambda i, j: (i, j)),
        out_shape=jax.ShapeDtypeStruct((n, m), F32),
        scratch_shapes=[pltpu.VMEM((TOK_BLOCK, da + db), BF16)],
        compiler_params=_cparams(("parallel", "arbitrary")),
        name="res_matmul_ln",
    )(oa, ub, lg, lb, w, x)


def _dot_f32(a, b, dims):
    return lax.dot_general(a, b, (dims, ((), ())), precision=HIGHEST, preferred_element_type=F32)


def _dot_bf16(a, b, dims):
    return lax.dot_general(a.astype(BF16), b.astype(BF16), (dims, ((), ())), preferred_element_type=F32)


def _head_out(o, gn, g):
    y = o * lax.rsqrt(jnp.mean(o * o, axis=-1, keepdims=True) + EPS) * gn
    return y * (g * _sigmoid(g))


def _gla_prompt_body(q_ref, k_ref, v_ref, g_ref, la_ref, gn_ref, o_ref, s_out_ref, s_ref, acc_ref):
    c = pl.program_id(2)
    dk = q_ref.shape[1]

    @pl.when(c == 0)
    def _():
        s_ref[...] = jnp.zeros_like(s_ref)

    la = la_ref[...]
    row = lax.broadcasted_iota(jnp.int32, (GLA_CHUNK, GLA_CHUNK), 0)
    col = lax.broadcasted_iota(jnp.int32, (GLA_CHUNK, GLA_CHUNK), 1)
    tri = (row >= col).astype(F32)
    b = _dot_f32(tri, la, ((1,), (0,)))
    b_last = b[GLA_CHUNK - 1:GLA_CHUNK, :]
    q = q_ref[...] * (dk ** -0.5)
    k = k_ref[...]
    v = v_ref[...]
    s = s_ref[...]

    v_b = v.astype(BF16)
    acc_ref[...] = _dot_bf16(q * jnp.exp(b), s, ((1,), (0,)))

    sub_row = lax.broadcasted_iota(jnp.int32, (GLA_SUB, GLA_SUB), 0)
    sub_col = lax.broadcasted_iota(jnp.int32, (GLA_SUB, GLA_SUB), 1)
    causal = sub_row >= sub_col
    for j in range(GLA_CHUNK // GLA_SUB):
        lo, hi = j * GLA_SUB, (j + 1) * GLA_SUB
        b_j = b[lo:hi, :]
        b_first = b[lo:lo + 1, :]
        b_end = b[hi - 1:hi, :]
        q_d = q[lo:hi, :] * jnp.exp(b_j - b_first)
        k_d = k[lo:hi, :] * jnp.exp(jnp.minimum(b_first - b_j, EXP_CLAMP))
        sc = jnp.where(causal, _dot_bf16(q_d, k_d, ((1,), (1,))), 0.0)
        acc_ref[lo:hi, :] += _dot_bf16(sc, v_b[lo:hi, :], ((1,), (0,)))
        if hi < GLA_CHUNK:
            k_p = k[lo:hi, :] * jnp.exp(b_end - b_j)
            q_p = q[hi:, :] * jnp.exp(b[hi:, :] - b_end)
            sc = _dot_bf16(q_p, k_p, ((1,), (1,)))
            acc_ref[hi:, :] += _dot_bf16(sc, v_b[lo:hi, :], ((1,), (0,)))

    decay = jnp.exp(jnp.broadcast_to(b_last, (dk, dk))).T
    k_e = k * jnp.exp(b_last - b)
    s_new = jnp.concatenate([decay] * (s.shape[1] // dk), axis=1) * s + _dot_bf16(k_e.T, v_b, ((1,), (0,)))
    s_ref[...] = s_new

    @pl.when(c == pl.num_programs(2) - 1)
    def _():
        s_out_ref[0, 0, 0] = s_new

    o_ref[...] = _head_out(acc_ref[...], gn_ref[...], g_ref[...])


def _gla_prompt(z, la, gn, bsz, t):
    dk, dv = GLA_CHUNK, 2 * GLA_CHUNK
    nc = t // GLA_CHUNK
    rows = lambda b, h, c: b * nc + c
    return pl.pallas_call(
        _gla_prompt_body,
        grid=(bsz, H_A, nc),
        in_specs=[pl.BlockSpec((GLA_CHUNK, dk), lambda b, h, c: (rows(b, h, c), h)),
                  pl.BlockSpec((GLA_CHUNK, dk), lambda b, h, c: (rows(b, h, c), H_A + h)),
                  pl.BlockSpec((GLA_CHUNK, dv), lambda b, h, c: (rows(b, h, c), H_A + h)),
                  pl.BlockSpec((GLA_CHUNK, dv), lambda b, h, c: (rows(b, h, c), 2 * H_A + h)),
                  pl.BlockSpec((GLA_CHUNK, dk), lambda b, h, c: (rows(b, h, c), h)),
                  pl.BlockSpec((1, dv), lambda b, h, c: (0, h))],
        out_specs=[pl.BlockSpec((GLA_CHUNK, dv), lambda b, h, c: (rows(b, h, c), h)),
                   pl.BlockSpec((1, 1, 1, dk, dv), lambda b, h, c: (0, b, h, 0, 0))],
        out_shape=[jax.ShapeDtypeStruct((bsz * t, H_A * dv), F32),
                   jax.ShapeDtypeStruct((1, bsz, H_A, dk, dv), F32)],
        scratch_shapes=[pltpu.VMEM((dk, dv), F32), pltpu.VMEM((GLA_CHUNK, dv), F32)],
        compiler_params=_cparams(("parallel", "parallel", "arbitrary")),
        name="gla_prompt",
    )(z, z, z, z, la, gn)


def _gla_sample_body(st_ref, lat_ref, kt_ref, qt_ref, v_ref, g_ref, gn_ref, o_ref, s_out_ref):
    dv = st_ref.shape[-1]
    for i in range(SAMPLE_BLOCK):
        for h in range(H_A):
            decay = jnp.exp(lat_ref[0, h, :, i:i + 1])
            v_row = v_ref[i:i + 1, h * dv:(h + 1) * dv]
            s_new = decay * st_ref[0, i, h] + kt_ref[0, h, :, i:i + 1] * v_row
            s_out_ref[0, i, h] = s_new
            q_col = qt_ref[0, h, :, i:i + 1] * (st_ref.shape[-2] ** -0.5)
            o = jnp.sum(q_col * s_new, axis=0, keepdims=True)
            o_ref[i:i + 1, h * dv:(h + 1) * dv] = _head_out(
                o, gn_ref[:, h * dv:(h + 1) * dv], g_ref[i:i + 1, h * dv:(h + 1) * dv])


def _gla_sample(state, lat, kt, qt, v, g, gn):
    _, ns, _, dk, dv = state.shape
    steps = ns // SAMPLE_BLOCK
    col_spec = pl.BlockSpec((1, H_A, dk, SAMPLE_BLOCK), lambda i: (i, 0, 0, 0))
    row_spec = pl.BlockSpec((SAMPLE_BLOCK, H_A * dv), lambda i: (i, 0))
    st_spec = pl.BlockSpec((1, SAMPLE_BLOCK, H_A, dk, dv), lambda i: (0, i, 0, 0, 0))
    return pl.pallas_call(
        _gla_sample_body,
        grid=(steps,),
        in_specs=[st_spec, col_spec, col_spec, col_spec, row_spec, row_spec,
                  pl.BlockSpec((1, H_A * dv), lambda i: (0, 0))],
        out_specs=[row_spec, st_spec],
        out_shape=[jax.ShapeDtypeStruct((ns, H_A * dv), F32), jax.ShapeDtypeStruct(state.shape, F32)],
        compiler_params=_cparams(("parallel",)),
        name="gla_sample",
    )(state, lat, kt, qt, v, g, gn)


def _fill_padded(pad_ref, hist, t, value):
    cols = pad_ref.shape[1]
    pad_ref[0:hist, :] = jnp.zeros((hist, cols), F32)
    pad_ref[hist:hist + t, :] = value
    pad_ref[hist + t:, :] = jnp.zeros((SUBLANES, cols), F32)


def _causal_taps(pad_ref, w_ref, base, hist, n_taps, acc):
    first = hist - (n_taps - 1)
    win = pad_ref[pl.ds(base, CONV_ROWS + hist + SUBLANES), :]
    for r in range(SUBLANES):
        taps = [w for w in range(n_taps) if (first + w) % SUBLANES == r]
        if not taps:
            continue
        shifted = win[r:r + CONV_ROWS + hist, :]
        for w in taps:
            lo = (first + w) // SUBLANES * SUBLANES
            acc = acc + shifted[lo:lo + CONV_ROWS, :] * w_ref[w:w + 1, :]
    return acc


def _conv_b_prompt_body(a_ref, b_ref, w_ref, bias_ref, o_ref, buf_ref, pad_ref):
    t = a_ref.shape[0]
    hist = 4 * SUBLANES
    _fill_padded(pad_ref, hist, t, a_ref[...] * _sigmoid(b_ref[...]))
    buf_ref[0, 0] = pad_ref[hist + t - (CONV_B_W - 1):hist + t, :]

    def step(i, carry):
        base = pl.multiple_of(i * CONV_ROWS, CONV_ROWS)
        acc = jnp.zeros((CONV_ROWS, pad_ref.shape[1]), F32) + bias_ref[...]
        o_ref[pl.ds(base, CONV_ROWS), :] = _causal_taps(pad_ref, w_ref, base, hist, CONV_B_W, acc)
        return carry

    lax.fori_loop(0, t // CONV_ROWS, step, 0)


def _conv_b_prompt(z, w, bias, bsz, t, col0, d_b):
    cb = 256
    nb = d_b // cb
    a0, b0 = col0 // cb, (col0 + d_b) // cb
    return pl.pallas_call(
        _conv_b_prompt_body,
        grid=(bsz, nb),
        in_specs=[pl.BlockSpec((t, cb), lambda b, j: (b, a0 + j)),
                  pl.BlockSpec((t, cb), lambda b, j: (b, b0 + j)),
                  pl.BlockSpec((CONV_B_W, cb), lambda b, j: (0, j)),
                  pl.BlockSpec((1, cb), lambda b, j: (0, j))],
        out_specs=[pl.BlockSpec((t, cb), lambda b, j: (b, j)),
                   pl.BlockSpec((1, 1, CONV_B_W - 1, cb), lambda b, j: (0, b, 0, j))],
        out_shape=[jax.ShapeDtypeStruct((bsz * t, d_b), F32),
                   jax.ShapeDtypeStruct((1, bsz, CONV_B_W - 1, d_b), F32)],
        scratch_shapes=[pltpu.VMEM((t + 5 * SUBLANES, cb), F32)],
        compiler_params=_cparams(("parallel", "parallel")),
        name="conv_b_prompt",
    )(z, z, w, bias)


def _conv_b_sample_body(buf_ref, a_ref, b_ref, w_ref, bias_ref, o_ref, nbuf_ref):
    d_b = a_ref.shape[1]
    glu = a_ref[...] * _sigmoid(b_ref[...])
    acc = bias_ref[...] + glu * w_ref[CONV_B_W - 1:CONV_B_W, :]
    for w in range(CONV_B_W - 1):
        acc = acc + buf_ref[:, w * d_b:(w + 1) * d_b] * w_ref[w:w + 1, :]
    o_ref[...] = acc
    nbuf_ref[:, :(CONV_B_W - 2) * d_b] = buf_ref[:, d_b:]
    nbuf_ref[:, (CONV_B_W - 2) * d_b:] = glu


def _conv_b_sample(buf, z, w, bias, row0, col0, d_b):
    ns = buf.shape[0]
    blk = 2 * SAMPLE_BLOCK
    r0 = row0 // blk
    return pl.pallas_call(
        _conv_b_sample_body,
        grid=(ns // blk,),
        in_specs=[pl.BlockSpec((blk, buf.shape[1]), lambda i: (i, 0)),
                  pl.BlockSpec((blk, d_b), lambda i: (r0 + i, col0 // d_b)),
                  pl.BlockSpec((blk, d_b), lambda i: (r0 + i, col0 // d_b + 1)),
                  pl.BlockSpec((CONV_B_W, d_b), lambda i: (0, 0)),
                  pl.BlockSpec((1, d_b), lambda i: (0, 0))],
        out_specs=[pl.BlockSpec((blk, d_b), lambda i: (i, 0)),
                   pl.BlockSpec((blk, buf.shape[1]), lambda i: (i, 0))],
        out_shape=[jax.ShapeDtypeStruct((ns, d_b), F32), jax.ShapeDtypeStruct(buf.shape, F32)],
        compiler_params=_cparams(("parallel",)),
        name="conv_b_sample",
    )(buf, z, z, w, bias)


def _conv_c_prompt_body(gb_ref, gc_ref, hv_ref, w_ref, o_ref, buf_ref, pad_ref):
    t = gb_ref.shape[0]
    hist = SUBLANES
    _fill_padded(pad_ref, hist, t, gc_ref[...] * hv_ref[...])
    buf_ref[0, 0] = pad_ref[hist + t - (CONV_C_W - 1):hist + t, :]

    def step(i, carry):
        base = pl.multiple_of(i * CONV_ROWS, CONV_ROWS)
        acc = jnp.zeros((CONV_ROWS, pad_ref.shape[1]), F32)
        acc = _causal_taps(pad_ref, w_ref, base, hist, CONV_C_W, acc)
        o_ref[pl.ds(base, CONV_ROWS), :] = acc * gb_ref[pl.ds(base, CONV_ROWS), :]
        return carry

    lax.fori_loop(0, t // CONV_ROWS, step, 0)


def _conv_c_prompt(z, w, bsz, t, d_c):
    cb = 512
    nb = d_c // cb
    return pl.pallas_call(
        _conv_c_prompt_body,
        grid=(bsz, nb),
        in_specs=[pl.BlockSpec((t, cb), lambda b, j: (b, j)),
                  pl.BlockSpec((t, cb), lambda b, j: (b, nb + j)),
                  pl.BlockSpec((t, cb), lambda b, j: (b, 2 * nb + j)),
                  pl.BlockSpec((CONV_C_W, cb), lambda b, j: (0, j))],
        out_specs=[pl.BlockSpec((t, cb), lambda b, j: (b, j)),
                   pl.BlockSpec((1, 1, CONV_C_W - 1, cb), lambda b, j: (0, b, 0, j))],
        out_shape=[jax.ShapeDtypeStruct((bsz * t, d_c), F32),
                   jax.ShapeDtypeStruct((1, bsz, CONV_C_W - 1, d_c), F32)],
        scratch_shapes=[pltpu.VMEM((t + 2 * SUBLANES, cb), F32)],
        compiler_params=_cparams(("parallel", "parallel")),
        name="conv_c_prompt",
    )(z, z, z, w)


def _conv_c_sample_body(buf_ref, gb_ref, gc_ref, hv_ref, w_ref, o_ref, nbuf_ref):
    d_c = gb_ref.shape[1]
    u = gc_ref[...] * hv_ref[...]
    c = buf_ref[:, :d_c] * w_ref[0:1, :] + buf_ref[:, d_c:] * w_ref[1:2, :] + u * w_ref[2:3, :]
    o_ref[...] = gb_ref[...] * c
    nbuf_ref[:, :d_c] = buf_ref[:, d_c:]
    nbuf_ref[:, d_c:] = u


def _conv_c_sample(buf, z, w, row0, d_c):
    ns = buf.shape[0]
    blk = 2 * SAMPLE_BLOCK
    r0 = row0 // blk
    return pl.pallas_call(
        _conv_c_sample_body,
        grid=(ns // blk,),
        in_specs=[pl.BlockSpec((blk, buf.shape[1]), lambda i: (i, 0)),
                  pl.BlockSpec((blk, d_c), lambda i: (r0 + i, 0)),
                  pl.BlockSpec((blk, d_c), lambda i: (r0 + i, 1)),
                  pl.BlockSpec((blk, d_c), lambda i: (r0 + i, 2)),
                  pl.BlockSpec((CONV_C_W, d_c), lambda i: (0, 0))],
        out_specs=[pl.BlockSpec((blk, d_c), lambda i: (i, 0)),
                   pl.BlockSpec((blk, buf.shape[1]), lambda i: (i, 0))],
        out_shape=[jax.ShapeDtypeStruct((ns, d_c), F32), jax.ShapeDtypeStruct(buf.shape, F32)],
        compiler_params=_cparams(("parallel",)),
        name="conv_c_sample",
    )(buf, z, z, z, w)


def _candidate_tables():
    pairs = [(a, b) for a in range(TOPK) for b in range(TOPK) if (a + 1) * (b + 1) <= TOPK]
    rows = -(-len(pairs) // SUBLANES) * SUBLANES
    p1 = np.zeros((rows, TOPK), np.float32)
    p2 = np.zeros((rows, TOPK), np.float32)
    neg = np.zeros((rows, LANES), np.float32)
    for r, (a, b) in enumerate(pairs):
        p1[r, a] = 1.0
        p2[r, b] = 1.0
    neg[len(pairs):] = -np.inf
    return p1, p2, neg, p1.T.copy()


def _take_top(works, index, count, tie_break):
    rows, lanes = works[0].shape
    slot = lax.broadcasted_iota(jnp.int32, (count, lanes), 0)
    works = list(works)
    vals = [jnp.zeros((count, lanes), F32) for _ in works]
    ranks = [jnp.full((rows, lanes), float(count), F32) for _ in works]
    for a in range(count):
        for i, work in enumerate(works):
            m = jnp.max(work, axis=0, keepdims=True)
            sel = work == m
            if tie_break:
                first = jnp.min(jnp.where(sel, index, float(rows)), axis=0, keepdims=True)
                sel = index == first
            ranks[i] = jnp.where(sel, float(a), ranks[i])
            works[i] = jnp.where(sel, -jnp.inf, work)
            vals[i] = jnp.where(slot == a, m, vals[i])
    return vals, ranks


def _count(mask):
    return jnp.sum(mask.astype(F32), axis=0, keepdims=True)


def _route_heads(s1, s2, p1, p2, neg, p1t, key_idx, cand_idx, tie_break):
    n = len(s1)
    vals, ranks = _take_top(list(s1) + list(s2), key_idx, TOPK, tie_break)
    v1, v2, rank1, rank2 = vals[:n], vals[n:], ranks[:n], ranks[n:]
    cand = [_dot_f32(p1, v1[i], ((1,), (0,))) + _dot_f32(p2, v2[i], ((1,), (0,))) + neg for i in range(n)]
    _, crank = _take_top(cand, cand_idx, TOPK, tie_break)
    out = []
    for i in range(n):
        chosen = crank[i] < float(TOPK)
        cmax = v1[i][0:1, :] + v2[i][0:1, :]
        z = jnp.sum(jnp.where(chosen, jnp.exp(cand[i] - cmax), 0.0), axis=0, keepdims=True)
        width = jnp.dot(p1t, chosen.astype(F32), preferred_element_type=F32)
        b1 = jnp.zeros_like(rank1[i])
        for a in range(TOPK):
            b1 = jnp.where(rank1[i] == float(a), width[a:a + 1, :], b1)
        in1 = rank1[i] < float(TOPK)
        in2 = rank2[i] < float(TOPK)
        c1 = jnp.where(in1, jnp.exp(jnp.where(in1, s1[i] - v1[i][0:1, :], 0.0)) / z, 0.0)
        e2 = jnp.where(in2, jnp.exp(jnp.where(in2, s2[i] - v2[i][0:1, :], 0.0)), 0.0)
        clean = ((_count(in1) == float(TOPK)) & (_count(in2) == float(TOPK))
                 & (_count(chosen) == float(TOPK)))
        out.append((c1, b1, rank2[i], e2, clean))
    return out


ROUTE_HEADS = 4


def _route_body(q_ref, keys_ref, p1_ref, p2_ref, neg_ref, p1t_ref, u_ref, v_ref,
                r2_ref, e2_ref, c1_ref, b1_ref, ub_ref, vt_ref, q3_ref):
    ub_ref[...] = u_ref[0].astype(BF16)
    vt_ref[...] = v_ref[0].T.astype(BF16)

    for hp in range(2 * P_HEADS):
        q3_ref[hp] = q_ref[:, hp * LANES:(hp + 1) * LANES]
    key_idx = lax.broadcasted_iota(jnp.int32, (N_KEYS, LANES), 0).astype(F32)
    cand_idx = lax.broadcasted_iota(jnp.int32, (p1_ref.shape[0], LANES), 0).astype(F32)

    def emit(heads, tie_break):
        s1 = [_dot_f32(keys_ref[h, 0], q3_ref[2 * h], ((1,), (1,))) for h in heads]
        s2 = [_dot_f32(keys_ref[h, 1], q3_ref[2 * h + 1], ((1,), (1,))) for h in heads]
        rows = _route_heads(s1, s2, p1_ref[...], p2_ref[...], neg_ref[...], p1t_ref[...],
                            key_idx, cand_idx, tie_break)
        all_clean = None
        for h, (c1, b1, r2, e2, clean) in zip(heads, rows):
            c1_ref[h] = c1
            b1_ref[h] = b1
            r2_ref[h] = r2.astype(r2_ref.dtype)
            e2_ref[h] = e2.astype(e2_ref.dtype)
            all_clean = clean if all_clean is None else all_clean & clean
        return all_clean

    def step(i, carry):
        heads = [i * ROUTE_HEADS + j for j in range(ROUTE_HEADS)]
        all_clean = emit(heads, False)

        @pl.when(jnp.min(all_clean.astype(F32)) < 0.5)
        def _():
            emit(heads, True)

        return carry

    lax.fori_loop(0, P_HEADS // ROUTE_HEADS, step, 0)


def _route(q, keys, table_u, table_v, layer):
    n = q.shape[0]
    _, n_exp, d = table_u.shape
    steps = n // LANES
    n_slabs = min(n_exp // TABLE_SLAB, 1 << (steps.bit_length() - 1))
    slab = n_exp // n_slabs
    assert n_slabs * slab == n_exp
    p1, p2, neg, p1t = _candidate_tables()
    whole = lambda a: pl.BlockSpec(a.shape, lambda i: (0,) * a.ndim)
    out_spec = pl.BlockSpec((P_HEADS, N_KEYS, LANES), lambda i: (0, 0, i))
    shape = lambda dt: jax.ShapeDtypeStruct((P_HEADS, N_KEYS, n), dt)
    slab_of = lambda i: jnp.minimum(i, n_slabs - 1)
    table_spec = pl.BlockSpec((1, slab, d), lambda i: (layer, slab_of(i), 0))
    return pl.pallas_call(
        _route_body,
        grid=(steps,),
        in_specs=[pl.BlockSpec((LANES, q.shape[1]), lambda i: (i, 0)), whole(keys),
                  whole(p1), whole(p2), whole(neg), whole(p1t), table_spec, table_spec],
        out_specs=[out_spec] * 4 + [pl.BlockSpec((slab, d), lambda i: (slab_of(i), 0)),
                                    pl.BlockSpec((d, slab), lambda i: (0, slab_of(i)))],
        out_shape=[shape(BF16), shape(BF16), shape(F32), shape(F32),
                   jax.ShapeDtypeStruct((n_exp, d), BF16), jax.ShapeDtypeStruct((d, n_exp), BF16)],
        scratch_shapes=[pltpu.VMEM((2 * P_HEADS, LANES, LANES), F32)],
        compiler_params=_cparams(("arbitrary",)),
        name="peer_route",
    )(q, keys, jnp.asarray(p1), jnp.asarray(p2), jnp.asarray(neg), jnp.asarray(p1t), table_u, table_v)


def _bf16_pair(c):
    hi = float(np.asarray(c, dtype=BF16))
    return hi, float(np.asarray(c - hi, dtype=BF16))


def _gelu_tanh(x):
    c0 = float(np.sqrt(2.0 / np.pi))
    c0_hi, c0_lo = _bf16_pair(c0)
    c1_hi, c1_lo = _bf16_pair(c0 * 0.044715)
    x2 = x * x
    inner = x * ((c1_hi * x2 + c0_hi) + (c1_lo * x2 + c0_lo))
    half = 0.5 * x
    return half + half * jnp.tanh(inner)


def _peer_dense_body(final, xt_ref, u_ref, vt_ref, r2_ref, e2_ref, c1_ref, b1_ref, x_ref, gf_ref, o_ref,
                     acc_ref):
    e = pl.program_id(1)
    n_groups = xt_ref.shape[1] // PEER_LANES

    @pl.when(e == 0)
    def _():
        acc_ref[...] = jnp.zeros_like(acc_ref)

    def scores(j):
        cols = slice(j * PEER_LANES, (j + 1) * PEER_LANES)
        return jnp.dot(u_ref[...], xt_ref[:, cols], preferred_element_type=F32)

    def gates(j):
        cols = slice(j * PEER_LANES, (j + 1) * PEER_LANES)
        parts = []
        for r in range(EXPERT_BLOCK // N_KEYS):
            gate = jnp.zeros((N_KEYS, PEER_LANES), BF16)
            for h in range(P_HEADS):
                hit = r2_ref[h, :, cols] < b1_ref[h, r:r + 1, cols].astype(BF16)
                gate = gate + c1_ref[h, r:r + 1, cols].astype(BF16) * jnp.where(
                    hit, e2_ref[h, :, cols], jnp.zeros((), BF16))
            parts.append(gate)
        return jnp.concatenate(parts, axis=0)

    s_next = scores(0)
    for j in range(n_groups):
        s = s_next
        if j + 1 < n_groups:
            s_next = scores(j + 1)
        cols = slice(j * PEER_LANES, (j + 1) * PEER_LANES)
        weighted = gates(j) * _gelu_tanh(s.astype(BF16))
        acc_ref[:, cols] += jnp.dot(vt_ref[...], weighted, preferred_element_type=F32)

    @pl.when(e == pl.num_programs(1) - 1)
    def _():
        for j in range(n_groups):
            rows = slice(j * PEER_LANES, (j + 1) * PEER_LANES)
            out = x_ref[rows, :] + acc_ref[:, rows].T
            o_ref[rows, :] = _rmsnorm(out, gf_ref[...]) if final else out


def _peer_dense(xt, u_b, vt_b, r2, e2, c1, b1, x, g_final, final):
    d, n = xt.shape
    n_exp = u_b.shape[0]
    keys_per_block = EXPERT_BLOCK // N_KEYS
    once = pl.Buffered(1)
    rank_spec = pl.BlockSpec((P_HEADS, N_KEYS, TOK_BLOCK), lambda i, e: (0, 0, i), pipeline_mode=once)
    first_spec = pl.BlockSpec((P_HEADS, keys_per_block, TOK_BLOCK), lambda i, e: (0, e, i))
    return pl.pallas_call(
        functools.partial(_peer_dense_body, final),
        grid=(n // TOK_BLOCK, n_exp // EXPERT_BLOCK),
        in_specs=[pl.BlockSpec((d, TOK_BLOCK), lambda i, e: (0, i), pipeline_mode=once),
                  pl.BlockSpec((EXPERT_BLOCK, d), lambda i, e: (e, 0)),
                  pl.BlockSpec((d, EXPERT_BLOCK), lambda i, e: (0, e)),
                  rank_spec, rank_spec, first_spec, first_spec,
                  pl.BlockSpec((TOK_BLOCK, d), lambda i, e: (i, 0), pipeline_mode=once),
                  pl.BlockSpec((1, d), lambda i, e: (0, 0))],
        out_specs=pl.BlockSpec((TOK_BLOCK, d), lambda i, e: (i, 0), pipeline_mode=once),
        out_shape=jax.ShapeDtypeStruct((n, d), F32),
        scratch_shapes=[pltpu.VMEM((d, TOK_BLOCK), F32)],
        compiler_params=_cparams(("parallel", "arbitrary")),
        name="peer_dense_final" if final else "peer_dense",
    )(xt, u_b, vt_b, r2, e2, c1, b1, x, g_final)


def _peer(x, g_norm, wq, keys, table_u, table_v, layer, g_final, final):
    q, xt = _norm_matmul_t(x, g_norm, wq, 1024)
    r2, e2, c1, b1, u_b, vt_b = _route(q, keys, table_u, table_v, layer)
    return _peer_dense(xt, u_b, vt_b, r2, e2, c1, b1, x, g_final, final)


def _col_form(a, n_dk):
    ns = a.shape[0]
    a = a.reshape(ns // SAMPLE_BLOCK, SAMPLE_BLOCK, H_A, n_dk)
    return a.transpose(0, 2, 3, 1)


def kernel(x_prompt, x_sample, state_gla, state_conv_b, state_conv_c, norm_mix, norm_ffn, norm_final, w_in_even, w_a2, b_a2, gla_norm_g, conv_b_w, conv_b_bias, conv_b_ln_g, conv_b_ln_b, w_out_even, w_in_odd, conv_c_w, w_out_odd, peer_wq, peer_keys, peer_u, peer_v):
    bsz, t, d = x_prompt.shape
    ns = x_sample.shape[0]
    n_prompt = bsz * t
    n_real = n_prompt + ns
    n_tok = -(-n_real // TOK_BLOCK) * TOK_BLOCK
    assert t % GLA_CHUNK == 0 and t % CONV_ROWS == 0 and ns % (2 * SAMPLE_BLOCK) == 0
    assert x_sample.shape[1] == 1 and n_prompt % (2 * SAMPLE_BLOCK) == 0

    d_a = d // 2
    dk_a = d_a // 2
    d_b = d - d_a
    rank = w_a2.shape[1]
    col_a = 2 * dk_a + 2 * d_a
    row = lambda a: a.reshape(1, -1)

    x = jnp.concatenate([x_prompt.reshape(n_prompt, d), x_sample.reshape(ns, d),
                         jnp.zeros((n_tok - n_real, d), F32)], axis=0)

    w_in = w_in_even[0]
    w_main = jnp.concatenate([w_in[:, :col_a], w_in[:, col_a + rank:]], axis=1).astype(BF16)
    w_gate = jnp.pad(w_in[:, col_a:col_a + rank], ((0, 0), (0, LANES - rank))).astype(BF16)
    w_a2p = jnp.pad(w_a2[0], ((0, LANES - rank), (0, 0)))
    z, la = _norm_matmul_gate(x, row(norm_mix[0]), w_main, w_gate, w_a2p, row(b_a2[0]), 1024)

    gn = row(gla_norm_g[0])
    oa_p, gla_p = _gla_prompt(z, la, gn, bsz, t)
    zs = z[n_prompt:n_real]
    las = la[n_prompt:n_real]
    oa_s, gla_s = _gla_sample(state_gla, _col_form(las, dk_a // H_A), _col_form(zs[:, dk_a:2 * dk_a], dk_a // H_A),
                              _col_form(zs[:, :dk_a], dk_a // H_A),
                              zs[:, 2 * dk_a:2 * dk_a + d_a], zs[:, 2 * dk_a + d_a:col_a], gn)

    ub_p, cb_p = _conv_b_prompt(z, conv_b_w[0], row(conv_b_bias[0]), bsz, t, col_a, d_b)
    ub_s, cb_s = _conv_b_sample(state_conv_b[0].reshape(ns, -1), z, conv_b_w[0], row(conv_b_bias[0]),
                                n_prompt, col_a, d_b)
    tail = jnp.zeros((n_tok - n_real, d_a), F32)
    oa = jnp.concatenate([oa_p, oa_s, tail], axis=0)
    ub = jnp.concatenate([ub_p, ub_s, tail], axis=0)
    x = _res_matmul_ln(oa, ub, row(conv_b_ln_g[0]), row(conv_b_ln_b[0]), w_out_even[0].astype(BF16), x, 1024)
    wq_b = peer_wq.astype(BF16)
    x = _peer(x, row(norm_ffn[0]), wq_b[0], peer_keys[0], peer_u, peer_v, 0, row(norm_final), False)

    z = _norm_matmul(x, row(norm_mix[1]), w_in_odd[0].astype(BF16), 1024)
    gc_p, cc_p = _conv_c_prompt(z, conv_c_w[0], bsz, t, d)
    gc_s, cc_s = _conv_c_sample(state_conv_c[0].reshape(ns, -1), z, conv_c_w[0], n_prompt, d)
    gc = jnp.concatenate([gc_p, gc_s, jnp.zeros((n_tok - n_real, d), F32)], axis=0)
    x = _res_matmul(gc, w_out_odd[0].astype(BF16), x, 1024)
    y = _peer(x, row(norm_ffn[1]), wq_b[1], peer_keys[1], peer_u, peer_v, 1, row(norm_final), True)

    y_prompt = y[:n_prompt].reshape(bsz, t, d)
    y_sample = y[n_prompt:n_real].reshape(ns, 1, d)
    return (y_prompt, y_sample, gla_p, cb_p, cc_p,
            gla_s, cb_s.reshape(state_conv_b.shape), cc_s.reshape(state_conv_c.shape))
```

```python
import functools
from typing import NamedTuple

import numpy as np
import jax
import jax.numpy as jnp
from jax import lax
from jax.experimental import pallas as pl
from jax.experimental.pallas import tpu as pltpu

F32 = jnp.float32
BF16 = jnp.bfloat16
HIGHEST = lax.Precision.HIGHEST

EPS = 1e-6
GATE_TAU = 16.0
H_A = 4
P_HEADS = 8
N_KEYS = 128
TOPK = 16
CONV_B_W = 31
CONV_C_W = 3

LANES = 128
SUBLANES = 8
VMEM_LIMIT = 56 * 1024 * 1024

TOK_BLOCK = 768
EXPERT_BLOCK = 1024
PEER_LANES = 256
TABLE_SLAB = 256
GLA_CHUNK = 128
GLA_SUB = 16
EXP_CLAMP = 80.0
CONV_ROWS = 64
SAMPLE_BLOCK = 8


def _cparams(sem):
    return pltpu.CompilerParams(dimension_semantics=sem, vmem_limit_bytes=VMEM_LIMIT)


def _rmsnorm(x, g):
    return x * lax.rsqrt(jnp.mean(x * x, axis=-1, keepdims=True) + EPS) * g


def _sigmoid(x):
    return jax.nn.sigmoid(x)


def _log_sigmoid(x):
    return jnp.minimum(x, 0.0) - jnp.log(1.0 + jnp.exp(-jnp.abs(x)))


def _norm_matmul_body(x_ref, g_ref, w_ref, o_ref, xn_ref):
    @pl.when(pl.program_id(1) == 0)
    def _():
        xn_ref[...] = _rmsnorm(x_ref[...], g_ref[...]).astype(BF16)

    o_ref[...] = jnp.dot(xn_ref[...], w_ref[...], preferred_element_type=F32)


def _norm_matmul(x, g, w, tn):
    n, d = x.shape
    m = w.shape[1]
    return pl.pallas_call(
        _norm_matmul_body,
        grid=(n // TOK_BLOCK, m // tn),
        in_specs=[pl.BlockSpec((TOK_BLOCK, d), lambda i, j: (i, 0)),
                  pl.BlockSpec((1, d), lambda i, j: (0, 0)),
                  pl.BlockSpec((d, tn), lambda i, j: (0, j))],
        out_specs=pl.BlockSpec((TOK_BLOCK, tn), lambda i, j: (i, j)),
        out_shape=jax.ShapeDtypeStruct((n, m), F32),
        scratch_shapes=[pltpu.VMEM((TOK_BLOCK, d), BF16)],
        compiler_params=_cparams(("parallel", "arbitrary")),
        name="norm_matmul",
    )(x, g, w)


def _norm_matmul_gate_body(x_ref, g_ref, w_ref, wa_ref, wa2_ref, ba2_ref, o_ref, la_ref, xn_ref):
    @pl.when(pl.program_id(1) == 0)
    def _():
        xn = _rmsnorm(x_ref[...], g_ref[...]).astype(BF16)
        xn_ref[...] = xn
        a_lr = jnp.dot(xn, wa_ref[...], preferred_element_type=F32)
        pre = jnp.dot(a_lr, wa2_ref[...], precision=HIGHEST, preferred_element_type=F32) + ba2_ref[...]
        la_ref[...] = _log_sigmoid(pre) / GATE_TAU

    o_ref[...] = jnp.dot(xn_ref[...], w_ref[...], preferred_element_type=F32)


def _norm_matmul_gate(x, g, w, wa, wa2, ba2, tn):
    n, d = x.shape
    m = w.shape[1]
    dk = wa2.shape[1]
    return pl.pallas_call(
        _norm_matmul_gate_body,
        grid=(n // TOK_BLOCK, m // tn),
        in_specs=[pl.BlockSpec((TOK_BLOCK, d), lambda i, j: (i, 0)),
                  pl.BlockSpec((1, d), lambda i, j: (0, 0)),
                  pl.BlockSpec((d, tn), lambda i, j: (0, j)),
                  pl.BlockSpec((d, LANES), lambda i, j: (0, 0)),
                  pl.BlockSpec((LANES, dk), lambda i, j: (0, 0)),
                  pl.BlockSpec((1, dk), lambda i, j: (0, 0))],
        out_specs=[pl.BlockSpec((TOK_BLOCK, tn), lambda i, j: (i, j)),
                   pl.BlockSpec((TOK_BLOCK, dk), lambda i, j: (i, 0))],
        out_shape=[jax.ShapeDtypeStruct((n, m), F32), jax.ShapeDtypeStruct((n, dk), F32)],
        scratch_shapes=[pltpu.VMEM((TOK_BLOCK, d), BF16)],
        compiler_params=_cparams(("parallel", "arbitrary")),
        name="norm_matmul_gate",
    )(x, g, w, wa, wa2, ba2)


def _norm_matmul_t_body(x_ref, g_ref, w_ref, o_ref, xt_ref, xn_ref):
    @pl.when(pl.program_id(1) == 0)
    def _():
        xn = _rmsnorm(x_ref[...], g_ref[...])
        xn_ref[...] = xn.astype(BF16)
        xt_ref[...] = xn.T.astype(BF16)

    o_ref[...] = jnp.dot(xn_ref[...], w_ref[...], preferred_element_type=F32)


def _norm_matmul_t(x, g, w, tn):
    n, d = x.shape
    m = w.shape[1]
    return pl.pallas_call(
        _norm_matmul_t_body,
        grid=(n // TOK_BLOCK, m // tn),
        in_specs=[pl.BlockSpec((TOK_BLOCK, d), lambda i, j: (i, 0)),
                  pl.BlockSpec((1, d), lambda i, j: (0, 0)),
                  pl.BlockSpec((d, tn), lambda i, j: (0, j))],
        out_specs=[pl.BlockSpec((TOK_BLOCK, tn), lambda i, j: (i, j)),
                   pl.BlockSpec((d, TOK_BLOCK), lambda i, j: (0, i))],
        out_shape=[jax.ShapeDtypeStruct((n, m), F32), jax.ShapeDtypeStruct((d, n), BF16)],
        scratch_shapes=[pltpu.VMEM((TOK_BLOCK, d), BF16)],
        compiler_params=_cparams(("parallel", "arbitrary")),
        name="norm_matmul_t",
    )(x, g, w)


def _res_matmul_body(a_ref, w_ref, x_ref, o_ref, ab_ref):
    @pl.when(pl.program_id(1) == 0)
    def _():
        ab_ref[...] = a_ref[...].astype(BF16)

    o_ref[...] = x_ref[...] + jnp.dot(ab_ref[...], w_ref[...], preferred_element_type=F32)


def _res_matmul(a, w, x, tn):
    n, k = a.shape
    m = w.shape[1]
    return pl.pallas_call(
        _res_matmul_body,
        grid=(n // TOK_BLOCK, m // tn),
        in_specs=[pl.BlockSpec((TOK_BLOCK, k), lambda i, j: (i, 0)),
                  pl.BlockSpec((k, tn), lambda i, j: (0, j)),
                  pl.BlockSpec((TOK_BLOCK, tn), lambda i, j: (i, j))],
        out_specs=pl.BlockSpec((TOK_BLOCK, tn), lambda i, j: (i, j)),
        out_shape=jax.ShapeDtypeStruct((n, m), F32),
        scratch_shapes=[pltpu.VMEM((TOK_BLOCK, k), BF16)],
        compiler_params=_cparams(("parallel", "arbitrary")),
        name="res_matmul",
    )(a, w, x)


def _res_matmul_ln_body(oa_ref, ub_ref, lg_ref, lb_ref, w_ref, x_ref, o_ref, ab_ref):
    da = oa_ref.shape[1]

    @pl.when(pl.program_id(1) == 0)
    def _():
        u = ub_ref[...]
        uc = u - jnp.mean(u, axis=-1, keepdims=True)
        y = uc * lax.rsqrt(jnp.mean(uc * uc, axis=-1, keepdims=True) + EPS) * lg_ref[...] + lb_ref[...]
        ab_ref[:, :da] = oa_ref[...].astype(BF16)
        ab_ref[:, da:] = (y * _sigmoid(y)).astype(BF16)

    o_ref[...] = x_ref[...] + jnp.dot(ab_ref[...], w_ref[...], preferred_element_type=F32)


def _res_matmul_ln(oa, ub, lg, lb, w, x, tn):
    n, da = oa.shape
    db = ub.shape[1]
    m = w.shape[1]
    return pl.pallas_call(
        _res_matmul_ln_body,
        grid=(n // TOK_BLOCK, m // tn),
        in_specs=[pl.BlockSpec((TOK_BLOCK, da), lambda i, j: (i, 0)),
                  pl.BlockSpec((TOK_BLOCK, db), lambda i, j: (i, 0)),
                  pl.BlockSpec((1, db), lambda i, j: (0, 0)),
                  pl.BlockSpec((1, db), lambda i, j: (0, 0)),
                  pl.BlockSpec((da + db, tn), lambda i, j: (0, j)),
                  pl.BlockSpec((TOK_BLOCK, tn), lambda i, j: (i, j))],
        out_specs=pl.BlockSpec((TOK_BLOCK, tn), lambda i, j: (i, j)),
        out_shape=jax.ShapeDtypeStruct((n, m), F32),
        scratch_shapes=[pltpu.VMEM((TOK_BLOCK, da + db), BF16)],
        compiler_params=_cparams(("parallel", "arbitrary")),
        name="res_matmul_ln",
    )(oa, ub, lg, lb, w, x)


def _dot_f32(a, b, dims):
    return lax.dot_general(a, b, (dims, ((), ())), precision=HIGHEST, preferred_element_type=F32)


def _dot_bf16(a, b, dims):
    return lax.dot_general(a.astype(BF16), b.astype(BF16), (dims, ((), ())), preferred_element_type=F32)


def _head_out(o, gn, g):
    y = o * lax.rsqrt(jnp.mean(o * o, axis=-1, keepdims=True) + EPS) * gn
    return y * (g * _sigmoid(g))


def _gla_prompt_body(q_ref, k_ref, v_ref, g_ref, la_ref, gn_ref, dst_ref, o_ref, s_out_ref, s_ref, acc_ref):
    del dst_ref
    c = pl.program_id(2)
    dk = q_ref.shape[1]

    @pl.when(c == 0)
    def _():
        s_ref[...] = jnp.zeros_like(s_ref)

    la = la_ref[...]
    row = lax.broadcasted_iota(jnp.int32, (GLA_CHUNK, GLA_CHUNK), 0)
    col = lax.broadcasted_iota(jnp.int32, (GLA_CHUNK, GLA_CHUNK), 1)
    tri = (row >= col).astype(F32)
    b = _dot_f32(tri, la, ((1,), (0,)))
    b_last = b[GLA_CHUNK - 1:GLA_CHUNK, :]
    q = q_ref[...] * (dk ** -0.5)
    k = k_ref[...]
    v = v_ref[...]
    s = s_ref[...]

    v_b = v.astype(BF16)
    acc_ref[...] = _dot_bf16(q * jnp.exp(b), s, ((1,), (0,)))

    sub_row = lax.broadcasted_iota(jnp.int32, (GLA_SUB, GLA_SUB), 0)
    sub_col = lax.broadcasted_iota(jnp.int32, (GLA_SUB, GLA_SUB), 1)
    causal = sub_row >= sub_col
    for j in range(GLA_CHUNK // GLA_SUB):
        lo, hi = j * GLA_SUB, (j + 1) * GLA_SUB
        b_j = b[lo:hi, :]
        b_first = b[lo:lo + 1, :]
        b_end = b[hi - 1:hi, :]
        q_d = q[lo:hi, :] * jnp.exp(b_j - b_first)
        k_d = k[lo:hi, :] * jnp.exp(jnp.minimum(b_first - b_j, EXP_CLAMP))
        sc = jnp.where(causal, _dot_bf16(q_d, k_d, ((1,), (1,))), 0.0)
        acc_ref[lo:hi, :] += _dot_bf16(sc, v_b[lo:hi, :], ((1,), (0,)))
        if hi < GLA_CHUNK:
            k_p = k[lo:hi, :] * jnp.exp(b_end - b_j)
            q_p = q[hi:, :] * jnp.exp(b[hi:, :] - b_end)
            sc = _dot_bf16(q_p, k_p, ((1,), (1,)))
            acc_ref[hi:, :] += _dot_bf16(sc, v_b[lo:hi, :], ((1,), (0,)))

    decay = jnp.exp(jnp.broadcast_to(b_last, (dk, dk))).T
    k_e = k * jnp.exp(b_last - b)
    s_new = jnp.concatenate([decay] * (s.shape[1] // dk), axis=1) * s + _dot_bf16(k_e.T, v_b, ((1,), (0,)))
    s_ref[...] = s_new

    @pl.when(c == pl.num_programs(2) - 1)
    def _():
        s_out_ref[0, 0, 0] = s_new

    o_ref[...] = _head_out(acc_ref[...], gn_ref[...], g_ref[...])


def _gla_prompt(z, la, gn, bsz, t, dst):
    dk, dv = GLA_CHUNK, 2 * GLA_CHUNK
    nc = t // GLA_CHUNK
    rows = lambda b, h, c: b * nc + c
    return pl.pallas_call(
        _gla_prompt_body,
        grid=(bsz, H_A, nc),
        in_specs=[pl.BlockSpec((GLA_CHUNK, dk), lambda b, h, c: (rows(b, h, c), h)),
                  pl.BlockSpec((GLA_CHUNK, dk), lambda b, h, c: (rows(b, h, c), H_A + h)),
                  pl.BlockSpec((GLA_CHUNK, dv), lambda b, h, c: (rows(b, h, c), H_A + h)),
                  pl.BlockSpec((GLA_CHUNK, dv), lambda b, h, c: (rows(b, h, c), 2 * H_A + h)),
                  pl.BlockSpec((GLA_CHUNK, dk), lambda b, h, c: (rows(b, h, c), h)),
                  pl.BlockSpec((1, dv), lambda b, h, c: (0, h)),
                  pl.BlockSpec(memory_space=pl.ANY)],
        out_specs=[pl.BlockSpec((GLA_CHUNK, dv), lambda b, h, c: (rows(b, h, c), h)),
                   pl.BlockSpec((1, 1, 1, dk, dv), lambda b, h, c: (0, b, h, 0, 0))],
        out_shape=[jax.ShapeDtypeStruct(dst.shape, F32),
                   jax.ShapeDtypeStruct((1, bsz, H_A, dk, dv), F32)],
        input_output_aliases={6: 0},
        scratch_shapes=[pltpu.VMEM((dk, dv), F32), pltpu.VMEM((GLA_CHUNK, dv), F32)],
        compiler_params=_cparams(("parallel", "parallel", "arbitrary")),
        name="gla_prompt",
    )(z, z, z, z, la, gn, dst)


def _tail_spec(n_rows, row0, cols):
    tail = n_rows - row0
    assert row0 % tail == 0 and tail % SUBLANES == 0
    return pl.BlockSpec((tail, cols), lambda i: (row0 // tail, 0))


def _zero_tail_rows(o_ref, n_valid):
    @pl.when(pl.program_id(0) == 0)
    def _():
        o_ref[n_valid:, :] = jnp.zeros((o_ref.shape[0] - n_valid, o_ref.shape[1]), F32)


def _gla_sample_body(n_valid, st_ref, lat_ref, kt_ref, qt_ref, v_ref, g_ref, gn_ref, o_ref, s_out_ref, rows_ref):
    dv = st_ref.shape[-1]
    _zero_tail_rows(o_ref, n_valid)
    for i in range(SAMPLE_BLOCK):
        for h in range(H_A):
            decay = jnp.exp(lat_ref[0, h, :, i:i + 1])
            v_row = v_ref[i:i + 1, h * dv:(h + 1) * dv]
            s_new = decay * st_ref[0, i, h] + kt_ref[0, h, :, i:i + 1] * v_row
            s_out_ref[0, i, h] = s_new
            q_col = qt_ref[0, h, :, i:i + 1] * (st_ref.shape[-2] ** -0.5)
            o = jnp.sum(q_col * s_new, axis=0, keepdims=True)
            rows_ref[i:i + 1, h * dv:(h + 1) * dv] = _head_out(
                o, gn_ref[:, h * dv:(h + 1) * dv], g_ref[i:i + 1, h * dv:(h + 1) * dv])
    base = pl.multiple_of(pl.program_id(0) * SAMPLE_BLOCK, SAMPLE_BLOCK)
    o_ref[pl.ds(base, SAMPLE_BLOCK), :] = rows_ref[...]


def _gla_sample(state, lat, kt, qt, v, g, gn, n_rows, row0):
    _, ns, _, dk, dv = state.shape
    steps = ns // SAMPLE_BLOCK
    col_spec = pl.BlockSpec((1, H_A, dk, SAMPLE_BLOCK), lambda i: (i, 0, 0, 0))
    row_spec = pl.BlockSpec((SAMPLE_BLOCK, H_A * dv), lambda i: (i, 0))
    st_spec = pl.BlockSpec((1, SAMPLE_BLOCK, H_A, dk, dv), lambda i: (0, i, 0, 0, 0))
    return pl.pallas_call(
        functools.partial(_gla_sample_body, ns),
        grid=(steps,),
        in_specs=[st_spec, col_spec, col_spec, col_spec, row_spec, row_spec,
                  pl.BlockSpec((1, H_A * dv), lambda i: (0, 0))],
        out_specs=[_tail_spec(n_rows, row0, H_A * dv), st_spec],
        out_shape=[jax.ShapeDtypeStruct((n_rows, H_A * dv), F32), jax.ShapeDtypeStruct(state.shape, F32)],
        scratch_shapes=[pltpu.VMEM((SAMPLE_BLOCK, H_A * dv), F32)],
        compiler_params=_cparams(("arbitrary",)),
        name="gla_sample",
    )(state, lat, kt, qt, v, g, gn)


def _fill_padded(pad_ref, hist, t, value):
    cols = pad_ref.shape[1]
    pad_ref[0:hist, :] = jnp.zeros((hist, cols), F32)
    pad_ref[hist:hist + t, :] = value
    pad_ref[hist + t:, :] = jnp.zeros((SUBLANES, cols), F32)


def _causal_taps(pad_ref, w_ref, base, hist, n_taps, acc):
    first = hist - (n_taps - 1)
    win = pad_ref[pl.ds(base, CONV_ROWS + hist + SUBLANES), :]
    for r in range(SUBLANES):
        taps = [w for w in range(n_taps) if (first + w) % SUBLANES == r]
        if not taps:
            continue
        shifted = win[r:r + CONV_ROWS + hist, :]
        for w in taps:
            lo = (first + w) // SUBLANES * SUBLANES
            acc = acc + shifted[lo:lo + CONV_ROWS, :] * w_ref[w:w + 1, :]
    return acc


def _conv_b_prompt_body(a_ref, b_ref, w_ref, bias_ref, dst_ref, o_ref, buf_ref, pad_ref):
    del dst_ref
    t = a_ref.shape[0]
    hist = 4 * SUBLANES
    _fill_padded(pad_ref, hist, t, a_ref[...] * _sigmoid(b_ref[...]))
    buf_ref[0, 0] = pad_ref[hist + t - (CONV_B_W - 1):hist + t, :]

    def step(i, carry):
        base = pl.multiple_of(i * CONV_ROWS, CONV_ROWS)
        acc = jnp.zeros((CONV_ROWS, pad_ref.shape[1]), F32) + bias_ref[...]
        o_ref[pl.ds(base, CONV_ROWS), :] = _causal_taps(pad_ref, w_ref, base, hist, CONV_B_W, acc)
        return carry

    lax.fori_loop(0, t // CONV_ROWS, step, 0)


def _conv_b_prompt(z, w, bias, bsz, t, col0, d_b, dst):
    cb = 256
    nb = d_b // cb
    a0, b0 = col0 // cb, (col0 + d_b) // cb
    return pl.pallas_call(
        _conv_b_prompt_body,
        grid=(bsz, nb),
        in_specs=[pl.BlockSpec((t, cb), lambda b, j: (b, a0 + j)),
                  pl.BlockSpec((t, cb), lambda b, j: (b, b0 + j)),
                  pl.BlockSpec((CONV_B_W, cb), lambda b, j: (0, j)),
                  pl.BlockSpec((1, cb), lambda b, j: (0, j)),
                  pl.BlockSpec(memory_space=pl.ANY)],
        out_specs=[pl.BlockSpec((t, cb), lambda b, j: (b, j)),
                   pl.BlockSpec((1, 1, CONV_B_W - 1, cb), lambda b, j: (0, b, 0, j))],
        out_shape=[jax.ShapeDtypeStruct(dst.shape, F32),
                   jax.ShapeDtypeStruct((1, bsz, CONV_B_W - 1, d_b), F32)],
        input_output_aliases={4: 0},
        scratch_shapes=[pltpu.VMEM((t + 5 * SUBLANES, cb), F32)],
        compiler_params=_cparams(("parallel", "parallel")),
        name="conv_b_prompt",
    )(z, z, w, bias, dst)


def _conv_b_sample_body(n_valid, buf_ref, a_ref, b_ref, w_ref, bias_ref, o_ref, nbuf_ref):
    d_b = a_ref.shape[1]
    blk = a_ref.shape[0]
    _zero_tail_rows(o_ref, n_valid)
    glu = a_ref[...] * _sigmoid(b_ref[...])
    acc = bias_ref[...] + glu * w_ref[CONV_B_W - 1:CONV_B_W, :]
    for w in range(CONV_B_W - 1):
        acc = acc + buf_ref[:, w * d_b:(w + 1) * d_b] * w_ref[w:w + 1, :]
    o_ref[pl.ds(pl.multiple_of(pl.program_id(0) * blk, blk), blk), :] = acc
    nbuf_ref[:, :(CONV_B_W - 2) * d_b] = buf_ref[:, d_b:]
    nbuf_ref[:, (CONV_B_W - 2) * d_b:] = glu


def _conv_b_sample(buf, z, w, bias, row0, col0, d_b):
    ns = buf.shape[0]
    blk = 2 * SAMPLE_BLOCK
    r0 = row0 // blk
    return pl.pallas_call(
        functools.partial(_conv_b_sample_body, ns),
        grid=(ns // blk,),
        in_specs=[pl.BlockSpec((blk, buf.shape[1]), lambda i: (i, 0)),
                  pl.BlockSpec((blk, d_b), lambda i: (r0 + i, col0 // d_b)),
                  pl.BlockSpec((blk, d_b), lambda i: (r0 + i, col0 // d_b + 1)),
                  pl.BlockSpec((CONV_B_W, d_b), lambda i: (0, 0)),
                  pl.BlockSpec((1, d_b), lambda i: (0, 0))],
        out_specs=[_tail_spec(z.shape[0], row0, d_b),
                   pl.BlockSpec((blk, buf.shape[1]), lambda i: (i, 0))],
        out_shape=[jax.ShapeDtypeStruct((z.shape[0], d_b), F32), jax.ShapeDtypeStruct(buf.shape, F32)],
        compiler_params=_cparams(("arbitrary",)),
        name="conv_b_sample",
    )(buf, z, z, w, bias)


def _conv_c_prompt_body(gb_ref, gc_ref, hv_ref, w_ref, dst_ref, o_ref, buf_ref, pad_ref):
    del dst_ref
    t = gb_ref.shape[0]
    hist = SUBLANES
    _fill_padded(pad_ref, hist, t, gc_ref[...] * hv_ref[...])
    buf_ref[0, 0] = pad_ref[hist + t - (CONV_C_W - 1):hist + t, :]

    def step(i, carry):
        base = pl.multiple_of(i * CONV_ROWS, CONV_ROWS)
        acc = jnp.zeros((CONV_ROWS, pad_ref.shape[1]), F32)
        acc = _causal_taps(pad_ref, w_ref, base, hist, CONV_C_W, acc)
        o_ref[pl.ds(base, CONV_ROWS), :] = acc * gb_ref[pl.ds(base, CONV_ROWS), :]
        return carry

    lax.fori_loop(0, t // CONV_ROWS, step, 0)


def _conv_c_prompt(z, w, bsz, t, d_c, dst):
    cb = 512
    nb = d_c // cb
    return pl.pallas_call(
        _conv_c_prompt_body,
        grid=(bsz, nb),
        in_specs=[pl.BlockSpec((t, cb), lambda b, j: (b, j)),
                  pl.BlockSpec((t, cb), lambda b, j: (b, nb + j)),
                  pl.BlockSpec((t, cb), lambda b, j: (b, 2 * nb + j)),
                  pl.BlockSpec((CONV_C_W, cb), lambda b, j: (0, j)),
                  pl.BlockSpec(memory_space=pl.ANY)],
        out_specs=[pl.BlockSpec((t, cb), lambda b, j: (b, j)),
                   pl.BlockSpec((1, 1, CONV_C_W - 1, cb), lambda b, j: (0, b, 0, j))],
        out_shape=[jax.ShapeDtypeStruct(dst.shape, F32),
                   jax.ShapeDtypeStruct((1, bsz, CONV_C_W - 1, d_c), F32)],
        input_output_aliases={4: 0},
        scratch_shapes=[pltpu.VMEM((t + 2 * SUBLANES, cb), F32)],
        compiler_params=_cparams(("parallel", "parallel")),
        name="conv_c_prompt",
    )(z, z, z, w, dst)


def _conv_c_sample_body(n_valid, buf_ref, gb_ref, gc_ref, hv_ref, w_ref, o_ref, nbuf_ref):
    d_c = gb_ref.shape[1]
    blk = gb_ref.shape[0]
    _zero_tail_rows(o_ref, n_valid)
    u = gc_ref[...] * hv_ref[...]
    c = buf_ref[:, :d_c] * w_ref[0:1, :] + buf_ref[:, d_c:] * w_ref[1:2, :] + u * w_ref[2:3, :]
    o_ref[pl.ds(pl.multiple_of(pl.program_id(0) * blk, blk), blk), :] = gb_ref[...] * c
    nbuf_ref[:, :d_c] = buf_ref[:, d_c:]
    nbuf_ref[:, d_c:] = u


def _conv_c_sample(buf, z, w, row0, d_c):
    ns = buf.shape[0]
    blk = 2 * SAMPLE_BLOCK
    r0 = row0 // blk
    return pl.pallas_call(
        functools.partial(_conv_c_sample_body, ns),
        grid=(ns // blk,),
        in_specs=[pl.BlockSpec((blk, buf.shape[1]), lambda i: (i, 0)),
                  pl.BlockSpec((blk, d_c), lambda i: (r0 + i, 0)),
                  pl.BlockSpec((blk, d_c), lambda i: (r0 + i, 1)),
                  pl.BlockSpec((blk, d_c), lambda i: (r0 + i, 2)),
                  pl.BlockSpec((CONV_C_W, d_c), lambda i: (0, 0))],
        out_specs=[_tail_spec(z.shape[0], row0, d_c),
                   pl.BlockSpec((blk, buf.shape[1]), lambda i: (i, 0))],
        out_shape=[jax.ShapeDtypeStruct((z.shape[0], d_c), F32), jax.ShapeDtypeStruct(buf.shape, F32)],
        compiler_params=_cparams(("arbitrary",)),
        name="conv_c_sample",
    )(buf, z, z, z, w)


def _candidate_tables():
    pairs = [(a, b) for a in range(TOPK) for b in range(TOPK) if (a + 1) * (b + 1) <= TOPK]
    rows = -(-len(pairs) // SUBLANES) * SUBLANES
    p1 = np.zeros((rows, TOPK), np.float32)
    p2 = np.zeros((rows, TOPK), np.float32)
    neg = np.zeros((rows, LANES), np.float32)
    for r, (a, b) in enumerate(pairs):
        p1[r, a] = 1.0
        p2[r, b] = 1.0
    neg[len(pairs):] = -np.inf
    return p1, p2, neg, p1.T.copy()


def _take_top(works, index, count, tie_break):
    rows, lanes = works[0].shape
    slot = lax.broadcasted_iota(jnp.int32, (count, lanes), 0)
    works = list(works)
    vals = [jnp.zeros((count, lanes), F32) for _ in works]
    ranks = [jnp.full((rows, lanes), float(count), F32) for _ in works]
    for a in range(count):
        for i, work in enumerate(works):
            m = jnp.max(work, axis=0, keepdims=True)
            sel = work == m
            if tie_break:
                first = jnp.min(jnp.where(sel, index, float(rows)), axis=0, keepdims=True)
                sel = index == first
            ranks[i] = jnp.where(sel, float(a), ranks[i])
            works[i] = jnp.where(sel, -jnp.inf, work)
            vals[i] = jnp.where(slot == a, m, vals[i])
    return vals, ranks


def _count(mask):
    return jnp.sum(mask.astype(F32), axis=0, keepdims=True)


def _route_heads(s1, s2, p1, p2, neg, p1t, key_idx, cand_idx, tie_break):
    n = len(s1)
    vals, ranks = _take_top(list(s1) + list(s2), key_idx, TOPK, tie_break)
    v1, v2, rank1, rank2 = vals[:n], vals[n:], ranks[:n], ranks[n:]
    cand = [_dot_f32(p1, v1[i], ((1,), (0,))) + _dot_f32(p2, v2[i], ((1,), (0,))) + neg for i in range(n)]
    _, crank = _take_top(cand, cand_idx, TOPK, tie_break)
    out = []
    for i in range(n):
        chosen = crank[i] < float(TOPK)
        cmax = v1[i][0:1, :] + v2[i][0:1, :]
        z = jnp.sum(jnp.where(chosen, jnp.exp(cand[i] - cmax), 0.0), axis=0, keepdims=True)
        width = jnp.dot(p1t, chosen.astype(F32), preferred_element_type=F32)
        b1 = jnp.zeros_like(rank1[i])
        for a in range(TOPK):
            b1 = jnp.where(rank1[i] == float(a), width[a:a + 1, :], b1)
        in1 = rank1[i] < float(TOPK)
        in2 = rank2[i] < float(TOPK)
        c1 = jnp.where(in1, jnp.exp(jnp.where(in1, s1[i] - v1[i][0:1, :], 0.0)) / z, 0.0)
        e2 = jnp.where(in2, jnp.exp(jnp.where(in2, s2[i] - v2[i][0:1, :], 0.0)), 0.0)
        clean = ((_count(in1) == float(TOPK)) & (_count(in2) == float(TOPK))
                 & (_count(chosen) == float(TOPK)))
        out.append((c1, b1, rank2[i], e2, clean))
    return out


ROUTE_HEADS = 4


def _route_body(q_ref, keys_ref, p1_ref, p2_ref, neg_ref, p1t_ref, u_ref, v_ref,
                r2_ref, e2_ref, c1_ref, b1_ref, ub_ref, vt_ref, q3_ref):
    ub_ref[...] = u_ref[0].astype(BF16)
    vt_ref[...] = v_ref[0].T.astype(BF16)

    for hp in range(2 * P_HEADS):
        q3_ref[hp] = q_ref[:, hp * LANES:(hp + 1) * LANES]
    key_idx = lax.broadcasted_iota(jnp.int32, (N_KEYS, LANES), 0).astype(F32)
    cand_idx = lax.broadcasted_iota(jnp.int32, (p1_ref.shape[0], LANES), 0).astype(F32)

    def emit(heads, tie_break):
        s1 = [_dot_f32(keys_ref[h, 0], q3_ref[2 * h], ((1,), (1,))) for h in heads]
        s2 = [_dot_f32(keys_ref[h, 1], q3_ref[2 * h + 1], ((1,), (1,))) for h in heads]
        rows = _route_heads(s1, s2, p1_ref[...], p2_ref[...], neg_ref[...], p1t_ref[...],
                            key_idx, cand_idx, tie_break)
        all_clean = None
        for h, (c1, b1, r2, e2, clean) in zip(heads, rows):
            c1_ref[h] = c1
            b1_ref[h] = b1
            r2_ref[h] = r2.astype(r2_ref.dtype)
            e2_ref[h] = e2.astype(e2_ref.dtype)
            all_clean = clean if all_clean is None else all_clean & clean
        return all_clean

    def step(i, carry):
        heads = [i * ROUTE_HEADS + j for j in range(ROUTE_HEADS)]
        all_clean = emit(heads, False)

        @pl.when(jnp.min(all_clean.astype(F32)) < 0.5)
        def _():
            emit(heads, True)

        return carry

    lax.fori_loop(0, P_HEADS // ROUTE_HEADS, step, 0)


def _route(q, keys, table_u, table_v, layer):
    n = q.shape[0]
    _, n_exp, d = table_u.shape
    steps = n // LANES
    n_slabs = min(n_exp // TABLE_SLAB, 1 << (steps.bit_length() - 1))
    slab = n_exp // n_slabs
    assert n_slabs * slab == n_exp
    p1, p2, neg, p1t = _candidate_tables()
    whole = lambda a: pl.BlockSpec(a.shape, lambda i: (0,) * a.ndim)
    out_spec = pl.BlockSpec((P_HEADS, N_KEYS, LANES), lambda i: (0, 0, i))
    shape = lambda dt: jax.ShapeDtypeStruct((P_HEADS, N_KEYS, n), dt)
    slab_of = lambda i: jnp.minimum(i, n_slabs - 1)
    table_spec = pl.BlockSpec((1, slab, d), lambda i: (layer, slab_of(i), 0))
    return pl.pallas_call(
        _route_body,
        grid=(steps,),
        in_specs=[pl.BlockSpec((LANES, q.shape[1]), lambda i: (i, 0)), whole(keys),
                  whole(p1), whole(p2), whole(neg), whole(p1t), table_spec, table_spec],
        out_specs=[out_spec] * 4 + [pl.BlockSpec((slab, d), lambda i: (slab_of(i), 0)),
                                    pl.BlockSpec((d, slab), lambda i: (0, slab_of(i)))],
        out_shape=[shape(BF16), shape(BF16), shape(F32), shape(F32),
                   jax.ShapeDtypeStruct((n_exp, d), BF16), jax.ShapeDtypeStruct((d, n_exp), BF16)],
        scratch_shapes=[pltpu.VMEM((2 * P_HEADS, LANES, LANES), F32)],
        compiler_params=_cparams(("arbitrary",)),
        name="peer_route",
    )(q, keys, jnp.asarray(p1), jnp.asarray(p2), jnp.asarray(neg), jnp.asarray(p1t), table_u, table_v)


def _bf16_pair(c):
    hi = float(np.asarray(c, dtype=BF16))
    return hi, float(np.asarray(c - hi, dtype=BF16))


def _gelu_tanh(x):
    c0 = float(np.sqrt(2.0 / np.pi))
    c0_hi, c0_lo = _bf16_pair(c0)
    c1_hi, c1_lo = _bf16_pair(c0 * 0.044715)
    x2 = x * x
    inner = x * ((c1_hi * x2 + c0_hi) + (c1_lo * x2 + c0_lo))
    half = 0.5 * x
    return half + half * jnp.tanh(inner)


def _peer_dense_body(split, xt_ref, u_ref, vt_ref, r2_ref, e2_ref, c1_ref, b1_ref, x_ref, gf_ref, *rest):
    if split is None:
        o_ref, acc_ref = rest
    else:
        o_ref, ys_ref, acc_ref = rest
    e = pl.program_id(1)
    n_groups = xt_ref.shape[1] // PEER_LANES

    @pl.when(e == 0)
    def _():
        acc_ref[...] = jnp.zeros_like(acc_ref)

    def scores(j):
        cols = slice(j * PEER_LANES, (j + 1) * PEER_LANES)
        return jnp.dot(u_ref[...], xt_ref[:, cols], preferred_element_type=F32)

    def gates(j):
        cols = slice(j * PEER_LANES, (j + 1) * PEER_LANES)
        parts = []
        for r in range(EXPERT_BLOCK // N_KEYS):
            gate = jnp.zeros((N_KEYS, PEER_LANES), BF16)
            for h in range(P_HEADS):
                hit = r2_ref[h, :, cols] < b1_ref[h, r:r + 1, cols].astype(BF16)
                gate = gate + c1_ref[h, r:r + 1, cols].astype(BF16) * jnp.where(
                    hit, e2_ref[h, :, cols], jnp.zeros((), BF16))
            parts.append(gate)
        return jnp.concatenate(parts, axis=0)

    s_next = scores(0)
    for j in range(n_groups):
        s = s_next
        if j + 1 < n_groups:
            s_next = scores(j + 1)
        cols = slice(j * PEER_LANES, (j + 1) * PEER_LANES)
        weighted = gates(j) * _gelu_tanh(s.astype(BF16))
        acc_ref[:, cols] += jnp.dot(vt_ref[...], weighted, preferred_element_type=F32)

    @pl.when(e == pl.num_programs(1) - 1)
    def _():
        for j in range(n_groups):
            rows = slice(j * PEER_LANES, (j + 1) * PEER_LANES)
            out = x_ref[rows, :] + acc_ref[:, rows].T
            if split is None:
                o_ref[rows, :] = out
                continue
            y = _rmsnorm(out, gf_ref[...])
            o_ref[rows, :] = y
            if j == split.group:
                @pl.when(pl.program_id(0) == split.block)
                def _():
                    ys_ref[...] = y[:ys_ref.shape[0], :]


class _SampleSplit(NamedTuple):
    block: int
    group: int


def _peer_dense(xt, u_b, vt_b, r2, e2, c1, b1, x, g_final, n_prompt=None, n_sample=None):
    d, n = xt.shape
    n_exp = u_b.shape[0]
    keys_per_block = EXPERT_BLOCK // N_KEYS
    once = pl.Buffered(1)
    rank_spec = pl.BlockSpec((P_HEADS, N_KEYS, TOK_BLOCK), lambda i, e: (0, 0, i), pipeline_mode=once)
    first_spec = pl.BlockSpec((P_HEADS, keys_per_block, TOK_BLOCK), lambda i, e: (0, e, i))
    row_spec = pl.BlockSpec((TOK_BLOCK, d), lambda i, e: (i, 0), pipeline_mode=once)
    if n_prompt is None:
        split = None
        out_specs = row_spec
        out_shape = jax.ShapeDtypeStruct((n, d), F32)
    else:
        start = n_prompt % TOK_BLOCK
        assert start % PEER_LANES == 0 and n_sample <= PEER_LANES and n_prompt + n_sample <= n
        split = _SampleSplit(n_prompt // TOK_BLOCK, start // PEER_LANES)
        out_specs = [row_spec, pl.BlockSpec((n_sample, d), lambda i, e: (0, 0))]
        out_shape = [jax.ShapeDtypeStruct((n_prompt, d), F32), jax.ShapeDtypeStruct((n_sample, d), F32)]
    return pl.pallas_call(
        functools.partial(_peer_dense_body, split),
        grid=(n // TOK_BLOCK, n_exp // EXPERT_BLOCK),
        in_specs=[pl.BlockSpec((d, TOK_BLOCK), lambda i, e: (0, i), pipeline_mode=once),
                  pl.BlockSpec((EXPERT_BLOCK, d), lambda i, e: (e, 0)),
                  pl.BlockSpec((d, EXPERT_BLOCK), lambda i, e: (0, e)),
                  rank_spec, rank_spec, first_spec, first_spec, row_spec,
                  pl.BlockSpec((1, d), lambda i, e: (0, 0))],
        out_specs=out_specs,
        out_shape=out_shape,
        scratch_shapes=[pltpu.VMEM((d, TOK_BLOCK), F32)],
        compiler_params=_cparams(("arbitrary", "arbitrary")),
        name="peer_dense" if split is None else "peer_dense_final",
    )(xt, u_b, vt_b, r2, e2, c1, b1, x, g_final)


def _peer(x, g_norm, wq, keys, table_u, table_v, layer, g_final, n_prompt=None, n_sample=None):
    q, xt = _norm_matmul_t(x, g_norm, wq, 1024)
    r2, e2, c1, b1, u_b, vt_b = _route(q, keys, table_u, table_v, layer)
    return _peer_dense(xt, u_b, vt_b, r2, e2, c1, b1, x, g_final, n_prompt, n_sample)


def _col_form(a, n_dk):
    ns = a.shape[0]
    a = a.reshape(ns // SAMPLE_BLOCK, SAMPLE_BLOCK, H_A, n_dk)
    return a.transpose(0, 2, 3, 1)


def kernel(x_prompt, x_sample, state_gla, state_conv_b, state_conv_c, norm_mix, norm_ffn, norm_final, w_in_even, w_a2, b_a2, gla_norm_g, conv_b_w, conv_b_bias, conv_b_ln_g, conv_b_ln_b, w_out_even, w_in_odd, conv_c_w, w_out_odd, peer_wq, peer_keys, peer_u, peer_v):
    bsz, t, d = x_prompt.shape
    ns = x_sample.shape[0]
    n_prompt = bsz * t
    n_real = n_prompt + ns
    n_tok = -(-n_real // TOK_BLOCK) * TOK_BLOCK
    assert t % GLA_CHUNK == 0 and t % CONV_ROWS == 0 and ns % (2 * SAMPLE_BLOCK) == 0
    assert x_sample.shape[1] == 1 and n_prompt % (2 * SAMPLE_BLOCK) == 0

    d_a = d // 2
    dk_a = d_a // 2
    d_b = d - d_a
    rank = w_a2.shape[1]
    col_a = 2 * dk_a + 2 * d_a
    row = lambda a: a.reshape(1, -1)

    x = jnp.concatenate([x_prompt.reshape(n_prompt, d), x_sample.reshape(ns, d),
                         jnp.zeros((n_tok - n_real, d), F32)], axis=0)

    w_in = w_in_even[0]
    w_main = jnp.concatenate([w_in[:, :col_a], w_in[:, col_a + rank:]], axis=1).astype(BF16)
    w_gate = jnp.pad(w_in[:, col_a:col_a + rank], ((0, 0), (0, LANES - rank))).astype(BF16)
    w_a2p = jnp.pad(w_a2[0], ((0, LANES - rank), (0, 0)))
    z, la = _norm_matmul_gate(x, row(norm_mix[0]), w_main, w_gate, w_a2p, row(b_a2[0]), 1024)

    gn = row(gla_norm_g[0])
    zs = z[n_prompt:n_real]
    las = la[n_prompt:n_real]
    oa, gla_s = _gla_sample(state_gla, _col_form(las, dk_a // H_A), _col_form(zs[:, dk_a:2 * dk_a], dk_a // H_A),
                            _col_form(zs[:, :dk_a], dk_a // H_A),
                            zs[:, 2 * dk_a:2 * dk_a + d_a], zs[:, 2 * dk_a + d_a:col_a], gn, n_tok, n_prompt)
    oa, gla_p = _gla_prompt(z, la, gn, bsz, t, oa)

    ub, cb_s = _conv_b_sample(state_conv_b[0].reshape(ns, -1), z, conv_b_w[0], row(conv_b_bias[0]),
                              n_prompt, col_a, d_b)
    ub, cb_p = _conv_b_prompt(z, conv_b_w[0], row(conv_b_bias[0]), bsz, t, col_a, d_b, ub)
    x = _res_matmul_ln(oa, ub, row(conv_b_ln_g[0]), row(conv_b_ln_b[0]), w_out_even[0].astype(BF16), x, 1024)
    wq_b = peer_wq.astype(BF16)
    x = _peer(x, row(norm_ffn[0]), wq_b[0], peer_keys[0], peer_u, peer_v, 0, row(norm_final))

    z = _norm_matmul(x, row(norm_mix[1]), w_in_odd[0].astype(BF16), 1024)
    gc, cc_s = _conv_c_sample(state_conv_c[0].reshape(ns, -1), z, conv_c_w[0], n_prompt, d)
    gc, cc_p = _conv_c_prompt(z, conv_c_w[0], bsz, t, d, gc)
    x = _res_matmul(gc, w_out_odd[0].astype(BF16), x, 1024)
    y_prompt, y_sample = _peer(x, row(norm_ffn[1]), wq_b[1], peer_keys[1], peer_u, peer_v, 1, row(norm_final),
                               n_prompt, ns)
    return (y_prompt.reshape(bsz, t, d), y_sample.reshape(ns, 1, d), gla_p, cb_p, cc_p,
            gla_s, cb_s.reshape(state_conv_b.shape), cc_s.reshape(state_conv_c.shape))
```

```python
import functools
from typing import NamedTuple

import numpy as np
import jax
import jax.numpy as jnp
from jax import lax
from jax.experimental import pallas as pl
from jax.experimental.pallas import tpu as pltpu

F32 = jnp.float32
BF16 = jnp.bfloat16
HIGHEST = lax.Precision.HIGHEST

EPS = 1e-6
GATE_TAU = 16.0
H_A = 4
P_HEADS = 8
N_KEYS = 128
TOPK = 16
CONV_B_W = 31
CONV_C_W = 3

LANES = 128
SUBLANES = 8
VMEM_LIMIT = 56 * 1024 * 1024

TOK_BLOCK = 768
EXPERT_BLOCK = 1024
PEER_LANES = 256
TABLE_SLAB = 256
GLA_CHUNK = 128
GLA_SUB = 16
EXP_CLAMP = 80.0
CONV_ROWS = 64
SAMPLE_BLOCK = 8


def _cparams(sem):
    return pltpu.CompilerParams(dimension_semantics=sem, vmem_limit_bytes=VMEM_LIMIT)


def _rmsnorm(x, g):
    return x * lax.rsqrt(jnp.mean(x * x, axis=-1, keepdims=True) + EPS) * g


def _sigmoid(x):
    return jax.nn.sigmoid(x)


def _weight_spec(k, m, tn):
    if tn == m:
        return pl.BlockSpec((k, tn), lambda i, j: (0, 0), pipeline_mode=pl.Buffered(1))
    return pl.BlockSpec((k, tn), lambda i, j: (0, j))


def _log_sigmoid(x):
    return jnp.minimum(x, 0.0) - jnp.log(1.0 + jnp.exp(-jnp.abs(x)))


def _norm_matmul_body(x_ref, g_ref, w_ref, o_ref, xn_ref):
    @pl.when(pl.program_id(1) == 0)
    def _():
        xn_ref[...] = _rmsnorm(x_ref[...], g_ref[...]).astype(BF16)

    o_ref[...] = jnp.dot(xn_ref[...], w_ref[...], preferred_element_type=F32)


def _norm_matmul(x, g, w, tn):
    n, d = x.shape
    m = w.shape[1]
    return pl.pallas_call(
        _norm_matmul_body,
        grid=(n // TOK_BLOCK, m // tn),
        in_specs=[pl.BlockSpec((TOK_BLOCK, d), lambda i, j: (i, 0)),
                  pl.BlockSpec((1, d), lambda i, j: (0, 0)),
                  pl.BlockSpec((d, tn), lambda i, j: (0, j))],
        out_specs=pl.BlockSpec((TOK_BLOCK, tn), lambda i, j: (i, j)),
        out_shape=jax.ShapeDtypeStruct((n, m), F32),
        scratch_shapes=[pltpu.VMEM((TOK_BLOCK, d), BF16)],
        compiler_params=_cparams(("parallel", "arbitrary")),
        name="norm_matmul",
    )(x, g, w)


def _norm_matmul_gate_body(x_ref, g_ref, w_ref, wa_ref, wa2_ref, ba2_ref, o_ref, la_ref, xn_ref):
    @pl.when(pl.program_id(1) == 0)
    def _():
        xn = _rmsnorm(x_ref[...], g_ref[...]).astype(BF16)
        xn_ref[...] = xn
        a_lr = jnp.dot(xn, wa_ref[...], preferred_element_type=F32)
        pre = jnp.dot(a_lr, wa2_ref[...], precision=HIGHEST, preferred_element_type=F32) + ba2_ref[...]
        la_ref[...] = _log_sigmoid(pre) / GATE_TAU

    o_ref[...] = jnp.dot(xn_ref[...], w_ref[...], preferred_element_type=F32)


def _norm_matmul_gate(x, g, w, wa, wa2, ba2, tn):
    n, d = x.shape
    m = w.shape[1]
    dk = wa2.shape[1]
    return pl.pallas_call(
        _norm_matmul_gate_body,
        grid=(n // TOK_BLOCK, m // tn),
        in_specs=[pl.BlockSpec((TOK_BLOCK, d), lambda i, j: (i, 0)),
                  pl.BlockSpec((1, d), lambda i, j: (0, 0)),
                  pl.BlockSpec((d, tn), lambda i, j: (0, j)),
                  pl.BlockSpec((d, LANES), lambda i, j: (0, 0)),
                  pl.BlockSpec((LANES, dk), lambda i, j: (0, 0)),
                  pl.BlockSpec((1, dk), lambda i, j: (0, 0))],
        out_specs=[pl.BlockSpec((TOK_BLOCK, tn), lambda i, j: (i, j)),
                   pl.BlockSpec((TOK_BLOCK, dk), lambda i, j: (i, 0))],
        out_shape=[jax.ShapeDtypeStruct((n, m), F32), jax.ShapeDtypeStruct((n, dk), F32)],
        scratch_shapes=[pltpu.VMEM((TOK_BLOCK, d), BF16)],
        compiler_params=_cparams(("parallel", "arbitrary")),
        name="norm_matmul_gate",
    )(x, g, w, wa, wa2, ba2)


def _norm_matmul_t_body(x_ref, g_ref, w_ref, o_ref, xt_ref, xn_ref):
    @pl.when(pl.program_id(1) == 0)
    def _():
        xn = _rmsnorm(x_ref[...], g_ref[...])
        xn_ref[...] = xn.astype(BF16)
        xt_ref[...] = xn.T.astype(BF16)

    o_ref[...] = jnp.dot(xn_ref[...], w_ref[...], preferred_element_type=F32)


def _norm_matmul_t(x, g, w, tn):
    n, d = x.shape
    m = w.shape[1]
    return pl.pallas_call(
        _norm_matmul_t_body,
        grid=(n // TOK_BLOCK, m // tn),
        in_specs=[pl.BlockSpec((TOK_BLOCK, d), lambda i, j: (i, 0)),
                  pl.BlockSpec((1, d), lambda i, j: (0, 0)),
                  _weight_spec(d, m, tn)],
        out_specs=[pl.BlockSpec((TOK_BLOCK, tn), lambda i, j: (i, j)),
                   pl.BlockSpec((d, TOK_BLOCK), lambda i, j: (0, i))],
        out_shape=[jax.ShapeDtypeStruct((n, m), F32), jax.ShapeDtypeStruct((d, n), BF16)],
        scratch_shapes=[pltpu.VMEM((TOK_BLOCK, d), BF16)],
        compiler_params=_cparams(("parallel", "arbitrary")),
        name="norm_matmul_t",
    )(x, g, w)


def _res_matmul_body(a_ref, w_ref, x_ref, o_ref, ab_ref):
    @pl.when(pl.program_id(1) == 0)
    def _():
        ab_ref[...] = a_ref[...].astype(BF16)

    o_ref[...] = x_ref[...] + jnp.dot(ab_ref[...], w_ref[...], preferred_element_type=F32)


def _res_matmul(a, w, x, tn):
    n, k = a.shape
    m = w.shape[1]
    return pl.pallas_call(
        _res_matmul_body,
        grid=(n // TOK_BLOCK, m // tn),
        in_specs=[pl.BlockSpec((TOK_BLOCK, k), lambda i, j: (i, 0)),
                  _weight_spec(k, m, tn),
                  pl.BlockSpec((TOK_BLOCK, tn), lambda i, j: (i, j))],
        out_specs=pl.BlockSpec((TOK_BLOCK, tn), lambda i, j: (i, j)),
        out_shape=jax.ShapeDtypeStruct((n, m), F32),
        scratch_shapes=[pltpu.VMEM((TOK_BLOCK, k), BF16)],
        compiler_params=_cparams(("parallel", "arbitrary")),
        name="res_matmul",
    )(a, w, x)


def _res_matmul_ln_body(oa_ref, ub_ref, lg_ref, lb_ref, w_ref, x_ref, o_ref, ab_ref):
    da = oa_ref.shape[1]

    @pl.when(pl.program_id(1) == 0)
    def _():
        u = ub_ref[...]
        uc = u - jnp.mean(u, axis=-1, keepdims=True)
        y = uc * lax.rsqrt(jnp.mean(uc * uc, axis=-1, keepdims=True) + EPS) * lg_ref[...] + lb_ref[...]
        ab_ref[:, :da] = oa_ref[...].astype(BF16)
        ab_ref[:, da:] = (y * _sigmoid(y)).astype(BF16)

    o_ref[...] = x_ref[...] + jnp.dot(ab_ref[...], w_ref[...], preferred_element_type=F32)


def _res_matmul_ln(oa, ub, lg, lb, w, x, tn):
    n, da = oa.shape
    db = ub.shape[1]
    m = w.shape[1]
    return pl.pallas_call(
        _res_matmul_ln_body,
        grid=(n // TOK_BLOCK, m // tn),
        in_specs=[pl.BlockSpec((TOK_BLOCK, da), lambda i, j: (i, 0)),
                  pl.BlockSpec((TOK_BLOCK, db), lambda i, j: (i, 0)),
                  pl.BlockSpec((1, db), lambda i, j: (0, 0)),
                  pl.BlockSpec((1, db), lambda i, j: (0, 0)),
                  _weight_spec(da + db, m, tn),
                  pl.BlockSpec((TOK_BLOCK, tn), lambda i, j: (i, j))],
        out_specs=pl.BlockSpec((TOK_BLOCK, tn), lambda i, j: (i, j)),
        out_shape=jax.ShapeDtypeStruct((n, m), F32),
        scratch_shapes=[pltpu.VMEM((TOK_BLOCK, da + db), BF16)],
        compiler_params=_cparams(("parallel", "arbitrary")),
        name="res_matmul_ln",
    )(oa, ub, lg, lb, w, x)


def _dot_f32(a, b, dims):
    return lax.dot_general(a, b, (dims, ((), ())), precision=HIGHEST, preferred_element_type=F32)


def _dot_bf16(a, b, dims):
    return lax.dot_general(a.astype(BF16), b.astype(BF16), (dims, ((), ())), preferred_element_type=F32)


def _head_out(o, gn, g):
    y = o * lax.rsqrt(jnp.mean(o * o, axis=-1, keepdims=True) + EPS) * gn
    return y * (g * _sigmoid(g))


def _gla_prompt_body(q_ref, k_ref, v_ref, g_ref, la_ref, gn_ref, dst_ref, o_ref, s_out_ref, s_ref, acc_ref):
    del dst_ref
    c = pl.program_id(1)
    dk = GLA_CHUNK
    dv = s_ref.shape[2]

    @pl.when(c == 0)
    def _():
        s_ref[...] = jnp.zeros_like(s_ref)

    row = lax.broadcasted_iota(jnp.int32, (GLA_CHUNK, GLA_CHUNK), 0)
    col = lax.broadcasted_iota(jnp.int32, (GLA_CHUNK, GLA_CHUNK), 1)
    tri = (row >= col).astype(F32)
    sub_row = lax.broadcasted_iota(jnp.int32, (GLA_SUB, GLA_SUB), 0)
    sub_col = lax.broadcasted_iota(jnp.int32, (GLA_SUB, GLA_SUB), 1)
    causal = sub_row >= sub_col

    for h in range(H_A):
        kcols = slice(h * dk, (h + 1) * dk)
        vcols = slice(h * dv, (h + 1) * dv)
        la = la_ref[:, kcols]
        b = _dot_f32(tri, la, ((1,), (0,)))
        b_last = b[GLA_CHUNK - 1:GLA_CHUNK, :]
        q = q_ref[:, kcols] * (dk ** -0.5)
        k = k_ref[:, kcols]
        v_b = v_ref[:, vcols].astype(BF16)
        s = s_ref[h]

        acc_ref[h] = _dot_bf16(q * jnp.exp(b), s, ((1,), (0,)))
        for j in range(GLA_CHUNK // GLA_SUB):
            lo, hi = j * GLA_SUB, (j + 1) * GLA_SUB
            b_j = b[lo:hi, :]
            b_first = b[lo:lo + 1, :]
            b_end = b[hi - 1:hi, :]
            q_d = q[lo:hi, :] * jnp.exp(b_j - b_first)
            k_d = k[lo:hi, :] * jnp.exp(jnp.minimum(b_first - b_j, EXP_CLAMP))
            sc = jnp.where(causal, _dot_bf16(q_d, k_d, ((1,), (1,))), 0.0)
            acc_ref[h, lo:hi, :] += _dot_bf16(sc, v_b[lo:hi, :], ((1,), (0,)))
            if hi < GLA_CHUNK:
                k_p = k[lo:hi, :] * jnp.exp(b_end - b_j)
                q_p = q[hi:, :] * jnp.exp(b[hi:, :] - b_end)
                sc = _dot_bf16(q_p, k_p, ((1,), (1,)))
                acc_ref[h, hi:, :] += _dot_bf16(sc, v_b[lo:hi, :], ((1,), (0,)))

        decay = jnp.exp(jnp.broadcast_to(b_last, (dk, dk))).T
        k_e = k * jnp.exp(b_last - b)
        s_new = jnp.concatenate([decay] * (dv // dk), axis=1) * s + _dot_bf16(k_e.T, v_b, ((1,), (0,)))
        s_ref[h] = s_new
        o_ref[:, vcols] = _head_out(acc_ref[h], gn_ref[:, vcols], g_ref[:, vcols])

    @pl.when(c == pl.num_programs(1) - 1)
    def _():
        s_out_ref[0, 0] = s_ref[...]


def _gla_prompt(z, la, gn, bsz, t, dst):
    dk, dv = GLA_CHUNK, 2 * GLA_CHUNK
    d_k, d_v = H_A * dk, H_A * dv
    nc = t // GLA_CHUNK
    rows = lambda b, c: b * nc + c
    return pl.pallas_call(
        _gla_prompt_body,
        grid=(bsz, nc),
        in_specs=[pl.BlockSpec((GLA_CHUNK, d_k), lambda b, c: (rows(b, c), 0)),
                  pl.BlockSpec((GLA_CHUNK, d_k), lambda b, c: (rows(b, c), 1)),
                  pl.BlockSpec((GLA_CHUNK, d_v), lambda b, c: (rows(b, c), 2 * d_k // d_v)),
                  pl.BlockSpec((GLA_CHUNK, d_v), lambda b, c: (rows(b, c), 2 * d_k // d_v + 1)),
                  pl.BlockSpec((GLA_CHUNK, d_k), lambda b, c: (rows(b, c), 0)),
                  pl.BlockSpec((1, d_v), lambda b, c: (0, 0)),
                  pl.BlockSpec(memory_space=pl.ANY)],
        out_specs=[pl.BlockSpec((GLA_CHUNK, d_v), lambda b, c: (rows(b, c), 0)),
                   pl.BlockSpec((1, 1, H_A, dk, dv), lambda b, c: (0, b, 0, 0, 0))],
        out_shape=[jax.ShapeDtypeStruct(dst.shape, F32),
                   jax.ShapeDtypeStruct((1, bsz, H_A, dk, dv), F32)],
        input_output_aliases={6: 0},
        scratch_shapes=[pltpu.VMEM((H_A, dk, dv), F32), pltpu.VMEM((H_A, GLA_CHUNK, dv), F32)],
        compiler_params=_cparams(("parallel", "arbitrary")),
        name="gla_prompt",
    )(z, z, z, z, la, gn, dst)


def _tail_spec(n_rows, row0, cols):
    tail = n_rows - row0
    assert row0 % tail == 0 and tail % SUBLANES == 0
    return pl.BlockSpec((tail, cols), lambda i: (row0 // tail, 0))


def _zero_tail_rows(o_ref, n_valid):
    @pl.when(pl.program_id(0) == 0)
    def _():
        o_ref[n_valid:, :] = jnp.zeros((o_ref.shape[0] - n_valid, o_ref.shape[1]), F32)


def _gla_sample_body(n_valid, st_ref, lat_ref, kt_ref, qt_ref, v_ref, g_ref, gn_ref, o_ref, s_out_ref, rows_ref):
    dv = st_ref.shape[-1]
    _zero_tail_rows(o_ref, n_valid)
    for i in range(SAMPLE_BLOCK):
        for h in range(H_A):
            decay = jnp.exp(lat_ref[0, h, :, i:i + 1])
            v_row = v_ref[i:i + 1, h * dv:(h + 1) * dv]
            s_new = decay * st_ref[0, i, h] + kt_ref[0, h, :, i:i + 1] * v_row
            s_out_ref[0, i, h] = s_new
            q_col = qt_ref[0, h, :, i:i + 1] * (st_ref.shape[-2] ** -0.5)
            o = jnp.sum(q_col * s_new, axis=0, keepdims=True)
            rows_ref[i:i + 1, h * dv:(h + 1) * dv] = _head_out(
                o, gn_ref[:, h * dv:(h + 1) * dv], g_ref[i:i + 1, h * dv:(h + 1) * dv])
    base = pl.multiple_of(pl.program_id(0) * SAMPLE_BLOCK, SAMPLE_BLOCK)
    o_ref[pl.ds(base, SAMPLE_BLOCK), :] = rows_ref[...]


def _gla_sample(state, lat, kt, qt, v, g, gn, n_rows, row0):
    _, ns, _, dk, dv = state.shape
    steps = ns // SAMPLE_BLOCK
    col_spec = pl.BlockSpec((1, H_A, dk, SAMPLE_BLOCK), lambda i: (i, 0, 0, 0))
    row_spec = pl.BlockSpec((SAMPLE_BLOCK, H_A * dv), lambda i: (i, 0))
    st_spec = pl.BlockSpec((1, SAMPLE_BLOCK, H_A, dk, dv), lambda i: (0, i, 0, 0, 0))
    return pl.pallas_call(
        functools.partial(_gla_sample_body, ns),
        grid=(steps,),
        in_specs=[st_spec, col_spec, col_spec, col_spec, row_spec, row_spec,
                  pl.BlockSpec((1, H_A * dv), lambda i: (0, 0))],
        out_specs=[_tail_spec(n_rows, row0, H_A * dv), st_spec],
        out_shape=[jax.ShapeDtypeStruct((n_rows, H_A * dv), F32), jax.ShapeDtypeStruct(state.shape, F32)],
        scratch_shapes=[pltpu.VMEM((SAMPLE_BLOCK, H_A * dv), F32)],
        compiler_params=_cparams(("arbitrary",)),
        name="gla_sample",
    )(state, lat, kt, qt, v, g, gn)


def _fill_padded(pad_ref, hist, t, value):
    cols = pad_ref.shape[1]
    pad_ref[0:hist, :] = jnp.zeros((hist, cols), F32)
    pad_ref[hist:hist + t, :] = value
    pad_ref[hist + t:, :] = jnp.zeros((SUBLANES, cols), F32)


def _causal_taps(pad_ref, w_ref, base, hist, n_taps, acc):
    first = hist - (n_taps - 1)
    win = pad_ref[pl.ds(base, CONV_ROWS + hist + SUBLANES), :]
    for r in range(SUBLANES):
        taps = [w for w in range(n_taps) if (first + w) % SUBLANES == r]
        if not taps:
            continue
        shifted = win[r:r + CONV_ROWS + hist, :]
        for w in taps:
            lo = (first + w) // SUBLANES * SUBLANES
            acc = acc + shifted[lo:lo + CONV_ROWS, :] * w_ref[w:w + 1, :]
    return acc


def _conv_b_prompt_body(a_ref, b_ref, w_ref, bias_ref, dst_ref, o_ref, buf_ref, pad_ref):
    del dst_ref
    t = a_ref.shape[0]
    hist = 4 * SUBLANES
    _fill_padded(pad_ref, hist, t, a_ref[...] * _sigmoid(b_ref[...]))
    buf_ref[0, 0] = pad_ref[hist + t - (CONV_B_W - 1):hist + t, :]

    def step(i, carry):
        base = pl.multiple_of(i * CONV_ROWS, CONV_ROWS)
        acc = jnp.zeros((CONV_ROWS, pad_ref.shape[1]), F32) + bias_ref[...]
        o_ref[pl.ds(base, CONV_ROWS), :] = _causal_taps(pad_ref, w_ref, base, hist, CONV_B_W, acc)
        return carry

    lax.fori_loop(0, t // CONV_ROWS, step, 0)


def _conv_b_prompt(z, w, bias, bsz, t, col0, d_b, dst):
    cb = 256
    nb = d_b // cb
    a0, b0 = col0 // cb, (col0 + d_b) // cb
    return pl.pallas_call(
        _conv_b_prompt_body,
        grid=(bsz, nb),
        in_specs=[pl.BlockSpec((t, cb), lambda b, j: (b, a0 + j)),
                  pl.BlockSpec((t, cb), lambda b, j: (b, b0 + j)),
                  pl.BlockSpec((CONV_B_W, cb), lambda b, j: (0, j)),
                  pl.BlockSpec((1, cb), lambda b, j: (0, j)),
                  pl.BlockSpec(memory_space=pl.ANY)],
        out_specs=[pl.BlockSpec((t, cb), lambda b, j: (b, j)),
                   pl.BlockSpec((1, 1, CONV_B_W - 1, cb), lambda b, j: (0, b, 0, j))],
        out_shape=[jax.ShapeDtypeStruct(dst.shape, F32),
                   jax.ShapeDtypeStruct((1, bsz, CONV_B_W - 1, d_b), F32)],
        input_output_aliases={4: 0},
        scratch_shapes=[pltpu.VMEM((t + 5 * SUBLANES, cb), F32)],
        compiler_params=_cparams(("parallel", "parallel")),
        name="conv_b_prompt",
    )(z, z, w, bias, dst)


def _conv_b_sample_body(n_valid, buf_ref, a_ref, b_ref, w_ref, bias_ref, o_ref, nbuf_ref):
    d_b = a_ref.shape[1]
    blk = a_ref.shape[0]
    _zero_tail_rows(o_ref, n_valid)
    glu = a_ref[...] * _sigmoid(b_ref[...])
    acc = bias_ref[...] + glu * w_ref[CONV_B_W - 1:CONV_B_W, :]
    for w in range(CONV_B_W - 1):
        acc = acc + buf_ref[:, w * d_b:(w + 1) * d_b] * w_ref[w:w + 1, :]
    o_ref[pl.ds(pl.multiple_of(pl.program_id(0) * blk, blk), blk), :] = acc
    nbuf_ref[:, :(CONV_B_W - 2) * d_b] = buf_ref[:, d_b:]
    nbuf_ref[:, (CONV_B_W - 2) * d_b:] = glu


def _conv_b_sample(buf, z, w, bias, row0, col0, d_b):
    ns = buf.shape[0]
    blk = 2 * SAMPLE_BLOCK
    r0 = row0 // blk
    return pl.pallas_call(
        functools.partial(_conv_b_sample_body, ns),
        grid=(ns // blk,),
        in_specs=[pl.BlockSpec((blk, buf.shape[1]), lambda i: (i, 0)),
                  pl.BlockSpec((blk, d_b), lambda i: (r0 + i, col0 // d_b)),
                  pl.BlockSpec((blk, d_b), lambda i: (r0 + i, col0 // d_b + 1)),
                  pl.BlockSpec((CONV_B_W, d_b), lambda i: (0, 0)),
                  pl.BlockSpec((1, d_b), lambda i: (0, 0))],
        out_specs=[_tail_spec(z.shape[0], row0, d_b),
                   pl.BlockSpec((blk, buf.shape[1]), lambda i: (i, 0))],
        out_shape=[jax.ShapeDtypeStruct((z.shape[0], d_b), F32), jax.ShapeDtypeStruct(buf.shape, F32)],
        compiler_params=_cparams(("arbitrary",)),
        name="conv_b_sample",
    )(buf, z, z, w, bias)


def _conv_c_prompt_body(gb_ref, gc_ref, hv_ref, w_ref, dst_ref, o_ref, buf_ref, pad_ref):
    del dst_ref
    t = gb_ref.shape[0]
    hist = SUBLANES
    _fill_padded(pad_ref, hist, t, gc_ref[...] * hv_ref[...])
    buf_ref[0, 0] = pad_ref[hist + t - (CONV_C_W - 1):hist + t, :]

    def step(i, carry):
        base = pl.multiple_of(i * CONV_ROWS, CONV_ROWS)
        acc = jnp.zeros((CONV_ROWS, pad_ref.shape[1]), F32)
        acc = _causal_taps(pad_ref, w_ref, base, hist, CONV_C_W, acc)
        o_ref[pl.ds(base, CONV_ROWS), :] = acc * gb_ref[pl.ds(base, CONV_ROWS), :]
        return carry

    lax.fori_loop(0, t // CONV_ROWS, step, 0)


def _conv_c_prompt(z, w, bsz, t, d_c, dst):
    cb = 512
    nb = d_c // cb
    return pl.pallas_call(
        _conv_c_prompt_body,
        grid=(bsz, nb),
        in_specs=[pl.BlockSpec((t, cb), lambda b, j: (b, j)),
                  pl.BlockSpec((t, cb), lambda b, j: (b, nb + j)),
                  pl.BlockSpec((t, cb), lambda b, j: (b, 2 * nb + j)),
                  pl.BlockSpec((CONV_C_W, cb), lambda b, j: (0, j)),
                  pl.BlockSpec(memory_space=pl.ANY)],
        out_specs=[pl.BlockSpec((t, cb), lambda b, j: (b, j)),
                   pl.BlockSpec((1, 1, CONV_C_W - 1, cb), lambda b, j: (0, b, 0, j))],
        out_shape=[jax.ShapeDtypeStruct(dst.shape, F32),
                   jax.ShapeDtypeStruct((1, bsz, CONV_C_W - 1, d_c), F32)],
        input_output_aliases={4: 0},
        scratch_shapes=[pltpu.VMEM((t + 2 * SUBLANES, cb), F32)],
        compiler_params=_cparams(("parallel", "parallel")),
        name="conv_c_prompt",
    )(z, z, z, w, dst)


def _conv_c_sample_body(n_valid, buf_ref, gb_ref, gc_ref, hv_ref, w_ref, o_ref, nbuf_ref):
    d_c = gb_ref.shape[1]
    blk = gb_ref.shape[0]
    _zero_tail_rows(o_ref, n_valid)
    u = gc_ref[...] * hv_ref[...]
    c = buf_ref[:, :d_c] * w_ref[0:1, :] + buf_ref[:, d_c:] * w_ref[1:2, :] + u * w_ref[2:3, :]
    o_ref[pl.ds(pl.multiple_of(pl.program_id(0) * blk, blk), blk), :] = gb_ref[...] * c
    nbuf_ref[:, :d_c] = buf_ref[:, d_c:]
    nbuf_ref[:, d_c:] = u


def _conv_c_sample(buf, z, w, row0, d_c):
    ns = buf.shape[0]
    blk = 2 * SAMPLE_BLOCK
    r0 = row0 // blk
    return pl.pallas_call(
        functools.partial(_conv_c_sample_body, ns),
        grid=(ns // blk,),
        in_specs=[pl.BlockSpec((blk, buf.shape[1]), lambda i: (i, 0)),
                  pl.BlockSpec((blk, d_c), lambda i: (r0 + i, 0)),
                  pl.BlockSpec((blk, d_c), lambda i: (r0 + i, 1)),
                  pl.BlockSpec((blk, d_c), lambda i: (r0 + i, 2)),
                  pl.BlockSpec((CONV_C_W, d_c), lambda i: (0, 0))],
        out_specs=[_tail_spec(z.shape[0], row0, d_c),
                   pl.BlockSpec((blk, buf.shape[1]), lambda i: (i, 0))],
        out_shape=[jax.ShapeDtypeStruct((z.shape[0], d_c), F32), jax.ShapeDtypeStruct(buf.shape, F32)],
        compiler_params=_cparams(("arbitrary",)),
        name="conv_c_sample",
    )(buf, z, z, z, w)


def _candidate_tables():
    pairs = [(a, b) for a in range(TOPK) for b in range(TOPK) if (a + 1) * (b + 1) <= TOPK]
    rows = -(-len(pairs) // SUBLANES) * SUBLANES
    p1 = np.zeros((rows, TOPK), np.float32)
    p2 = np.zeros((rows, TOPK), np.float32)
    neg = np.zeros((rows, LANES), np.float32)
    for r, (a, b) in enumerate(pairs):
        p1[r, a] = 1.0
        p2[r, b] = 1.0
    neg[len(pairs):] = -np.inf
    return p1, p2, neg, p1.T.copy()


def _take_top(works, index, count, tie_break):
    rows, lanes = works[0].shape
    slot = lax.broadcasted_iota(jnp.int32, (count, lanes), 0)
    works = list(works)
    vals = [jnp.zeros((count, lanes), F32) for _ in works]
    ranks = [jnp.full((rows, lanes), float(count), F32) for _ in works]
    for a in range(count):
        for i, work in enumerate(works):
            m = jnp.max(work, axis=0, keepdims=True)
            sel = work == m
            if tie_break:
                first = jnp.min(jnp.where(sel, index, float(rows)), axis=0, keepdims=True)
                sel = index == first
            ranks[i] = jnp.where(sel, float(a), ranks[i])
            works[i] = jnp.where(sel, -jnp.inf, work)
            vals[i] = jnp.where(slot == a, m, vals[i])
    return vals, ranks


def _count(mask):
    return jnp.sum(mask.astype(F32), axis=0, keepdims=True)


def _route_heads(s1, s2, p1, p2, neg, p1t, key_idx, cand_idx, tie_break):
    n = len(s1)
    vals, ranks = _take_top(list(s1) + list(s2), key_idx, TOPK, tie_break)
    v1, v2, rank1, rank2 = vals[:n], vals[n:], ranks[:n], ranks[n:]
    cand = [_dot_f32(p1, v1[i], ((1,), (0,))) + _dot_f32(p2, v2[i], ((1,), (0,))) + neg for i in range(n)]
    _, crank = _take_top(cand, cand_idx, TOPK, tie_break)
    out = []
    for i in range(n):
        chosen = crank[i] < float(TOPK)
        cmax = v1[i][0:1, :] + v2[i][0:1, :]
        z = jnp.sum(jnp.where(chosen, jnp.exp(cand[i] - cmax), 0.0), axis=0, keepdims=True)
        width = jnp.dot(p1t, chosen.astype(F32), preferred_element_type=F32)
        b1 = jnp.zeros_like(rank1[i])
        for a in range(TOPK):
            b1 = jnp.where(rank1[i] == float(a), width[a:a + 1, :], b1)
        in1 = rank1[i] < float(TOPK)
        in2 = rank2[i] < float(TOPK)
        c1 = jnp.where(in1, jnp.exp(jnp.where(in1, s1[i] - v1[i][0:1, :], 0.0)) / z, 0.0)
        e2 = jnp.where(in2, jnp.exp(jnp.where(in2, s2[i] - v2[i][0:1, :], 0.0)), 0.0)
        clean = ((_count(in1) == float(TOPK)) & (_count(in2) == float(TOPK))
                 & (_count(chosen) == float(TOPK)))
        out.append((c1, b1, rank2[i], e2, clean))
    return out


ROUTE_HEADS = 4


def _route_body(q_ref, keys_ref, p1_ref, p2_ref, neg_ref, p1t_ref, u_ref, v_ref,
                r2_ref, e2_ref, c1_ref, b1_ref, ub_ref, vt_ref, q3_ref):
    ub_ref[...] = u_ref[0].astype(BF16)
    vt_ref[...] = v_ref[0].T.astype(BF16)

    for hp in range(2 * P_HEADS):
        q3_ref[hp] = q_ref[:, hp * LANES:(hp + 1) * LANES]
    key_idx = lax.broadcasted_iota(jnp.int32, (N_KEYS, LANES), 0).astype(F32)
    cand_idx = lax.broadcasted_iota(jnp.int32, (p1_ref.shape[0], LANES), 0).astype(F32)

    def emit(heads, tie_break):
        s1 = [_dot_f32(keys_ref[h, 0], q3_ref[2 * h], ((1,), (1,))) for h in heads]
        s2 = [_dot_f32(keys_ref[h, 1], q3_ref[2 * h + 1], ((1,), (1,))) for h in heads]
        rows = _route_heads(s1, s2, p1_ref[...], p2_ref[...], neg_ref[...], p1t_ref[...],
                            key_idx, cand_idx, tie_break)
        all_clean = None
        for h, (c1, b1, r2, e2, clean) in zip(heads, rows):
            c1_ref[h] = c1
            b1_ref[h] = b1
            r2_ref[h] = r2.astype(r2_ref.dtype)
            e2_ref[h] = e2.astype(e2_ref.dtype)
            all_clean = clean if all_clean is None else all_clean & clean
        return all_clean

    def step(i, carry):
        heads = [i * ROUTE_HEADS + j for j in range(ROUTE_HEADS)]
        all_clean = emit(heads, False)

        @pl.when(jnp.min(all_clean.astype(F32)) < 0.5)
        def _():
            emit(heads, True)

        return carry

    lax.fori_loop(0, P_HEADS // ROUTE_HEADS, step, 0)


def _route(q, keys, table_u, table_v, layer):
    n = q.shape[0]
    _, n_exp, d = table_u.shape
    steps = n // LANES
    n_slabs = min(n_exp // TABLE_SLAB, 1 << (steps.bit_length() - 1))
    slab = n_exp // n_slabs
    assert n_slabs * slab == n_exp
    p1, p2, neg, p1t = _candidate_tables()
    whole = lambda a: pl.BlockSpec(a.shape, lambda i: (0,) * a.ndim)
    out_spec = pl.BlockSpec((P_HEADS, N_KEYS, LANES), lambda i: (0, 0, i))
    shape = lambda dt: jax.ShapeDtypeStruct((P_HEADS, N_KEYS, n), dt)
    slab_of = lambda i: jnp.minimum(i, n_slabs - 1)
    table_spec = pl.BlockSpec((1, slab, d), lambda i: (layer, slab_of(i), 0))
    return pl.pallas_call(
        _route_body,
        grid=(steps,),
        in_specs=[pl.BlockSpec((LANES, q.shape[1]), lambda i: (i, 0)), whole(keys),
                  whole(p1), whole(p2), whole(neg), whole(p1t), table_spec, table_spec],
        out_specs=[out_spec] * 4 + [pl.BlockSpec((slab, d), lambda i: (slab_of(i), 0)),
                                    pl.BlockSpec((d, slab), lambda i: (0, slab_of(i)))],
        out_shape=[shape(BF16), shape(BF16), shape(F32), shape(F32),
                   jax.ShapeDtypeStruct((n_exp, d), BF16), jax.ShapeDtypeStruct((d, n_exp), BF16)],
        scratch_shapes=[pltpu.VMEM((2 * P_HEADS, LANES, LANES), F32)],
        compiler_params=_cparams(("arbitrary",)),
        name="peer_route",
    )(q, keys, jnp.asarray(p1), jnp.asarray(p2), jnp.asarray(neg), jnp.asarray(p1t), table_u, table_v)


def _bf16_pair(c):
    hi = float(np.asarray(c, dtype=BF16))
    return hi, float(np.asarray(c - hi, dtype=BF16))


def _gelu_tanh(x):
    c0 = float(np.sqrt(2.0 / np.pi))
    c0_hi, c0_lo = _bf16_pair(c0)
    c1_hi, c1_lo = _bf16_pair(c0 * 0.044715)
    x2 = x * x
    inner = x * ((c1_hi * x2 + c0_hi) + (c1_lo * x2 + c0_lo))
    half = 0.5 * x
    return half + half * jnp.tanh(inner)


def _peer_dense_body(split, xt_ref, u_ref, vt_ref, r2_ref, e2_ref, c1_ref, b1_ref, x_ref, gf_ref, *rest):
    if split is None:
        o_ref, acc_ref = rest
    else:
        o_ref, ys_ref, acc_ref = rest
    e = pl.program_id(1)
    n_groups = xt_ref.shape[1] // PEER_LANES

    @pl.when(e == 0)
    def _():
        acc_ref[...] = jnp.zeros_like(acc_ref)

    def scores(j):
        cols = slice(j * PEER_LANES, (j + 1) * PEER_LANES)
        return jnp.dot(u_ref[...], xt_ref[:, cols], preferred_element_type=F32)

    def gates(j):
        cols = slice(j * PEER_LANES, (j + 1) * PEER_LANES)
        parts = []
        for r in range(EXPERT_BLOCK // N_KEYS):
            gate = jnp.zeros((N_KEYS, PEER_LANES), BF16)
            for h in range(P_HEADS):
                hit = r2_ref[h, :, cols] < b1_ref[h, r:r + 1, cols].astype(BF16)
                gate = gate + c1_ref[h, r:r + 1, cols].astype(BF16) * jnp.where(
                    hit, e2_ref[h, :, cols], jnp.zeros((), BF16))
            parts.append(gate)
        return jnp.concatenate(parts, axis=0)

    s_next = scores(0)
    for j in range(n_groups):
        s = s_next
        if j + 1 < n_groups:
            s_next = scores(j + 1)
        cols = slice(j * PEER_LANES, (j + 1) * PEER_LANES)
        weighted = gates(j) * _gelu_tanh(s.astype(BF16))
        acc_ref[:, cols] += jnp.dot(vt_ref[...], weighted, preferred_element_type=F32)

    @pl.when(e == pl.num_programs(1) - 1)
    def _():
        for j in range(n_groups):
            rows = slice(j * PEER_LANES, (j + 1) * PEER_LANES)
            out = x_ref[rows, :] + acc_ref[:, rows].T
            if split is None:
                o_ref[rows, :] = out
                continue
            y = _rmsnorm(out, gf_ref[...])
            o_ref[rows, :] = y
            if j == split.group:
                @pl.when(pl.program_id(0) == split.block)
                def _():
                    ys_ref[...] = y[:ys_ref.shape[0], :]


class _SampleSplit(NamedTuple):
    block: int
    group: int


def _peer_dense(xt, u_b, vt_b, r2, e2, c1, b1, x, g_final, n_prompt=None, n_sample=None):
    d, n = xt.shape
    n_exp = u_b.shape[0]
    keys_per_block = EXPERT_BLOCK // N_KEYS
    once = pl.Buffered(1)
    rank_spec = pl.BlockSpec((P_HEADS, N_KEYS, TOK_BLOCK), lambda i, e: (0, 0, i), pipeline_mode=once)
    first_spec = pl.BlockSpec((P_HEADS, keys_per_block, TOK_BLOCK), lambda i, e: (0, e, i))
    row_spec = pl.BlockSpec((TOK_BLOCK, d), lambda i, e: (i, 0), pipeline_mode=once)
    if n_prompt is None:
        split = None
        out_specs = row_spec
        out_shape = jax.ShapeDtypeStruct((n, d), F32)
    else:
        start = n_prompt % TOK_BLOCK
        assert start % PEER_LANES == 0 and n_sample <= PEER_LANES and n_prompt + n_sample <= n
        split = _SampleSplit(n_prompt // TOK_BLOCK, start // PEER_LANES)
        out_specs = [row_spec, pl.BlockSpec((n_sample, d), lambda i, e: (0, 0))]
        out_shape = [jax.ShapeDtypeStruct((n_prompt, d), F32), jax.ShapeDtypeStruct((n_sample, d), F32)]
    return pl.pallas_call(
        functools.partial(_peer_dense_body, split),
        grid=(n // TOK_BLOCK, n_exp // EXPERT_BLOCK),
        in_specs=[pl.BlockSpec((d, TOK_BLOCK), lambda i, e: (0, i), pipeline_mode=once),
                  pl.BlockSpec((EXPERT_BLOCK, d), lambda i, e: (e, 0)),
                  pl.BlockSpec((d, EXPERT_BLOCK), lambda i, e: (0, e)),
                  rank_spec, rank_spec, first_spec, first_spec, row_spec,
                  pl.BlockSpec((1, d), lambda i, e: (0, 0))],
        out_specs=out_specs,
        out_shape=out_shape,
        scratch_shapes=[pltpu.VMEM((d, TOK_BLOCK), F32)],
        compiler_params=_cparams(("arbitrary", "arbitrary")),
        name="peer_dense" if split is None else "peer_dense_final",
    )(xt, u_b, vt_b, r2, e2, c1, b1, x, g_final)


def _peer(x, g_norm, wq, keys, table_u, table_v, layer, g_final, n_prompt=None, n_sample=None):
    q, xt = _norm_matmul_t(x, g_norm, wq, wq.shape[1])
    r2, e2, c1, b1, u_b, vt_b = _route(q, keys, table_u, table_v, layer)
    return _peer_dense(xt, u_b, vt_b, r2, e2, c1, b1, x, g_final, n_prompt, n_sample)


def _col_form(a, n_dk):
    ns = a.shape[0]
    a = a.reshape(ns // SAMPLE_BLOCK, SAMPLE_BLOCK, H_A, n_dk)
    return a.transpose(0, 2, 3, 1)


def kernel(x_prompt, x_sample, state_gla, state_conv_b, state_conv_c, norm_mix, norm_ffn, norm_final, w_in_even, w_a2, b_a2, gla_norm_g, conv_b_w, conv_b_bias, conv_b_ln_g, conv_b_ln_b, w_out_even, w_in_odd, conv_c_w, w_out_odd, peer_wq, peer_keys, peer_u, peer_v):
    bsz, t, d = x_prompt.shape
    ns = x_sample.shape[0]
    n_prompt = bsz * t
    n_real = n_prompt + ns
    n_tok = -(-n_real // TOK_BLOCK) * TOK_BLOCK
    assert t % GLA_CHUNK == 0 and t % CONV_ROWS == 0 and ns % (2 * SAMPLE_BLOCK) == 0
    assert x_sample.shape[1] == 1 and n_prompt % (2 * SAMPLE_BLOCK) == 0

    d_a = d // 2
    dk_a = d_a // 2
    d_b = d - d_a
    rank = w_a2.shape[1]
    col_a = 2 * dk_a + 2 * d_a
    row = lambda a: a.reshape(1, -1)

    x = jnp.concatenate([x_prompt.reshape(n_prompt, d), x_sample.reshape(ns, d),
                         jnp.zeros((n_tok - n_real, d), F32)], axis=0)

    w_in = w_in_even[0]
    w_main = jnp.concatenate([w_in[:, :col_a], w_in[:, col_a + rank:]], axis=1).astype(BF16)
    w_gate = jnp.pad(w_in[:, col_a:col_a + rank], ((0, 0), (0, LANES - rank))).astype(BF16)
    w_a2p = jnp.pad(w_a2[0], ((0, LANES - rank), (0, 0)))
    z, la = _norm_matmul_gate(x, row(norm_mix[0]), w_main, w_gate, w_a2p, row(b_a2[0]), 1024)

    gn = row(gla_norm_g[0])
    zs = z[n_prompt:n_real]
    las = la[n_prompt:n_real]
    oa, gla_s = _gla_sample(state_gla, _col_form(las, dk_a // H_A), _col_form(zs[:, dk_a:2 * dk_a], dk_a // H_A),
                            _col_form(zs[:, :dk_a], dk_a // H_A),
                            zs[:, 2 * dk_a:2 * dk_a + d_a], zs[:, 2 * dk_a + d_a:col_a], gn, n_tok, n_prompt)
    oa, gla_p = _gla_prompt(z, la, gn, bsz, t, oa)

    ub, cb_s = _conv_b_sample(state_conv_b[0].reshape(ns, -1), z, conv_b_w[0], row(conv_b_bias[0]),
                              n_prompt, col_a, d_b)
    ub, cb_p = _conv_b_prompt(z, conv_b_w[0], row(conv_b_bias[0]), bsz, t, col_a, d_b, ub)
    x = _res_matmul_ln(oa, ub, row(conv_b_ln_g[0]), row(conv_b_ln_b[0]), w_out_even[0].astype(BF16), x, d)
    wq_b = peer_wq.astype(BF16)
    x = _peer(x, row(norm_ffn[0]), wq_b[0], peer_keys[0], peer_u, peer_v, 0, row(norm_final))

    z = _norm_matmul(x, row(norm_mix[1]), w_in_odd[0].astype(BF16), 1024)
    gc, cc_s = _conv_c_sample(state_conv_c[0].reshape(ns, -1), z, conv_c_w[0], n_prompt, d)
    gc, cc_p = _conv_c_prompt(z, conv_c_w[0], bsz, t, d, gc)
    x = _res_matmul(gc, w_out_odd[0].astype(BF16), x, d)
    y_prompt, y_sample = _peer(x, row(norm_ffn[1]), wq_b[1], peer_keys[1], peer_u, peer_v, 1, row(norm_final),
                               n_prompt, ns)
    return (y_prompt.reshape(bsz, t, d), y_sample.reshape(ns, 1, d), gla_p, cb_p, cc_p,
            gla_s, cb_s.reshape(state_conv_b.shape), cc_s.reshape(state_conv_c.shape))
```

```python
import functools
from typing import NamedTuple

import numpy as np
import jax
import jax.numpy as jnp
from jax import lax
from jax.experimental import pallas as pl
from jax.experimental.pallas import tpu as pltpu

F32 = jnp.float32
BF16 = jnp.bfloat16
HIGHEST = lax.Precision.HIGHEST

EPS = 1e-6
GATE_TAU = 16.0
H_A = 4
P_HEADS = 8
N_KEYS = 128
TOPK = 16
CONV_B_W = 31
CONV_C_W = 3

LANES = 128
SUBLANES = 8
VMEM_LIMIT = 56 * 1024 * 1024

TOK_BLOCK = 768
EXPERT_BLOCK = 1024
PEER_LANES = 256
TABLE_SLAB = 256
GLA_CHUNK = 128
GLA_SUB = 16
EXP_CLAMP = 80.0
CONV_ROWS = 64
SAMPLE_BLOCK = 8


def _cparams(sem):
    return pltpu.CompilerParams(dimension_semantics=sem, vmem_limit_bytes=VMEM_LIMIT)


def _rmsnorm(x, g):
    return x * lax.rsqrt(jnp.mean(x * x, axis=-1, keepdims=True) + EPS) * g


def _sigmoid(x):
    return jax.nn.sigmoid(x)


def _weight_spec(k, m, tn):
    if tn == m:
        return pl.BlockSpec((k, tn), lambda i, j: (0, 0), pipeline_mode=pl.Buffered(1))
    return pl.BlockSpec((k, tn), lambda i, j: (0, j))


def _log_sigmoid(x):
    return jnp.minimum(x, 0.0) - jnp.log(1.0 + jnp.exp(-jnp.abs(x)))


def _norm_matmul_body(x_ref, g_ref, w_ref, o_ref, xn_ref):
    @pl.when(pl.program_id(1) == 0)
    def _():
        xn_ref[...] = _rmsnorm(x_ref[...], g_ref[...]).astype(BF16)

    o_ref[...] = jnp.dot(xn_ref[...], w_ref[...], preferred_element_type=F32).astype(o_ref.dtype)


def _norm_matmul(x, g, w, tn):
    n, d = x.shape
    m = w.shape[1]
    return pl.pallas_call(
        _norm_matmul_body,
        grid=(n // TOK_BLOCK, m // tn),
        in_specs=[pl.BlockSpec((TOK_BLOCK, d), lambda i, j: (i, 0)),
                  pl.BlockSpec((1, d), lambda i, j: (0, 0)),
                  pl.BlockSpec((d, tn), lambda i, j: (0, j))],
        out_specs=pl.BlockSpec((TOK_BLOCK, tn), lambda i, j: (i, j)),
        out_shape=jax.ShapeDtypeStruct((n, m), BF16),
        scratch_shapes=[pltpu.VMEM((TOK_BLOCK, d), BF16)],
        compiler_params=_cparams(("parallel", "arbitrary")),
        name="norm_matmul",
    )(x, g, w)


def _norm_matmul_gate_body(x_ref, g_ref, w_ref, wa_ref, wa2_ref, ba2_ref, o_ref, la_ref, xn_ref):
    @pl.when(pl.program_id(1) == 0)
    def _():
        xn = _rmsnorm(x_ref[...], g_ref[...]).astype(BF16)
        xn_ref[...] = xn
        a_lr = jnp.dot(xn, wa_ref[...], preferred_element_type=F32)
        pre = jnp.dot(a_lr, wa2_ref[...], precision=HIGHEST, preferred_element_type=F32) + ba2_ref[...]
        la_ref[...] = _log_sigmoid(pre) / GATE_TAU

    o_ref[...] = jnp.dot(xn_ref[...], w_ref[...], preferred_element_type=F32).astype(o_ref.dtype)


def _norm_matmul_gate(x, g, w, wa, wa2, ba2, tn):
    n, d = x.shape
    m = w.shape[1]
    dk = wa2.shape[1]
    return pl.pallas_call(
        _norm_matmul_gate_body,
        grid=(n // TOK_BLOCK, m // tn),
        in_specs=[pl.BlockSpec((TOK_BLOCK, d), lambda i, j: (i, 0)),
                  pl.BlockSpec((1, d), lambda i, j: (0, 0)),
                  pl.BlockSpec((d, tn), lambda i, j: (0, j)),
                  pl.BlockSpec((d, LANES), lambda i, j: (0, 0)),
                  pl.BlockSpec((LANES, dk), lambda i, j: (0, 0)),
                  pl.BlockSpec((1, dk), lambda i, j: (0, 0))],
        out_specs=[pl.BlockSpec((TOK_BLOCK, tn), lambda i, j: (i, j)),
                   pl.BlockSpec((TOK_BLOCK, dk), lambda i, j: (i, 0))],
        out_shape=[jax.ShapeDtypeStruct((n, m), BF16), jax.ShapeDtypeStruct((n, dk), F32)],
        scratch_shapes=[pltpu.VMEM((TOK_BLOCK, d), BF16)],
        compiler_params=_cparams(("parallel", "arbitrary")),
        name="norm_matmul_gate",
    )(x, g, w, wa, wa2, ba2)


def _norm_matmul_t_body(x_ref, g_ref, w_ref, o_ref, xt_ref, xn_ref):
    @pl.when(pl.program_id(1) == 0)
    def _():
        xn = _rmsnorm(x_ref[...], g_ref[...])
        xn_ref[...] = xn.astype(BF16)
        xt_ref[...] = xn.T.astype(BF16)

    o_ref[...] = jnp.dot(xn_ref[...], w_ref[...], preferred_element_type=F32)


def _norm_matmul_t(x, g, w, tn):
    n, d = x.shape
    m = w.shape[1]
    return pl.pallas_call(
        _norm_matmul_t_body,
        grid=(n // TOK_BLOCK, m // tn),
        in_specs=[pl.BlockSpec((TOK_BLOCK, d), lambda i, j: (i, 0)),
                  pl.BlockSpec((1, d), lambda i, j: (0, 0)),
                  _weight_spec(d, m, tn)],
        out_specs=[pl.BlockSpec((TOK_BLOCK, tn), lambda i, j: (i, j)),
                   pl.BlockSpec((d, TOK_BLOCK), lambda i, j: (0, i))],
        out_shape=[jax.ShapeDtypeStruct((n, m), F32), jax.ShapeDtypeStruct((d, n), BF16)],
        scratch_shapes=[pltpu.VMEM((TOK_BLOCK, d), BF16)],
        compiler_params=_cparams(("parallel", "arbitrary")),
        name="norm_matmul_t",
    )(x, g, w)


def _res_matmul_body(a_ref, w_ref, x_ref, o_ref, ab_ref):
    @pl.when(pl.program_id(1) == 0)
    def _():
        ab_ref[...] = a_ref[...].astype(BF16)

    o_ref[...] = x_ref[...] + jnp.dot(ab_ref[...], w_ref[...], preferred_element_type=F32)


def _res_matmul(a, w, x, tn):
    n, k = a.shape
    m = w.shape[1]
    return pl.pallas_call(
        _res_matmul_body,
        grid=(n // TOK_BLOCK, m // tn),
        in_specs=[pl.BlockSpec((TOK_BLOCK, k), lambda i, j: (i, 0)),
                  _weight_spec(k, m, tn),
                  pl.BlockSpec((TOK_BLOCK, tn), lambda i, j: (i, j))],
        out_specs=pl.BlockSpec((TOK_BLOCK, tn), lambda i, j: (i, j)),
        out_shape=jax.ShapeDtypeStruct((n, m), F32),
        scratch_shapes=[pltpu.VMEM((TOK_BLOCK, k), BF16)],
        compiler_params=_cparams(("parallel", "arbitrary")),
        name="res_matmul",
    )(a, w, x)


def _res_matmul_ln_body(oa_ref, ub_ref, lg_ref, lb_ref, w_ref, x_ref, o_ref, ab_ref):
    da = oa_ref.shape[1]

    @pl.when(pl.program_id(1) == 0)
    def _():
        u = ub_ref[...]
        uc = u - jnp.mean(u, axis=-1, keepdims=True)
        y = uc * lax.rsqrt(jnp.mean(uc * uc, axis=-1, keepdims=True) + EPS) * lg_ref[...] + lb_ref[...]
        ab_ref[:, :da] = oa_ref[...].astype(BF16)
        ab_ref[:, da:] = (y * _sigmoid(y)).astype(BF16)

    o_ref[...] = x_ref[...] + jnp.dot(ab_ref[...], w_ref[...], preferred_element_type=F32)


def _res_matmul_ln(oa, ub, lg, lb, w, x, tn):
    n, da = oa.shape
    db = ub.shape[1]
    m = w.shape[1]
    return pl.pallas_call(
        _res_matmul_ln_body,
        grid=(n // TOK_BLOCK, m // tn),
        in_specs=[pl.BlockSpec((TOK_BLOCK, da), lambda i, j: (i, 0)),
                  pl.BlockSpec((TOK_BLOCK, db), lambda i, j: (i, 0)),
                  pl.BlockSpec((1, db), lambda i, j: (0, 0)),
                  pl.BlockSpec((1, db), lambda i, j: (0, 0)),
                  _weight_spec(da + db, m, tn),
                  pl.BlockSpec((TOK_BLOCK, tn), lambda i, j: (i, j))],
        out_specs=pl.BlockSpec((TOK_BLOCK, tn), lambda i, j: (i, j)),
        out_shape=jax.ShapeDtypeStruct((n, m), F32),
        scratch_shapes=[pltpu.VMEM((TOK_BLOCK, da + db), BF16)],
        compiler_params=_cparams(("parallel", "arbitrary")),
        name="res_matmul_ln",
    )(oa, ub, lg, lb, w, x)


def _dot_f32(a, b, dims):
    return lax.dot_general(a, b, (dims, ((), ())), precision=HIGHEST, preferred_element_type=F32)


def _dot_bf16(a, b, dims):
    return lax.dot_general(a.astype(BF16), b.astype(BF16), (dims, ((), ())), preferred_element_type=F32)


def _head_out(o, gn, g):
    y = o * lax.rsqrt(jnp.mean(o * o, axis=-1, keepdims=True) + EPS) * gn
    return y * (g * _sigmoid(g))


def _gla_prompt_body(q_ref, k_ref, v_ref, g_ref, la_ref, gn_ref, dst_ref, o_ref, s_out_ref, s_ref, acc_ref):
    del dst_ref
    c = pl.program_id(1)
    dk = GLA_CHUNK
    dv = s_ref.shape[2]

    @pl.when(c == 0)
    def _():
        s_ref[...] = jnp.zeros_like(s_ref)

    row = lax.broadcasted_iota(jnp.int32, (GLA_CHUNK, GLA_CHUNK), 0)
    col = lax.broadcasted_iota(jnp.int32, (GLA_CHUNK, GLA_CHUNK), 1)
    tri = (row >= col).astype(F32)
    sub_row = lax.broadcasted_iota(jnp.int32, (GLA_SUB, GLA_SUB), 0)
    sub_col = lax.broadcasted_iota(jnp.int32, (GLA_SUB, GLA_SUB), 1)
    causal = sub_row >= sub_col

    for h in range(H_A):
        kcols = slice(h * dk, (h + 1) * dk)
        vcols = slice(h * dv, (h + 1) * dv)
        la = la_ref[:, kcols]
        b = _dot_f32(tri, la, ((1,), (0,)))
        b_last = b[GLA_CHUNK - 1:GLA_CHUNK, :]
        q = q_ref[:, kcols].astype(F32) * (dk ** -0.5)
        k = k_ref[:, kcols].astype(F32)
        v_b = v_ref[:, vcols].astype(BF16)
        s = s_ref[h]

        acc_ref[h] = _dot_bf16(q * jnp.exp(b), s, ((1,), (0,)))
        for j in range(GLA_CHUNK // GLA_SUB):
            lo, hi = j * GLA_SUB, (j + 1) * GLA_SUB
            b_j = b[lo:hi, :]
            b_first = b[lo:lo + 1, :]
            b_end = b[hi - 1:hi, :]
            q_d = q[lo:hi, :] * jnp.exp(b_j - b_first)
            k_d = k[lo:hi, :] * jnp.exp(jnp.minimum(b_first - b_j, EXP_CLAMP))
            sc = jnp.where(causal, _dot_bf16(q_d, k_d, ((1,), (1,))), 0.0)
            acc_ref[h, lo:hi, :] += _dot_bf16(sc, v_b[lo:hi, :], ((1,), (0,)))
            if hi < GLA_CHUNK:
                k_p = k[lo:hi, :] * jnp.exp(b_end - b_j)
                q_p = q[hi:, :] * jnp.exp(b[hi:, :] - b_end)
                sc = _dot_bf16(q_p, k_p, ((1,), (1,)))
                acc_ref[h, hi:, :] += _dot_bf16(sc, v_b[lo:hi, :], ((1,), (0,)))

        decay = jnp.exp(jnp.broadcast_to(b_last, (dk, dk))).T
        k_e = k * jnp.exp(b_last - b)
        s_new = jnp.concatenate([decay] * (dv // dk), axis=1) * s + _dot_bf16(k_e.T, v_b, ((1,), (0,)))
        s_ref[h] = s_new
        o_ref[:, vcols] = _head_out(acc_ref[h], gn_ref[:, vcols], g_ref[:, vcols].astype(F32))

    @pl.when(c == pl.num_programs(1) - 1)
    def _():
        s_out_ref[0, 0] = s_ref[...]


def _gla_prompt(z, la, gn, bsz, t, dst):
    dk, dv = GLA_CHUNK, 2 * GLA_CHUNK
    d_k, d_v = H_A * dk, H_A * dv
    nc = t // GLA_CHUNK
    rows = lambda b, c: b * nc + c
    return pl.pallas_call(
        _gla_prompt_body,
        grid=(bsz, nc),
        in_specs=[pl.BlockSpec((GLA_CHUNK, d_k), lambda b, c: (rows(b, c), 0)),
                  pl.BlockSpec((GLA_CHUNK, d_k), lambda b, c: (rows(b, c), 1)),
                  pl.BlockSpec((GLA_CHUNK, d_v), lambda b, c: (rows(b, c), 2 * d_k // d_v)),
                  pl.BlockSpec((GLA_CHUNK, d_v), lambda b, c: (rows(b, c), 2 * d_k // d_v + 1)),
                  pl.BlockSpec((GLA_CHUNK, d_k), lambda b, c: (rows(b, c), 0)),
                  pl.BlockSpec((1, d_v), lambda b, c: (0, 0)),
                  pl.BlockSpec(memory_space=pl.ANY)],
        out_specs=[pl.BlockSpec((GLA_CHUNK, d_v), lambda b, c: (rows(b, c), 0)),
                   pl.BlockSpec((1, 1, H_A, dk, dv), lambda b, c: (0, b, 0, 0, 0))],
        out_shape=[jax.ShapeDtypeStruct(dst.shape, F32),
                   jax.ShapeDtypeStruct((1, bsz, H_A, dk, dv), F32)],
        input_output_aliases={6: 0},
        scratch_shapes=[pltpu.VMEM((H_A, dk, dv), F32), pltpu.VMEM((H_A, GLA_CHUNK, dv), F32)],
        compiler_params=_cparams(("parallel", "arbitrary")),
        name="gla_prompt",
    )(z, z, z, z, la, gn, dst)


def _tail_spec(n_rows, row0, cols):
    tail = n_rows - row0
    assert row0 % tail == 0 and tail % SUBLANES == 0
    return pl.BlockSpec((tail, cols), lambda i: (row0 // tail, 0))


def _zero_tail_rows(o_ref, n_valid):
    @pl.when(pl.program_id(0) == 0)
    def _():
        o_ref[n_valid:, :] = jnp.zeros((o_ref.shape[0] - n_valid, o_ref.shape[1]), F32)


def _gla_sample_body(n_valid, st_ref, lat_ref, kt_ref, qt_ref, v_ref, g_ref, gn_ref, dst_ref, o_ref, s_out_ref,
                     rows_ref):
    del dst_ref
    dv = st_ref.shape[-1]
    _zero_tail_rows(o_ref, n_valid)
    for i in range(SAMPLE_BLOCK):
        for h in range(H_A):
            decay = jnp.exp(lat_ref[0, h, :, i:i + 1])
            v_row = v_ref[i:i + 1, h * dv:(h + 1) * dv]
            s_new = decay * st_ref[0, i, h] + kt_ref[0, h, :, i:i + 1] * v_row
            s_out_ref[0, i, h] = s_new
            q_col = qt_ref[0, h, :, i:i + 1] * (st_ref.shape[-2] ** -0.5)
            o = jnp.sum(q_col * s_new, axis=0, keepdims=True)
            rows_ref[i:i + 1, h * dv:(h + 1) * dv] = _head_out(
                o, gn_ref[:, h * dv:(h + 1) * dv], g_ref[i:i + 1, h * dv:(h + 1) * dv])
    base = pl.multiple_of(pl.program_id(0) * SAMPLE_BLOCK, SAMPLE_BLOCK)
    o_ref[pl.ds(base, SAMPLE_BLOCK), :] = rows_ref[...]


def _gla_sample(state, lat, kt, qt, v, g, gn, dst, row0):
    _, ns, _, dk, dv = state.shape
    n_rows = dst.shape[0]
    steps = ns // SAMPLE_BLOCK
    col_spec = pl.BlockSpec((1, H_A, dk, SAMPLE_BLOCK), lambda i: (i, 0, 0, 0))
    row_spec = pl.BlockSpec((SAMPLE_BLOCK, H_A * dv), lambda i: (i, 0))
    st_spec = pl.BlockSpec((1, SAMPLE_BLOCK, H_A, dk, dv), lambda i: (0, i, 0, 0, 0))
    return pl.pallas_call(
        functools.partial(_gla_sample_body, ns),
        grid=(steps,),
        in_specs=[st_spec, col_spec, col_spec, col_spec, row_spec, row_spec,
                  pl.BlockSpec((1, H_A * dv), lambda i: (0, 0)),
                  pl.BlockSpec(memory_space=pl.ANY)],
        out_specs=[_tail_spec(n_rows, row0, H_A * dv), st_spec],
        out_shape=[jax.ShapeDtypeStruct(dst.shape, F32), jax.ShapeDtypeStruct(state.shape, F32)],
        input_output_aliases={7: 0},
        scratch_shapes=[pltpu.VMEM((SAMPLE_BLOCK, H_A * dv), F32)],
        compiler_params=_cparams(("arbitrary",)),
        name="gla_sample",
    )(state, lat, kt, qt, v, g, gn, dst)


def _fill_padded(pad_ref, hist, t, value):
    cols = pad_ref.shape[1]
    pad_ref[0:hist, :] = jnp.zeros((hist, cols), F32)
    pad_ref[hist:hist + t, :] = value
    pad_ref[hist + t:, :] = jnp.zeros((SUBLANES, cols), F32)


def _causal_taps(pad_ref, w_ref, base, hist, n_taps, acc):
    first = hist - (n_taps - 1)
    win = pad_ref[pl.ds(base, CONV_ROWS + hist + SUBLANES), :]
    for r in range(SUBLANES):
        taps = [w for w in range(n_taps) if (first + w) % SUBLANES == r]
        if not taps:
            continue
        shifted = win[r:r + CONV_ROWS + hist, :]
        for w in taps:
            lo = (first + w) // SUBLANES * SUBLANES
            acc = acc + shifted[lo:lo + CONV_ROWS, :] * w_ref[w:w + 1, :]
    return acc


def _conv_b_prompt_body(a_ref, b_ref, w_ref, bias_ref, dst_ref, o_ref, buf_ref, pad_ref):
    del dst_ref
    t = a_ref.shape[0]
    hist = 4 * SUBLANES
    _fill_padded(pad_ref, hist, t, a_ref[...].astype(F32) * _sigmoid(b_ref[...].astype(F32)))
    buf_ref[0, 0] = pad_ref[hist + t - (CONV_B_W - 1):hist + t, :]

    def step(i, carry):
        base = pl.multiple_of(i * CONV_ROWS, CONV_ROWS)
        acc = jnp.zeros((CONV_ROWS, pad_ref.shape[1]), F32) + bias_ref[...]
        o_ref[pl.ds(base, CONV_ROWS), :] = _causal_taps(pad_ref, w_ref, base, hist, CONV_B_W, acc)
        return carry

    lax.fori_loop(0, t // CONV_ROWS, step, 0)


def _conv_b_prompt(z, w, bias, bsz, t, col0, d_b, dst):
    cb = 256
    nb = d_b // cb
    a0, b0 = col0 // cb, (col0 + d_b) // cb
    return pl.pallas_call(
        _conv_b_prompt_body,
        grid=(bsz, nb),
        in_specs=[pl.BlockSpec((t, cb), lambda b, j: (b, a0 + j)),
                  pl.BlockSpec((t, cb), lambda b, j: (b, b0 + j)),
                  pl.BlockSpec((CONV_B_W, cb), lambda b, j: (0, j)),
                  pl.BlockSpec((1, cb), lambda b, j: (0, j)),
                  pl.BlockSpec(memory_space=pl.ANY)],
        out_specs=[pl.BlockSpec((t, cb), lambda b, j: (b, j)),
                   pl.BlockSpec((1, 1, CONV_B_W - 1, cb), lambda b, j: (0, b, 0, j))],
        out_shape=[jax.ShapeDtypeStruct(dst.shape, F32),
                   jax.ShapeDtypeStruct((1, bsz, CONV_B_W - 1, d_b), F32)],
        input_output_aliases={4: 0},
        scratch_shapes=[pltpu.VMEM((t + 5 * SUBLANES, cb), F32)],
        compiler_params=_cparams(("parallel", "parallel")),
        name="conv_b_prompt",
    )(z, z, w, bias, dst)


def _conv_b_sample_body(n_valid, buf_ref, a_ref, b_ref, w_ref, bias_ref, dst_ref, o_ref, nbuf_ref):
    del dst_ref
    d_b = a_ref.shape[1]
    blk = a_ref.shape[0]
    _zero_tail_rows(o_ref, n_valid)
    glu = a_ref[...].astype(F32) * _sigmoid(b_ref[...].astype(F32))
    acc = bias_ref[...] + glu * w_ref[CONV_B_W - 1:CONV_B_W, :]
    for w in range(CONV_B_W - 1):
        acc = acc + buf_ref[:, w * d_b:(w + 1) * d_b] * w_ref[w:w + 1, :]
    o_ref[pl.ds(pl.multiple_of(pl.program_id(0) * blk, blk), blk), :] = acc
    nbuf_ref[:, :(CONV_B_W - 2) * d_b] = buf_ref[:, d_b:]
    nbuf_ref[:, (CONV_B_W - 2) * d_b:] = glu


def _conv_b_sample(buf, z, w, bias, row0, col0, d_b, dst):
    ns = buf.shape[0]
    blk = 2 * SAMPLE_BLOCK
    r0 = row0 // blk
    return pl.pallas_call(
        functools.partial(_conv_b_sample_body, ns),
        grid=(ns // blk,),
        in_specs=[pl.BlockSpec((blk, buf.shape[1]), lambda i: (i, 0)),
                  pl.BlockSpec((blk, d_b), lambda i: (r0 + i, col0 // d_b)),
                  pl.BlockSpec((blk, d_b), lambda i: (r0 + i, col0 // d_b + 1)),
                  pl.BlockSpec((CONV_B_W, d_b), lambda i: (0, 0)),
                  pl.BlockSpec((1, d_b), lambda i: (0, 0)),
                  pl.BlockSpec(memory_space=pl.ANY)],
        out_specs=[_tail_spec(z.shape[0], row0, d_b),
                   pl.BlockSpec((blk, buf.shape[1]), lambda i: (i, 0))],
        out_shape=[jax.ShapeDtypeStruct(dst.shape, F32), jax.ShapeDtypeStruct(buf.shape, F32)],
        input_output_aliases={5: 0},
        compiler_params=_cparams(("arbitrary",)),
        name="conv_b_sample",
    )(buf, z, z, w, bias, dst)


def _conv_c_prompt_body(gb_ref, gc_ref, hv_ref, w_ref, dst_ref, o_ref, buf_ref, pad_ref):
    del dst_ref
    t = gb_ref.shape[0]
    hist = SUBLANES
    _fill_padded(pad_ref, hist, t, gc_ref[...].astype(F32) * hv_ref[...].astype(F32))
    buf_ref[0, 0] = pad_ref[hist + t - (CONV_C_W - 1):hist + t, :]

    def step(i, carry):
        base = pl.multiple_of(i * CONV_ROWS, CONV_ROWS)
        acc = jnp.zeros((CONV_ROWS, pad_ref.shape[1]), F32)
        acc = _causal_taps(pad_ref, w_ref, base, hist, CONV_C_W, acc)
        o_ref[pl.ds(base, CONV_ROWS), :] = acc * gb_ref[pl.ds(base, CONV_ROWS), :].astype(F32)
        return carry

    lax.fori_loop(0, t // CONV_ROWS, step, 0)


def _conv_c_prompt(z, w, bsz, t, d_c, dst):
    cb = 512
    nb = d_c // cb
    return pl.pallas_call(
        _conv_c_prompt_body,
        grid=(bsz, nb),
        in_specs=[pl.BlockSpec((t, cb), lambda b, j: (b, j)),
                  pl.BlockSpec((t, cb), lambda b, j: (b, nb + j)),
                  pl.BlockSpec((t, cb), lambda b, j: (b, 2 * nb + j)),
                  pl.BlockSpec((CONV_C_W, cb), lambda b, j: (0, j)),
                  pl.BlockSpec(memory_space=pl.ANY)],
        out_specs=[pl.BlockSpec((t, cb), lambda b, j: (b, j)),
                   pl.BlockSpec((1, 1, CONV_C_W - 1, cb), lambda b, j: (0, b, 0, j))],
        out_shape=[jax.ShapeDtypeStruct(dst.shape, F32),
                   jax.ShapeDtypeStruct((1, bsz, CONV_C_W - 1, d_c), F32)],
        input_output_aliases={4: 0},
        scratch_shapes=[pltpu.VMEM((t + 2 * SUBLANES, cb), F32)],
        compiler_params=_cparams(("parallel", "parallel")),
        name="conv_c_prompt",
    )(z, z, z, w, dst)


def _conv_c_sample_body(n_valid, buf_ref, gb_ref, gc_ref, hv_ref, w_ref, dst_ref, o_ref, nbuf_ref):
    del dst_ref
    d_c = gb_ref.shape[1]
    blk = gb_ref.shape[0]
    _zero_tail_rows(o_ref, n_valid)
    u = gc_ref[...].astype(F32) * hv_ref[...].astype(F32)
    c = buf_ref[:, :d_c] * w_ref[0:1, :] + buf_ref[:, d_c:] * w_ref[1:2, :] + u * w_ref[2:3, :]
    o_ref[pl.ds(pl.multiple_of(pl.program_id(0) * blk, blk), blk), :] = gb_ref[...].astype(F32) * c
    nbuf_ref[:, :d_c] = buf_ref[:, d_c:]
    nbuf_ref[:, d_c:] = u


def _conv_c_sample(buf, z, w, row0, d_c, dst):
    ns = buf.shape[0]
    blk = 2 * SAMPLE_BLOCK
    r0 = row0 // blk
    return pl.pallas_call(
        functools.partial(_conv_c_sample_body, ns),
        grid=(ns // blk,),
        in_specs=[pl.BlockSpec((blk, buf.shape[1]), lambda i: (i, 0)),
                  pl.BlockSpec((blk, d_c), lambda i: (r0 + i, 0)),
                  pl.BlockSpec((blk, d_c), lambda i: (r0 + i, 1)),
                  pl.BlockSpec((blk, d_c), lambda i: (r0 + i, 2)),
                  pl.BlockSpec((CONV_C_W, d_c), lambda i: (0, 0)),
                  pl.BlockSpec(memory_space=pl.ANY)],
        out_specs=[_tail_spec(z.shape[0], row0, d_c),
                   pl.BlockSpec((blk, buf.shape[1]), lambda i: (i, 0))],
        out_shape=[jax.ShapeDtypeStruct(dst.shape, F32), jax.ShapeDtypeStruct(buf.shape, F32)],
        input_output_aliases={5: 0},
        compiler_params=_cparams(("arbitrary",)),
        name="conv_c_sample",
    )(buf, z, z, z, w, dst)


def _candidate_tables():
    pairs = [(a, b) for a in range(TOPK) for b in range(TOPK) if (a + 1) * (b + 1) <= TOPK]
    rows = -(-len(pairs) // SUBLANES) * SUBLANES
    p1 = np.zeros((rows, TOPK), np.float32)
    p2 = np.zeros((rows, TOPK), np.float32)
    neg = np.zeros((rows, LANES), np.float32)
    for r, (a, b) in enumerate(pairs):
        p1[r, a] = 1.0
        p2[r, b] = 1.0
    neg[len(pairs):] = -np.inf
    return p1, p2, neg, p1.T.copy()


def _take_top(works, index, count, tie_break):
    rows, lanes = works[0].shape
    slot = lax.broadcasted_iota(jnp.int32, (count, lanes), 0)
    works = list(works)
    vals = [jnp.zeros((count, lanes), F32) for _ in works]
    ranks = [jnp.full((rows, lanes), float(count), F32) for _ in works]
    for a in range(count):
        for i, work in enumerate(works):
            m = jnp.max(work, axis=0, keepdims=True)
            sel = work == m
            if tie_break:
                first = jnp.min(jnp.where(sel, index, float(rows)), axis=0, keepdims=True)
                sel = index == first
            ranks[i] = jnp.where(sel, float(a), ranks[i])
            works[i] = jnp.where(sel, -jnp.inf, work)
            vals[i] = jnp.where(slot == a, m, vals[i])
    return vals, ranks


def _count(mask):
    return jnp.sum(mask.astype(F32), axis=0, keepdims=True)


def _route_heads(s1, s2, p1, p2, neg, p1t, key_idx, cand_idx, tie_break):
    n = len(s1)
    vals, ranks = _take_top(list(s1) + list(s2), key_idx, TOPK, tie_break)
    v1, v2, rank1, rank2 = vals[:n], vals[n:], ranks[:n], ranks[n:]
    cand = [_dot_f32(p1, v1[i], ((1,), (0,))) + _dot_f32(p2, v2[i], ((1,), (0,))) + neg for i in range(n)]
    _, crank = _take_top(cand, cand_idx, TOPK, tie_break)
    out = []
    for i in range(n):
        chosen = crank[i] < float(TOPK)
        cmax = v1[i][0:1, :] + v2[i][0:1, :]
        z = jnp.sum(jnp.where(chosen, jnp.exp(cand[i] - cmax), 0.0), axis=0, keepdims=True)
        width = jnp.dot(p1t, chosen.astype(F32), preferred_element_type=F32)
        b1 = jnp.zeros_like(rank1[i])
        for a in range(TOPK):
            b1 = jnp.where(rank1[i] == float(a), width[a:a + 1, :], b1)
        in1 = rank1[i] < float(TOPK)
        in2 = rank2[i] < float(TOPK)
        c1 = jnp.where(in1, jnp.exp(jnp.where(in1, s1[i] - v1[i][0:1, :], 0.0)) / z, 0.0)
        e2 = jnp.where(in2, jnp.exp(jnp.where(in2, s2[i] - v2[i][0:1, :], 0.0)), 0.0)
        clean = ((_count(in1) == float(TOPK)) & (_count(in2) == float(TOPK))
                 & (_count(chosen) == float(TOPK)))
        out.append((c1, b1, rank2[i], e2, clean))
    return out


ROUTE_HEADS = 4


def _route_body(q_ref, keys_ref, p1_ref, p2_ref, neg_ref, p1t_ref, u_ref, v_ref,
                r2_ref, e2_ref, c1_ref, b1_ref, ub_ref, vt_ref, q3_ref):
    ub_ref[...] = u_ref[0].astype(BF16)
    vt_ref[...] = v_ref[0].T.astype(BF16)

    for hp in range(2 * P_HEADS):
        q3_ref[hp] = q_ref[:, hp * LANES:(hp + 1) * LANES]
    key_idx = lax.broadcasted_iota(jnp.int32, (N_KEYS, LANES), 0).astype(F32)
    cand_idx = lax.broadcasted_iota(jnp.int32, (p1_ref.shape[0], LANES), 0).astype(F32)

    def emit(heads, tie_break):
        s1 = [_dot_f32(keys_ref[h, 0], q3_ref[2 * h], ((1,), (1,))) for h in heads]
        s2 = [_dot_f32(keys_ref[h, 1], q3_ref[2 * h + 1], ((1,), (1,))) for h in heads]
        rows = _route_heads(s1, s2, p1_ref[...], p2_ref[...], neg_ref[...], p1t_ref[...],
                            key_idx, cand_idx, tie_break)
        all_clean = None
        for h, (c1, b1, r2, e2, clean) in zip(heads, rows):
            c1_ref[h] = c1
            b1_ref[h] = b1
            r2_ref[h] = r2.astype(r2_ref.dtype)
            e2_ref[h] = e2.astype(e2_ref.dtype)
            all_clean = clean if all_clean is None else all_clean & clean
        return all_clean

    def step(i, carry):
        heads = [i * ROUTE_HEADS + j for j in range(ROUTE_HEADS)]
        all_clean = emit(heads, False)

        @pl.when(jnp.min(all_clean.astype(F32)) < 0.5)
        def _():
            emit(heads, True)

        return carry

    lax.fori_loop(0, P_HEADS // ROUTE_HEADS, step, 0)


def _route(q, keys, table_u, table_v, layer):
    n = q.shape[0]
    _, n_exp, d = table_u.shape
    steps = n // LANES
    n_slabs = min(n_exp // TABLE_SLAB, 1 << (steps.bit_length() - 1))
    slab = n_exp // n_slabs
    assert n_slabs * slab == n_exp
    p1, p2, neg, p1t = _candidate_tables()
    whole = lambda a: pl.BlockSpec(a.shape, lambda i: (0,) * a.ndim)
    out_spec = pl.BlockSpec((P_HEADS, N_KEYS, LANES), lambda i: (0, 0, i))
    shape = lambda dt: jax.ShapeDtypeStruct((P_HEADS, N_KEYS, n), dt)
    slab_of = lambda i: jnp.minimum(i, n_slabs - 1)
    table_spec = pl.BlockSpec((1, slab, d), lambda i: (layer, slab_of(i), 0))
    return pl.pallas_call(
        _route_body,
        grid=(steps,),
        in_specs=[pl.BlockSpec((LANES, q.shape[1]), lambda i: (i, 0)), whole(keys),
                  whole(p1), whole(p2), whole(neg), whole(p1t), table_spec, table_spec],
        out_specs=[out_spec] * 4 + [pl.BlockSpec((slab, d), lambda i: (slab_of(i), 0)),
                                    pl.BlockSpec((d, slab), lambda i: (0, slab_of(i)))],
        out_shape=[shape(BF16), shape(BF16), shape(F32), shape(F32),
                   jax.ShapeDtypeStruct((n_exp, d), BF16), jax.ShapeDtypeStruct((d, n_exp), BF16)],
        scratch_shapes=[pltpu.VMEM((2 * P_HEADS, LANES, LANES), F32)],
        compiler_params=_cparams(("arbitrary",)),
        name="peer_route",
    )(q, keys, jnp.asarray(p1), jnp.asarray(p2), jnp.asarray(neg), jnp.asarray(p1t), table_u, table_v)


def _bf16_pair(c):
    hi = float(np.asarray(c, dtype=BF16))
    return hi, float(np.asarray(c - hi, dtype=BF16))


def _gelu_tanh(x):
    c0 = float(np.sqrt(2.0 / np.pi))
    c0_hi, c0_lo = _bf16_pair(c0)
    c1_hi, c1_lo = _bf16_pair(c0 * 0.044715)
    x2 = x * x
    inner = x * ((c1_hi * x2 + c0_hi) + (c1_lo * x2 + c0_lo))
    half = 0.5 * x
    return half + half * jnp.tanh(inner)


def _peer_dense_body(split, xt_ref, u_ref, vt_ref, r2_ref, e2_ref, c1_ref, b1_ref, x_ref, gf_ref, *rest):
    if split is None:
        o_ref, acc_ref = rest
    else:
        o_ref, ys_ref, acc_ref = rest
    e = pl.program_id(1)
    n_groups = xt_ref.shape[1] // PEER_LANES

    @pl.when(e == 0)
    def _():
        acc_ref[...] = jnp.zeros_like(acc_ref)

    def scores(j):
        cols = slice(j * PEER_LANES, (j + 1) * PEER_LANES)
        return jnp.dot(u_ref[...], xt_ref[:, cols], preferred_element_type=F32)

    def gates(j):
        cols = slice(j * PEER_LANES, (j + 1) * PEER_LANES)
        parts = []
        for r in range(EXPERT_BLOCK // N_KEYS):
            gate = jnp.zeros((N_KEYS, PEER_LANES), BF16)
            for h in range(P_HEADS):
                hit = r2_ref[h, :, cols] < b1_ref[h, r:r + 1, cols].astype(BF16)
                gate = gate + c1_ref[h, r:r + 1, cols].astype(BF16) * jnp.where(
                    hit, e2_ref[h, :, cols], jnp.zeros((), BF16))
            parts.append(gate)
        return jnp.concatenate(parts, axis=0)

    s_next = scores(0)
    for j in range(n_groups):
        s = s_next
        if j + 1 < n_groups:
            s_next = scores(j + 1)
        cols = slice(j * PEER_LANES, (j + 1) * PEER_LANES)
        weighted = gates(j) * _gelu_tanh(s.astype(BF16))
        acc_ref[:, cols] += jnp.dot(vt_ref[...], weighted, preferred_element_type=F32)

    @pl.when(e == pl.num_programs(1) - 1)
    def _():
        for j in range(n_groups):
            rows = slice(j * PEER_LANES, (j + 1) * PEER_LANES)
            out = x_ref[rows, :] + acc_ref[:, rows].T
            if split is None:
                o_ref[rows, :] = out
                continue
            y = _rmsnorm(out, gf_ref[...])
            o_ref[rows, :] = y
            if j == split.group:
                @pl.when(pl.program_id(0) == split.block)
                def _():
                    ys_ref[...] = y[:ys_ref.shape[0], :]


class _SampleSplit(NamedTuple):
    block: int
    group: int


def _peer_dense(xt, u_b, vt_b, r2, e2, c1, b1, x, g_final, n_prompt=None, n_sample=None):
    d, n = xt.shape
    n_exp = u_b.shape[0]
    keys_per_block = EXPERT_BLOCK // N_KEYS
    once = pl.Buffered(1)
    rank_spec = pl.BlockSpec((P_HEADS, N_KEYS, TOK_BLOCK), lambda i, e: (0, 0, i), pipeline_mode=once)
    first_spec = pl.BlockSpec((P_HEADS, keys_per_block, TOK_BLOCK), lambda i, e: (0, e, i))
    row_spec = pl.BlockSpec((TOK_BLOCK, d), lambda i, e: (i, 0), pipeline_mode=once)
    if n_prompt is None:
        split = None
        out_specs = row_spec
        out_shape = jax.ShapeDtypeStruct((n, d), F32)
    else:
        start = n_prompt % TOK_BLOCK
        assert start % PEER_LANES == 0 and n_sample <= PEER_LANES and n_prompt + n_sample <= n
        split = _SampleSplit(n_prompt // TOK_BLOCK, start // PEER_LANES)
        out_specs = [row_spec, pl.BlockSpec((n_sample, d), lambda i, e: (0, 0))]
        out_shape = [jax.ShapeDtypeStruct((n_prompt, d), F32), jax.ShapeDtypeStruct((n_sample, d), F32)]
    return pl.pallas_call(
        functools.partial(_peer_dense_body, split),
        grid=(n // TOK_BLOCK, n_exp // EXPERT_BLOCK),
        in_specs=[pl.BlockSpec((d, TOK_BLOCK), lambda i, e: (0, i), pipeline_mode=once),
                  pl.BlockSpec((EXPERT_BLOCK, d), lambda i, e: (e, 0)),
                  pl.BlockSpec((d, EXPERT_BLOCK), lambda i, e: (0, e)),
                  rank_spec, rank_spec, first_spec, first_spec, row_spec,
                  pl.BlockSpec((1, d), lambda i, e: (0, 0))],
        out_specs=out_specs,
        out_shape=out_shape,
        scratch_shapes=[pltpu.VMEM((d, TOK_BLOCK), F32)],
        compiler_params=_cparams(("arbitrary", "arbitrary")),
        name="peer_dense" if split is None else "peer_dense_final",
    )(xt, u_b, vt_b, r2, e2, c1, b1, x, g_final)


def _peer(x, g_norm, wq, keys, table_u, table_v, layer, g_final, n_prompt=None, n_sample=None):
    q, xt = _norm_matmul_t(x, g_norm, wq, wq.shape[1])
    r2, e2, c1, b1, u_b, vt_b = _route(q, keys, table_u, table_v, layer)
    return _peer_dense(xt, u_b, vt_b, r2, e2, c1, b1, x, g_final, n_prompt, n_sample)


def _col_form(a, n_dk):
    ns = a.shape[0]
    a = a.reshape(ns // SAMPLE_BLOCK, SAMPLE_BLOCK, H_A, n_dk)
    return a.transpose(0, 2, 3, 1)


def kernel(x_prompt, x_sample, state_gla, state_conv_b, state_conv_c, norm_mix, norm_ffn, norm_final, w_in_even, w_a2, b_a2, gla_norm_g, conv_b_w, conv_b_bias, conv_b_ln_g, conv_b_ln_b, w_out_even, w_in_odd, conv_c_w, w_out_odd, peer_wq, peer_keys, peer_u, peer_v):
    bsz, t, d = x_prompt.shape
    ns = x_sample.shape[0]
    n_prompt = bsz * t
    n_real = n_prompt + ns
    n_tok = -(-n_real // TOK_BLOCK) * TOK_BLOCK
    assert t % GLA_CHUNK == 0 and t % CONV_ROWS == 0 and ns % (2 * SAMPLE_BLOCK) == 0
    assert x_sample.shape[1] == 1 and n_prompt % (2 * SAMPLE_BLOCK) == 0

    d_a = d // 2
    dk_a = d_a // 2
    d_b = d - d_a
    rank = w_a2.shape[1]
    col_a = 2 * dk_a + 2 * d_a
    row = lambda a: a.reshape(1, -1)

    x = jnp.concatenate([x_prompt.reshape(n_prompt, d), x_sample.reshape(ns, d),
                         jnp.zeros((n_tok - n_real, d), F32)], axis=0)

    w_in = w_in_even[0]
    w_main = jnp.concatenate([w_in[:, :col_a], w_in[:, col_a + rank:]], axis=1).astype(BF16)
    w_gate = jnp.pad(w_in[:, col_a:col_a + rank], ((0, 0), (0, LANES - rank))).astype(BF16)
    w_a2p = jnp.pad(w_a2[0], ((0, LANES - rank), (0, 0)))
    z, la = _norm_matmul_gate(x, row(norm_mix[0]), w_main, w_gate, w_a2p, row(b_a2[0]), 1024)

    gn = row(gla_norm_g[0])
    zs = z[n_prompt:n_real].astype(F32)
    las = la[n_prompt:n_real]
    oa, gla_s = _gla_sample(state_gla, _col_form(las, dk_a // H_A), _col_form(zs[:, dk_a:2 * dk_a], dk_a // H_A),
                            _col_form(zs[:, :dk_a], dk_a // H_A),
                            zs[:, 2 * dk_a:2 * dk_a + d_a], zs[:, 2 * dk_a + d_a:col_a], gn,
                            jnp.zeros((n_tok, d_a), F32), n_prompt)
    oa, gla_p = _gla_prompt(z, la, gn, bsz, t, oa)

    ub, cb_s = _conv_b_sample(state_conv_b[0].reshape(ns, -1), z, conv_b_w[0], row(conv_b_bias[0]),
                              n_prompt, col_a, d_b, jnp.zeros((n_tok, d_b), F32))
    ub, cb_p = _conv_b_prompt(z, conv_b_w[0], row(conv_b_bias[0]), bsz, t, col_a, d_b, ub)
    x = _res_matmul_ln(oa, ub, row(conv_b_ln_g[0]), row(conv_b_ln_b[0]), w_out_even[0].astype(BF16), x, d)
    wq_b = peer_wq.astype(BF16)
    x = _peer(x, row(norm_ffn[0]), wq_b[0], peer_keys[0], peer_u, peer_v, 0, row(norm_final))

    z = _norm_matmul(x, row(norm_mix[1]), w_in_odd[0].astype(BF16), 1024)
    gc, cc_s = _conv_c_sample(state_conv_c[0].reshape(ns, -1), z, conv_c_w[0], n_prompt, d,
                              jnp.zeros((n_tok, d), F32))
    gc, cc_p = _conv_c_prompt(z, conv_c_w[0], bsz, t, d, gc)
    x = _res_matmul(gc, w_out_odd[0].astype(BF16), x, d)
    y_prompt, y_sample = _peer(x, row(norm_ffn[1]), wq_b[1], peer_keys[1], peer_u, peer_v, 1, row(norm_final),
                               n_prompt, ns)
    return (y_prompt.reshape(bsz, t, d), y_sample.reshape(ns, 1, d), gla_p, cb_p, cc_p,
            gla_s, cb_s.reshape(state_conv_b.shape), cc_s.reshape(state_conv_c.shape))
```

```python
import functools
from typing import NamedTuple

import numpy as np
import jax
import jax.numpy as jnp
from jax import lax
from jax.experimental import pallas as pl
from jax.experimental.pallas import tpu as pltpu

F32 = jnp.float32
BF16 = jnp.bfloat16
HIGHEST = lax.Precision.HIGHEST

EPS = 1e-6
GATE_TAU = 16.0
H_A = 4
P_HEADS = 8
N_KEYS = 128
TOPK = 16
CONV_B_W = 31
CONV_C_W = 3

LANES = 128
SUBLANES = 8
VMEM_LIMIT = 56 * 1024 * 1024

TOK_BLOCK = 768
EXPERT_BLOCK = 1024
PEER_LANES = 256
TABLE_SLAB = 256
GLA_CHUNK = 128
GLA_SUB = 16
EXP_CLAMP = 80.0
CONV_ROWS = 64
SAMPLE_BLOCK = 8


def _cparams(sem):
    return pltpu.CompilerParams(dimension_semantics=sem, vmem_limit_bytes=VMEM_LIMIT)


def _rmsnorm(x, g):
    return x * lax.rsqrt(jnp.mean(x * x, axis=-1, keepdims=True) + EPS) * g


def _sigmoid(x):
    return jax.nn.sigmoid(x)


def _weight_spec(k, m, tn):
    if tn == m:
        return pl.BlockSpec((k, tn), lambda i, j: (0, 0), pipeline_mode=pl.Buffered(1))
    return pl.BlockSpec((k, tn), lambda i, j: (0, j))


def _log_sigmoid(x):
    return jnp.minimum(x, 0.0) - jnp.log(1.0 + jnp.exp(-jnp.abs(x)))


def _norm_matmul_body(x_ref, g_ref, w_ref, o_ref, xn_ref):
    @pl.when(pl.program_id(1) == 0)
    def _():
        xn_ref[...] = _rmsnorm(x_ref[...], g_ref[...]).astype(BF16)

    o_ref[...] = jnp.dot(xn_ref[...], w_ref[...], preferred_element_type=F32).astype(o_ref.dtype)


def _norm_matmul(x, g, w, tn):
    n, d = x.shape
    m = w.shape[1]
    return pl.pallas_call(
        _norm_matmul_body,
        grid=(n // TOK_BLOCK, m // tn),
        in_specs=[pl.BlockSpec((TOK_BLOCK, d), lambda i, j: (i, 0)),
                  pl.BlockSpec((1, d), lambda i, j: (0, 0)),
                  pl.BlockSpec((d, tn), lambda i, j: (0, j))],
        out_specs=pl.BlockSpec((TOK_BLOCK, tn), lambda i, j: (i, j)),
        out_shape=jax.ShapeDtypeStruct((n, m), BF16),
        scratch_shapes=[pltpu.VMEM((TOK_BLOCK, d), BF16)],
        compiler_params=_cparams(("parallel", "arbitrary")),
        name="norm_matmul",
    )(x, g, w)


def _norm_matmul_gate_body(x_ref, g_ref, w_ref, wa_ref, wa2_ref, ba2_ref, o_ref, la_ref, xn_ref):
    @pl.when(pl.program_id(1) == 0)
    def _():
        xn = _rmsnorm(x_ref[...], g_ref[...]).astype(BF16)
        xn_ref[...] = xn
        a_lr = jnp.dot(xn, wa_ref[...], preferred_element_type=F32)
        pre = jnp.dot(a_lr, wa2_ref[...], precision=HIGHEST, preferred_element_type=F32) + ba2_ref[...]
        la_ref[...] = _log_sigmoid(pre) / GATE_TAU

    o_ref[...] = jnp.dot(xn_ref[...], w_ref[...], preferred_element_type=F32).astype(o_ref.dtype)


def _norm_matmul_gate(x, g, w, wa, wa2, ba2, tn):
    n, d = x.shape
    m = w.shape[1]
    dk = wa2.shape[1]
    return pl.pallas_call(
        _norm_matmul_gate_body,
        grid=(n // TOK_BLOCK, m // tn),
        in_specs=[pl.BlockSpec((TOK_BLOCK, d), lambda i, j: (i, 0)),
                  pl.BlockSpec((1, d), lambda i, j: (0, 0)),
                  pl.BlockSpec((d, tn), lambda i, j: (0, j)),
                  pl.BlockSpec((d, LANES), lambda i, j: (0, 0)),
                  pl.BlockSpec((LANES, dk), lambda i, j: (0, 0)),
                  pl.BlockSpec((1, dk), lambda i, j: (0, 0))],
        out_specs=[pl.BlockSpec((TOK_BLOCK, tn), lambda i, j: (i, j)),
                   pl.BlockSpec((TOK_BLOCK, dk), lambda i, j: (i, 0))],
        out_shape=[jax.ShapeDtypeStruct((n, m), BF16), jax.ShapeDtypeStruct((n, dk), F32)],
        scratch_shapes=[pltpu.VMEM((TOK_BLOCK, d), BF16)],
        compiler_params=_cparams(("parallel", "arbitrary")),
        name="norm_matmul_gate",
    )(x, g, w, wa, wa2, ba2)


def _norm_matmul_t_body(x_ref, g_ref, w_ref, o_ref, xt_ref, xn_ref):
    @pl.when(pl.program_id(1) == 0)
    def _():
        xn = _rmsnorm(x_ref[...], g_ref[...])
        xn_ref[...] = xn.astype(BF16)
        xt_ref[...] = xn.T.astype(BF16)

    o_ref[...] = jnp.dot(xn_ref[...], w_ref[...], preferred_element_type=F32)


def _norm_matmul_t(x, g, w, tn):
    n, d = x.shape
    m = w.shape[1]
    return pl.pallas_call(
        _norm_matmul_t_body,
        grid=(n // TOK_BLOCK, m // tn),
        in_specs=[pl.BlockSpec((TOK_BLOCK, d), lambda i, j: (i, 0)),
                  pl.BlockSpec((1, d), lambda i, j: (0, 0)),
                  _weight_spec(d, m, tn)],
        out_specs=[pl.BlockSpec((TOK_BLOCK, tn), lambda i, j: (i, j)),
                   pl.BlockSpec((d, TOK_BLOCK), lambda i, j: (0, i))],
        out_shape=[jax.ShapeDtypeStruct((n, m), F32), jax.ShapeDtypeStruct((d, n), BF16)],
        scratch_shapes=[pltpu.VMEM((TOK_BLOCK, d), BF16)],
        compiler_params=_cparams(("parallel", "arbitrary")),
        name="norm_matmul_t",
    )(x, g, w)


def _res_matmul_body(a_ref, w_ref, x_ref, o_ref, ab_ref):
    @pl.when(pl.program_id(1) == 0)
    def _():
        ab_ref[...] = a_ref[...].astype(BF16)

    o_ref[...] = x_ref[...] + jnp.dot(ab_ref[...], w_ref[...], preferred_element_type=F32)


def _res_matmul(a, w, x, tn):
    n, k = a.shape
    m = w.shape[1]
    return pl.pallas_call(
        _res_matmul_body,
        grid=(n // TOK_BLOCK, m // tn),
        in_specs=[pl.BlockSpec((TOK_BLOCK, k), lambda i, j: (i, 0)),
                  _weight_spec(k, m, tn),
                  pl.BlockSpec((TOK_BLOCK, tn), lambda i, j: (i, j))],
        out_specs=pl.BlockSpec((TOK_BLOCK, tn), lambda i, j: (i, j)),
        out_shape=jax.ShapeDtypeStruct((n, m), F32),
        scratch_shapes=[pltpu.VMEM((TOK_BLOCK, k), BF16)],
        compiler_params=_cparams(("parallel", "arbitrary")),
        name="res_matmul",
    )(a, w, x)


def _res_matmul_ln_body(oa_ref, ub_ref, lg_ref, lb_ref, w_ref, x_ref, o_ref, ab_ref):
    da = oa_ref.shape[1]

    @pl.when(pl.program_id(1) == 0)
    def _():
        u = ub_ref[...]
        uc = u - jnp.mean(u, axis=-1, keepdims=True)
        y = uc * lax.rsqrt(jnp.mean(uc * uc, axis=-1, keepdims=True) + EPS) * lg_ref[...] + lb_ref[...]
        ab_ref[:, :da] = oa_ref[...].astype(BF16)
        ab_ref[:, da:] = (y * _sigmoid(y)).astype(BF16)

    o_ref[...] = x_ref[...] + jnp.dot(ab_ref[...], w_ref[...], preferred_element_type=F32)


def _res_matmul_ln(oa, ub, lg, lb, w, x, tn):
    n, da = oa.shape
    db = ub.shape[1]
    m = w.shape[1]
    return pl.pallas_call(
        _res_matmul_ln_body,
        grid=(n // TOK_BLOCK, m // tn),
        in_specs=[pl.BlockSpec((TOK_BLOCK, da), lambda i, j: (i, 0)),
                  pl.BlockSpec((TOK_BLOCK, db), lambda i, j: (i, 0)),
                  pl.BlockSpec((1, db), lambda i, j: (0, 0)),
                  pl.BlockSpec((1, db), lambda i, j: (0, 0)),
                  _weight_spec(da + db, m, tn),
                  pl.BlockSpec((TOK_BLOCK, tn), lambda i, j: (i, j))],
        out_specs=pl.BlockSpec((TOK_BLOCK, tn), lambda i, j: (i, j)),
        out_shape=jax.ShapeDtypeStruct((n, m), F32),
        scratch_shapes=[pltpu.VMEM((TOK_BLOCK, da + db), BF16)],
        compiler_params=_cparams(("parallel", "arbitrary")),
        name="res_matmul_ln",
    )(oa, ub, lg, lb, w, x)


def _dot_f32(a, b, dims):
    return lax.dot_general(a, b, (dims, ((), ())), precision=HIGHEST, preferred_element_type=F32)


def _dot_bf16(a, b, dims):
    return lax.dot_general(a.astype(BF16), b.astype(BF16), (dims, ((), ())), preferred_element_type=F32)


def _head_out(o, gn, g):
    y = o * lax.rsqrt(jnp.mean(o * o, axis=-1, keepdims=True) + EPS) * gn
    return y * (g * _sigmoid(g))


def _gla_prompt_body(q_ref, k_ref, v_ref, g_ref, la_ref, gn_ref, dst_ref, o_ref, s_out_ref, s_ref, acc_ref):
    del dst_ref
    c = pl.program_id(1)
    dk = GLA_CHUNK
    dv = s_ref.shape[2]

    @pl.when(c == 0)
    def _():
        s_ref[...] = jnp.zeros_like(s_ref)

    row = lax.broadcasted_iota(jnp.int32, (GLA_CHUNK, GLA_CHUNK), 0)
    col = lax.broadcasted_iota(jnp.int32, (GLA_CHUNK, GLA_CHUNK), 1)
    tri = (row >= col).astype(F32)
    sub_row = lax.broadcasted_iota(jnp.int32, (GLA_SUB, GLA_SUB), 0)
    sub_col = lax.broadcasted_iota(jnp.int32, (GLA_SUB, GLA_SUB), 1)
    causal = sub_row >= sub_col

    for h in range(H_A):
        kcols = slice(h * dk, (h + 1) * dk)
        vcols = slice(h * dv, (h + 1) * dv)
        la = la_ref[:, kcols]
        b = _dot_f32(tri, la, ((1,), (0,)))
        b_last = b[GLA_CHUNK - 1:GLA_CHUNK, :]
        q = q_ref[:, kcols].astype(F32) * (dk ** -0.5)
        k = k_ref[:, kcols].astype(F32)
        v_b = v_ref[:, vcols].astype(BF16)
        s = s_ref[h]

        acc_ref[h] = _dot_bf16(q * jnp.exp(b), s, ((1,), (0,)))
        for j in range(GLA_CHUNK // GLA_SUB):
            lo, hi = j * GLA_SUB, (j + 1) * GLA_SUB
            b_j = b[lo:hi, :]
            b_first = b[lo:lo + 1, :]
            b_end = b[hi - 1:hi, :]
            q_d = q[lo:hi, :] * jnp.exp(b_j - b_first)
            k_d = k[lo:hi, :] * jnp.exp(jnp.minimum(b_first - b_j, EXP_CLAMP))
            sc = jnp.where(causal, _dot_bf16(q_d, k_d, ((1,), (1,))), 0.0)
            acc_ref[h, lo:hi, :] += _dot_bf16(sc, v_b[lo:hi, :], ((1,), (0,)))
            if hi < GLA_CHUNK:
                k_p = k[lo:hi, :] * jnp.exp(b_end - b_j)
                q_p = q[hi:, :] * jnp.exp(b[hi:, :] - b_end)
                sc = _dot_bf16(q_p, k_p, ((1,), (1,)))
                acc_ref[h, hi:, :] += _dot_bf16(sc, v_b[lo:hi, :], ((1,), (0,)))

        decay = jnp.exp(jnp.broadcast_to(b_last, (dk, dk))).T
        k_e = k * jnp.exp(b_last - b)
        s_new = jnp.concatenate([decay] * (dv // dk), axis=1) * s + _dot_bf16(k_e.T, v_b, ((1,), (0,)))
        s_ref[h] = s_new
        o_ref[:, vcols] = _head_out(acc_ref[h], gn_ref[:, vcols], g_ref[:, vcols].astype(F32))

    @pl.when(c == pl.num_programs(1) - 1)
    def _():
        s_out_ref[0, 0] = s_ref[...]


def _gla_prompt(z, la, gn, bsz, t, dst):
    dk, dv = GLA_CHUNK, 2 * GLA_CHUNK
    d_k, d_v = H_A * dk, H_A * dv
    nc = t // GLA_CHUNK
    rows = lambda b, c: b * nc + c
    return pl.pallas_call(
        _gla_prompt_body,
        grid=(bsz, nc),
        in_specs=[pl.BlockSpec((GLA_CHUNK, d_k), lambda b, c: (rows(b, c), 0)),
                  pl.BlockSpec((GLA_CHUNK, d_k), lambda b, c: (rows(b, c), 1)),
                  pl.BlockSpec((GLA_CHUNK, d_v), lambda b, c: (rows(b, c), 2 * d_k // d_v)),
                  pl.BlockSpec((GLA_CHUNK, d_v), lambda b, c: (rows(b, c), 2 * d_k // d_v + 1)),
                  pl.BlockSpec((GLA_CHUNK, d_k), lambda b, c: (rows(b, c), 0)),
                  pl.BlockSpec((1, d_v), lambda b, c: (0, 0)),
                  pl.BlockSpec(memory_space=pl.ANY)],
        out_specs=[pl.BlockSpec((GLA_CHUNK, d_v), lambda b, c: (rows(b, c), 0)),
                   pl.BlockSpec((1, 1, H_A, dk, dv), lambda b, c: (0, b, 0, 0, 0))],
        out_shape=[jax.ShapeDtypeStruct(dst.shape, F32),
                   jax.ShapeDtypeStruct((1, bsz, H_A, dk, dv), F32)],
        input_output_aliases={6: 0},
        scratch_shapes=[pltpu.VMEM((H_A, dk, dv), F32), pltpu.VMEM((H_A, GLA_CHUNK, dv), F32)],
        compiler_params=_cparams(("parallel", "arbitrary")),
        name="gla_prompt",
    )(z, z, z, z, la, gn, dst)


def _tail_spec(n_rows, row0, cols):
    tail = n_rows - row0
    assert row0 % tail == 0 and tail % SUBLANES == 0
    return pl.BlockSpec((tail, cols), lambda i: (row0 // tail, 0))


def _zero_tail_rows(o_ref, n_valid):
    @pl.when(pl.program_id(0) == 0)
    def _():
        o_ref[n_valid:, :] = jnp.zeros((o_ref.shape[0] - n_valid, o_ref.shape[1]), F32)


def _gla_sample_body(n_valid, st_ref, lat_ref, kt_ref, qt_ref, v_ref, g_ref, gn_ref, dst_ref, o_ref, s_out_ref,
                     rows_ref):
    del dst_ref
    dv = st_ref.shape[-1]
    _zero_tail_rows(o_ref, n_valid)
    for i in range(SAMPLE_BLOCK):
        for h in range(H_A):
            decay = jnp.exp(lat_ref[0, h, :, i:i + 1])
            v_row = v_ref[i:i + 1, h * dv:(h + 1) * dv]
            s_new = decay * st_ref[0, i, h] + kt_ref[0, h, :, i:i + 1] * v_row
            s_out_ref[0, i, h] = s_new
            q_col = qt_ref[0, h, :, i:i + 1] * (st_ref.shape[-2] ** -0.5)
            o = jnp.sum(q_col * s_new, axis=0, keepdims=True)
            rows_ref[i:i + 1, h * dv:(h + 1) * dv] = _head_out(
                o, gn_ref[:, h * dv:(h + 1) * dv], g_ref[i:i + 1, h * dv:(h + 1) * dv])
    base = pl.multiple_of(pl.program_id(0) * SAMPLE_BLOCK, SAMPLE_BLOCK)
    o_ref[pl.ds(base, SAMPLE_BLOCK), :] = rows_ref[...]


def _gla_sample(state, lat, kt, qt, v, g, gn, dst, row0):
    _, ns, _, dk, dv = state.shape
    n_rows = dst.shape[0]
    steps = ns // SAMPLE_BLOCK
    col_spec = pl.BlockSpec((1, H_A, dk, SAMPLE_BLOCK), lambda i: (i, 0, 0, 0))
    row_spec = pl.BlockSpec((SAMPLE_BLOCK, H_A * dv), lambda i: (i, 0))
    st_spec = pl.BlockSpec((1, SAMPLE_BLOCK, H_A, dk, dv), lambda i: (0, i, 0, 0, 0))
    return pl.pallas_call(
        functools.partial(_gla_sample_body, ns),
        grid=(steps,),
        in_specs=[st_spec, col_spec, col_spec, col_spec, row_spec, row_spec,
                  pl.BlockSpec((1, H_A * dv), lambda i: (0, 0)),
                  pl.BlockSpec(memory_space=pl.ANY)],
        out_specs=[_tail_spec(n_rows, row0, H_A * dv), st_spec],
        out_shape=[jax.ShapeDtypeStruct(dst.shape, F32), jax.ShapeDtypeStruct(state.shape, F32)],
        input_output_aliases={7: 0},
        scratch_shapes=[pltpu.VMEM((SAMPLE_BLOCK, H_A * dv), F32)],
        compiler_params=_cparams(("arbitrary",)),
        name="gla_sample",
    )(state, lat, kt, qt, v, g, gn, dst)


def _fill_padded(pad_ref, hist, t, value):
    cols = pad_ref.shape[1]
    pad_ref[0:hist, :] = jnp.zeros((hist, cols), F32)
    pad_ref[hist:hist + t, :] = value
    pad_ref[hist + t:, :] = jnp.zeros((SUBLANES, cols), F32)


def _causal_taps(pad_ref, w_ref, base, hist, n_taps, acc):
    first = hist - (n_taps - 1)
    win = pad_ref[pl.ds(base, CONV_ROWS + hist + SUBLANES), :]
    for r in range(SUBLANES):
        taps = [w for w in range(n_taps) if (first + w) % SUBLANES == r]
        if not taps:
            continue
        shifted = win[r:r + CONV_ROWS + hist, :]
        for w in taps:
            lo = (first + w) // SUBLANES * SUBLANES
            acc = acc + shifted[lo:lo + CONV_ROWS, :] * w_ref[w:w + 1, :]
    return acc


def _conv_b_prompt_body(a_ref, b_ref, w_ref, bias_ref, dst_ref, o_ref, buf_ref, pad_ref):
    del dst_ref
    t = a_ref.shape[0]
    hist = 4 * SUBLANES
    _fill_padded(pad_ref, hist, t, a_ref[...].astype(F32) * _sigmoid(b_ref[...].astype(F32)))
    buf_ref[0, 0] = pad_ref[hist + t - (CONV_B_W - 1):hist + t, :]

    def step(i, carry):
        base = pl.multiple_of(i * CONV_ROWS, CONV_ROWS)
        acc = jnp.zeros((CONV_ROWS, pad_ref.shape[1]), F32) + bias_ref[...]
        o_ref[pl.ds(base, CONV_ROWS), :] = _causal_taps(pad_ref, w_ref, base, hist, CONV_B_W, acc)
        return carry

    lax.fori_loop(0, t // CONV_ROWS, step, 0)


def _conv_b_prompt(z, w, bias, bsz, t, col0, d_b, dst):
    cb = 256
    nb = d_b // cb
    a0, b0 = col0 // cb, (col0 + d_b) // cb
    return pl.pallas_call(
        _conv_b_prompt_body,
        grid=(bsz, nb),
        in_specs=[pl.BlockSpec((t, cb), lambda b, j: (b, a0 + j)),
                  pl.BlockSpec((t, cb), lambda b, j: (b, b0 + j)),
                  pl.BlockSpec((CONV_B_W, cb), lambda b, j: (0, j)),
                  pl.BlockSpec((1, cb), lambda b, j: (0, j)),
                  pl.BlockSpec(memory_space=pl.ANY)],
        out_specs=[pl.BlockSpec((t, cb), lambda b, j: (b, j)),
                   pl.BlockSpec((1, 1, CONV_B_W - 1, cb), lambda b, j: (0, b, 0, j))],
        out_shape=[jax.ShapeDtypeStruct(dst.shape, F32),
                   jax.ShapeDtypeStruct((1, bsz, CONV_B_W - 1, d_b), F32)],
        input_output_aliases={4: 0},
        scratch_shapes=[pltpu.VMEM((t + 5 * SUBLANES, cb), F32)],
        compiler_params=_cparams(("parallel", "parallel")),
        name="conv_b_prompt",
    )(z, z, w, bias, dst)


def _conv_b_sample_body(n_valid, buf_ref, a_ref, b_ref, w_ref, bias_ref, dst_ref, o_ref, nbuf_ref):
    del dst_ref
    d_b = a_ref.shape[1]
    blk = a_ref.shape[0]
    _zero_tail_rows(o_ref, n_valid)
    glu = a_ref[...].astype(F32) * _sigmoid(b_ref[...].astype(F32))
    acc = bias_ref[...] + glu * w_ref[CONV_B_W - 1:CONV_B_W, :]
    for w in range(CONV_B_W - 1):
        acc = acc + buf_ref[:, w * d_b:(w + 1) * d_b] * w_ref[w:w + 1, :]
    o_ref[pl.ds(pl.multiple_of(pl.program_id(0) * blk, blk), blk), :] = acc
    nbuf_ref[:, :(CONV_B_W - 2) * d_b] = buf_ref[:, d_b:]
    nbuf_ref[:, (CONV_B_W - 2) * d_b:] = glu


def _conv_b_sample(buf, z, w, bias, row0, col0, d_b, dst):
    ns = buf.shape[0]
    blk = 2 * SAMPLE_BLOCK
    r0 = row0 // blk
    return pl.pallas_call(
        functools.partial(_conv_b_sample_body, ns),
        grid=(ns // blk,),
        in_specs=[pl.BlockSpec((blk, buf.shape[1]), lambda i: (i, 0)),
                  pl.BlockSpec((blk, d_b), lambda i: (r0 + i, col0 // d_b)),
                  pl.BlockSpec((blk, d_b), lambda i: (r0 + i, col0 // d_b + 1)),
                  pl.BlockSpec((CONV_B_W, d_b), lambda i: (0, 0)),
                  pl.BlockSpec((1, d_b), lambda i: (0, 0)),
                  pl.BlockSpec(memory_space=pl.ANY)],
        out_specs=[_tail_spec(z.shape[0], row0, d_b),
                   pl.BlockSpec((blk, buf.shape[1]), lambda i: (i, 0))],
        out_shape=[jax.ShapeDtypeStruct(dst.shape, F32), jax.ShapeDtypeStruct(buf.shape, F32)],
        input_output_aliases={5: 0},
        compiler_params=_cparams(("arbitrary",)),
        name="conv_b_sample",
    )(buf, z, z, w, bias, dst)


def _conv_c_prompt_body(gb_ref, gc_ref, hv_ref, w_ref, dst_ref, o_ref, buf_ref, pad_ref):
    del dst_ref
    t = gb_ref.shape[0]
    hist = SUBLANES
    _fill_padded(pad_ref, hist, t, gc_ref[...].astype(F32) * hv_ref[...].astype(F32))
    buf_ref[0, 0] = pad_ref[hist + t - (CONV_C_W - 1):hist + t, :]

    def step(i, carry):
        base = pl.multiple_of(i * CONV_ROWS, CONV_ROWS)
        acc = jnp.zeros((CONV_ROWS, pad_ref.shape[1]), F32)
        acc = _causal_taps(pad_ref, w_ref, base, hist, CONV_C_W, acc)
        o_ref[pl.ds(base, CONV_ROWS), :] = acc * gb_ref[pl.ds(base, CONV_ROWS), :].astype(F32)
        return carry

    lax.fori_loop(0, t // CONV_ROWS, step, 0)


def _conv_c_prompt(z, w, bsz, t, d_c, dst):
    cb = 512
    nb = d_c // cb
    return pl.pallas_call(
        _conv_c_prompt_body,
        grid=(bsz, nb),
        in_specs=[pl.BlockSpec((t, cb), lambda b, j: (b, j)),
                  pl.BlockSpec((t, cb), lambda b, j: (b, nb + j)),
                  pl.BlockSpec((t, cb), lambda b, j: (b, 2 * nb + j)),
                  pl.BlockSpec((CONV_C_W, cb), lambda b, j: (0, j)),
                  pl.BlockSpec(memory_space=pl.ANY)],
        out_specs=[pl.BlockSpec((t, cb), lambda b, j: (b, j)),
                   pl.BlockSpec((1, 1, CONV_C_W - 1, cb), lambda b, j: (0, b, 0, j))],
        out_shape=[jax.ShapeDtypeStruct(dst.shape, F32),
                   jax.ShapeDtypeStruct((1, bsz, CONV_C_W - 1, d_c), F32)],
        input_output_aliases={4: 0},
        scratch_shapes=[pltpu.VMEM((t + 2 * SUBLANES, cb), F32)],
        compiler_params=_cparams(("parallel", "parallel")),
        name="conv_c_prompt",
    )(z, z, z, w, dst)


def _conv_c_sample_body(n_valid, buf_ref, gb_ref, gc_ref, hv_ref, w_ref, dst_ref, o_ref, nbuf_ref):
    del dst_ref
    d_c = gb_ref.shape[1]
    blk = gb_ref.shape[0]
    _zero_tail_rows(o_ref, n_valid)
    u = gc_ref[...].astype(F32) * hv_ref[...].astype(F32)
    c = buf_ref[:, :d_c] * w_ref[0:1, :] + buf_ref[:, d_c:] * w_ref[1:2, :] + u * w_ref[2:3, :]
    o_ref[pl.ds(pl.multiple_of(pl.program_id(0) * blk, blk), blk), :] = gb_ref[...].astype(F32) * c
    nbuf_ref[:, :d_c] = buf_ref[:, d_c:]
    nbuf_ref[:, d_c:] = u


def _conv_c_sample(buf, z, w, row0, d_c, dst):
    ns = buf.shape[0]
    blk = 2 * SAMPLE_BLOCK
    r0 = row0 // blk
    return pl.pallas_call(
        functools.partial(_conv_c_sample_body, ns),
        grid=(ns // blk,),
        in_specs=[pl.BlockSpec((blk, buf.shape[1]), lambda i: (i, 0)),
                  pl.BlockSpec((blk, d_c), lambda i: (r0 + i, 0)),
                  pl.BlockSpec((blk, d_c), lambda i: (r0 + i, 1)),
                  pl.BlockSpec((blk, d_c), lambda i: (r0 + i, 2)),
                  pl.BlockSpec((CONV_C_W, d_c), lambda i: (0, 0)),
                  pl.BlockSpec(memory_space=pl.ANY)],
        out_specs=[_tail_spec(z.shape[0], row0, d_c),
                   pl.BlockSpec((blk, buf.shape[1]), lambda i: (i, 0))],
        out_shape=[jax.ShapeDtypeStruct(dst.shape, F32), jax.ShapeDtypeStruct(buf.shape, F32)],
        input_output_aliases={5: 0},
        compiler_params=_cparams(("arbitrary",)),
        name="conv_c_sample",
    )(buf, z, z, z, w, dst)


def _candidate_tables():
    pairs = [(a, b) for a in range(TOPK) for b in range(TOPK) if (a + 1) * (b + 1) <= TOPK]
    rows = -(-len(pairs) // SUBLANES) * SUBLANES
    p1 = np.zeros((rows, TOPK), np.float32)
    p2 = np.zeros((rows, TOPK), np.float32)
    neg = np.zeros((rows, LANES), np.float32)
    for r, (a, b) in enumerate(pairs):
        p1[r, a] = 1.0
        p2[r, b] = 1.0
    neg[len(pairs):] = -np.inf
    return p1, p2, neg, p1.T.copy()


def _take_top(works, index, count, tie_break):
    rows, lanes = works[0].shape
    slot = lax.broadcasted_iota(jnp.int32, (count, lanes), 0)
    works = list(works)
    vals = [jnp.zeros((count, lanes), F32) for _ in works]
    ranks = [jnp.full((rows, lanes), float(count), F32) for _ in works]
    for a in range(count):
        for i, work in enumerate(works):
            m = jnp.max(work, axis=0, keepdims=True)
            sel = work == m
            if tie_break:
                first = jnp.min(jnp.where(sel, index, float(rows)), axis=0, keepdims=True)
                sel = index == first
            ranks[i] = jnp.where(sel, float(a), ranks[i])
            works[i] = jnp.where(sel, -jnp.inf, work)
            vals[i] = jnp.where(slot == a, m, vals[i])
    return vals, ranks


def _sort_network(n):
    pairs, p = [], 1
    while p < n:
        k = p
        while k >= 1:
            for j in range(k % p, n - k, 2 * k):
                for i in range(min(k, n - j - k)):
                    if (i + j) // (2 * p) == (i + j + k) // (2 * p):
                        pairs.append((i + j, i + j + k))
            k //= 2
        p *= 2
    return pairs


def _top_values(scores, count):
    rows, lanes = scores[0].shape
    n = rows // SUBLANES
    assert n * SUBLANES == rows and count <= n
    cols = [[s[g * SUBLANES:(g + 1) * SUBLANES, :] for g in range(n)] for s in scores]
    for a, b in _sort_network(n):
        for w in cols:
            w[a], w[b] = jnp.maximum(w[a], w[b]), jnp.minimum(w[a], w[b])
    slot = lax.broadcasted_iota(jnp.int32, (count, lanes), 0)
    vals = [jnp.zeros((count, lanes), F32) for _ in scores]
    distinct = [jnp.ones((1, lanes), jnp.bool_) for _ in scores]
    last = [None for _ in scores]
    for r in range(count):
        for i, w in enumerate(cols):
            m = jnp.max(w[0], axis=0, keepdims=True)
            win = w[0] == m
            vals[i] = jnp.where(slot == r, m, vals[i])
            if last[i] is not None:
                distinct[i] = distinct[i] & (m < last[i])
            last[i] = m
            for g in range(count - r - 1):
                w[g] = jnp.where(win, w[g + 1], w[g])
    return vals, distinct


def _count(mask):
    return jnp.sum(mask.astype(F32), axis=0, keepdims=True)


def _route_heads(s1, s2, p1, p2, neg, p1t, key_idx, cand_idx, tie_break):
    n = len(s1)
    both = list(s1) + list(s2)
    if tie_break:
        vals, ranks = _take_top(both, key_idx, TOPK, True)
        rank_is = lambda i, a: ranks[i] == float(a)
        member = lambda i: ranks[i] < float(TOPK)
    else:
        vals, distinct = _top_values(both, TOPK)
        rank_is = lambda i, a: both[i] == vals[i][a:a + 1, :]
        member = lambda i: both[i] >= vals[i][TOPK - 1:TOPK, :]
    v1, v2 = vals[:n], vals[n:]
    cand = [_dot_f32(p1, v1[i], ((1,), (0,))) + _dot_f32(p2, v2[i], ((1,), (0,))) + neg for i in range(n)]
    _, crank = _take_top(cand, cand_idx, TOPK, tie_break)
    out = []
    for i in range(n):
        chosen = crank[i] < float(TOPK)
        cmax = v1[i][0:1, :] + v2[i][0:1, :]
        z = jnp.sum(jnp.where(chosen, jnp.exp(cand[i] - cmax), 0.0), axis=0, keepdims=True)
        width = jnp.dot(p1t, chosen.astype(F32), preferred_element_type=F32)
        b1 = jnp.zeros_like(s1[i])
        r2 = jnp.full(s2[i].shape, float(TOPK), F32)
        for a in range(TOPK):
            b1 = jnp.where(rank_is(i, a), width[a:a + 1, :], b1)
            r2 = jnp.where(rank_is(n + i, a), float(a), r2)
        in1, in2 = member(i), member(n + i)
        c1 = jnp.where(in1, jnp.exp(jnp.where(in1, s1[i] - v1[i][0:1, :], 0.0)) / z, 0.0)
        e2 = jnp.where(in2, jnp.exp(jnp.where(in2, s2[i] - v2[i][0:1, :], 0.0)), 0.0)
        clean = ((_count(in1) == float(TOPK)) & (_count(in2) == float(TOPK))
                 & (_count(chosen) == float(TOPK)))
        if not tie_break:
            clean = clean & distinct[i] & distinct[n + i]
        out.append((c1, b1, r2, e2, clean))
    return out


ROUTE_HEADS = 4


def _route_body(q_ref, keys_ref, p1_ref, p2_ref, neg_ref, p1t_ref, u_ref, v_ref,
                r2_ref, e2_ref, c1_ref, b1_ref, ub_ref, vt_ref, q3_ref):
    ub_ref[...] = u_ref[0].astype(BF16)
    vt_ref[...] = v_ref[0].T.astype(BF16)

    for hp in range(2 * P_HEADS):
        q3_ref[hp] = q_ref[:, hp * LANES:(hp + 1) * LANES]
    key_idx = lax.broadcasted_iota(jnp.int32, (N_KEYS, LANES), 0).astype(F32)
    cand_idx = lax.broadcasted_iota(jnp.int32, (p1_ref.shape[0], LANES), 0).astype(F32)

    def emit(heads, tie_break):
        s1 = [_dot_f32(keys_ref[h, 0], q3_ref[2 * h], ((1,), (1,))) for h in heads]
        s2 = [_dot_f32(keys_ref[h, 1], q3_ref[2 * h + 1], ((1,), (1,))) for h in heads]
        rows = _route_heads(s1, s2, p1_ref[...], p2_ref[...], neg_ref[...], p1t_ref[...],
                            key_idx, cand_idx, tie_break)
        all_clean = None
        for h, (c1, b1, r2, e2, clean) in zip(heads, rows):
            c1_ref[h] = c1
            b1_ref[h] = b1
            r2_ref[h] = r2.astype(r2_ref.dtype)
            e2_ref[h] = e2.astype(e2_ref.dtype)
            all_clean = clean if all_clean is None else all_clean & clean
        return all_clean

    def step(i, carry):
        heads = [i * ROUTE_HEADS + j for j in range(ROUTE_HEADS)]
        all_clean = emit(heads, False)

        @pl.when(jnp.min(all_clean.astype(F32)) < 0.5)
        def _():
            emit(heads, True)

        return carry

    lax.fori_loop(0, P_HEADS // ROUTE_HEADS, step, 0)


def _route(q, keys, table_u, table_v, layer):
    n = q.shape[0]
    _, n_exp, d = table_u.shape
    steps = n // LANES
    n_slabs = min(n_exp // TABLE_SLAB, 1 << (steps.bit_length() - 1))
    slab = n_exp // n_slabs
    assert n_slabs * slab == n_exp
    p1, p2, neg, p1t = _candidate_tables()
    whole = lambda a: pl.BlockSpec(a.shape, lambda i: (0,) * a.ndim)
    out_spec = pl.BlockSpec((P_HEADS, N_KEYS, LANES), lambda i: (0, 0, i))
    shape = lambda dt: jax.ShapeDtypeStruct((P_HEADS, N_KEYS, n), dt)
    slab_of = lambda i: jnp.minimum(i, n_slabs - 1)
    table_spec = pl.BlockSpec((1, slab, d), lambda i: (layer, slab_of(i), 0))
    return pl.pallas_call(
        _route_body,
        grid=(steps,),
        in_specs=[pl.BlockSpec((LANES, q.shape[1]), lambda i: (i, 0)), whole(keys),
                  whole(p1), whole(p2), whole(neg), whole(p1t), table_spec, table_spec],
        out_specs=[out_spec] * 4 + [pl.BlockSpec((slab, d), lambda i: (slab_of(i), 0)),
                                    pl.BlockSpec((d, slab), lambda i: (0, slab_of(i)))],
        out_shape=[shape(BF16), shape(BF16), shape(F32), shape(F32),
                   jax.ShapeDtypeStruct((n_exp, d), BF16), jax.ShapeDtypeStruct((d, n_exp), BF16)],
        scratch_shapes=[pltpu.VMEM((2 * P_HEADS, LANES, LANES), F32)],
        compiler_params=_cparams(("arbitrary",)),
        name="peer_route",
    )(q, keys, jnp.asarray(p1), jnp.asarray(p2), jnp.asarray(neg), jnp.asarray(p1t), table_u, table_v)


def _bf16_pair(c):
    hi = float(np.asarray(c, dtype=BF16))
    return hi, float(np.asarray(c - hi, dtype=BF16))


def _gelu_tanh(x):
    c0 = float(np.sqrt(2.0 / np.pi))
    c0_hi, c0_lo = _bf16_pair(c0)
    c1_hi, c1_lo = _bf16_pair(c0 * 0.044715)
    x2 = x * x
    inner = x * ((c1_hi * x2 + c0_hi) + (c1_lo * x2 + c0_lo))
    half = 0.5 * x
    return half + half * jnp.tanh(inner)


def _peer_dense_body(split, xt_ref, u_ref, vt_ref, r2_ref, e2_ref, c1_ref, b1_ref, x_ref, gf_ref, *rest):
    if split is None:
        o_ref, acc_ref = rest
    else:
        o_ref, ys_ref, acc_ref = rest
    e = pl.program_id(1)
    n_groups = xt_ref.shape[1] // PEER_LANES

    @pl.when(e == 0)
    def _():
        acc_ref[...] = jnp.zeros_like(acc_ref)

    def scores(j):
        cols = slice(j * PEER_LANES, (j + 1) * PEER_LANES)
        return jnp.dot(u_ref[...], xt_ref[:, cols], preferred_element_type=F32)

    def gates(j):
        cols = slice(j * PEER_LANES, (j + 1) * PEER_LANES)
        parts = []
        for r in range(EXPERT_BLOCK // N_KEYS):
            gate = jnp.zeros((N_KEYS, PEER_LANES), BF16)
            for h in range(P_HEADS):
                hit = r2_ref[h, :, cols] < b1_ref[h, r:r + 1, cols].astype(BF16)
                gate = gate + c1_ref[h, r:r + 1, cols].astype(BF16) * jnp.where(
                    hit, e2_ref[h, :, cols], jnp.zeros((), BF16))
            parts.append(gate)
        return jnp.concatenate(parts, axis=0)

    s_next = scores(0)
    for j in range(n_groups):
        s = s_next
        if j + 1 < n_groups:
            s_next = scores(j + 1)
        cols = slice(j * PEER_LANES, (j + 1) * PEER_LANES)
        weighted = gates(j) * _gelu_tanh(s.astype(BF16))
        acc_ref[:, cols] += jnp.dot(vt_ref[...], weighted, preferred_element_type=F32)

    @pl.when(e == pl.num_programs(1) - 1)
    def _():
        for j in range(n_groups):
            rows = slice(j * PEER_LANES, (j + 1) * PEER_LANES)
            out = x_ref[rows, :] + acc_ref[:, rows].T
            if split is None:
                o_ref[rows, :] = out
                continue
            y = _rmsnorm(out, gf_ref[...])
            o_ref[rows, :] = y
            if j == split.group:
                @pl.when(pl.program_id(0) == split.block)
                def _():
                    ys_ref[...] = y[:ys_ref.shape[0], :]


class _SampleSplit(NamedTuple):
    block: int
    group: int


def _peer_dense(xt, u_b, vt_b, r2, e2, c1, b1, x, g_final, n_prompt=None, n_sample=None):
    d, n = xt.shape
    n_exp = u_b.shape[0]
    keys_per_block = EXPERT_BLOCK // N_KEYS
    once = pl.Buffered(1)
    rank_spec = pl.BlockSpec((P_HEADS, N_KEYS, TOK_BLOCK), lambda i, e: (0, 0, i), pipeline_mode=once)
    first_spec = pl.BlockSpec((P_HEADS, keys_per_block, TOK_BLOCK), lambda i, e: (0, e, i))
    row_spec = pl.BlockSpec((TOK_BLOCK, d), lambda i, e: (i, 0), pipeline_mode=once)
    if n_prompt is None:
        split = None
        out_specs = row_spec
        out_shape = jax.ShapeDtypeStruct((n, d), F32)
    else:
        start = n_prompt % TOK_BLOCK
        assert start % PEER_LANES == 0 and n_sample <= PEER_LANES and n_prompt + n_sample <= n
        split = _SampleSplit(n_prompt // TOK_BLOCK, start // PEER_LANES)
        out_specs = [row_spec, pl.BlockSpec((n_sample, d), lambda i, e: (0, 0))]
        out_shape = [jax.ShapeDtypeStruct((n_prompt, d), F32), jax.ShapeDtypeStruct((n_sample, d), F32)]
    return pl.pallas_call(
        functools.partial(_peer_dense_body, split),
        grid=(n // TOK_BLOCK, n_exp // EXPERT_BLOCK),
        in_specs=[pl.BlockSpec((d, TOK_BLOCK), lambda i, e: (0, i), pipeline_mode=once),
                  pl.BlockSpec((EXPERT_BLOCK, d), lambda i, e: (e, 0)),
                  pl.BlockSpec((d, EXPERT_BLOCK), lambda i, e: (0, e)),
                  rank_spec, rank_spec, first_spec, first_spec, row_spec,
                  pl.BlockSpec((1, d), lambda i, e: (0, 0))],
        out_specs=out_specs,
        out_shape=out_shape,
        scratch_shapes=[pltpu.VMEM((d, TOK_BLOCK), F32)],
        compiler_params=_cparams(("arbitrary", "arbitrary")),
        name="peer_dense" if split is None else "peer_dense_final",
    )(xt, u_b, vt_b, r2, e2, c1, b1, x, g_final)


def _peer(x, g_norm, wq, keys, table_u, table_v, layer, g_final, n_prompt=None, n_sample=None):
    q, xt = _norm_matmul_t(x, g_norm, wq, wq.shape[1])
    r2, e2, c1, b1, u_b, vt_b = _route(q, keys, table_u, table_v, layer)
    return _peer_dense(xt, u_b, vt_b, r2, e2, c1, b1, x, g_final, n_prompt, n_sample)


def _col_form(a, n_dk):
    ns = a.shape[0]
    a = a.reshape(ns // SAMPLE_BLOCK, SAMPLE_BLOCK, H_A, n_dk)
    return a.transpose(0, 2, 3, 1)


def kernel(x_prompt, x_sample, state_gla, state_conv_b, state_conv_c, norm_mix, norm_ffn, norm_final, w_in_even, w_a2, b_a2, gla_norm_g, conv_b_w, conv_b_bias, conv_b_ln_g, conv_b_ln_b, w_out_even, w_in_odd, conv_c_w, w_out_odd, peer_wq, peer_keys, peer_u, peer_v):
    bsz, t, d = x_prompt.shape
    ns = x_sample.shape[0]
    n_prompt = bsz * t
    n_real = n_prompt + ns
    n_tok = -(-n_real // TOK_BLOCK) * TOK_BLOCK
    assert t % GLA_CHUNK == 0 and t % CONV_ROWS == 0 and ns % (2 * SAMPLE_BLOCK) == 0
    assert x_sample.shape[1] == 1 and n_prompt % (2 * SAMPLE_BLOCK) == 0

    d_a = d // 2
    dk_a = d_a // 2
    d_b = d - d_a
    rank = w_a2.shape[1]
    col_a = 2 * dk_a + 2 * d_a
    row = lambda a: a.reshape(1, -1)

    x = jnp.concatenate([x_prompt.reshape(n_prompt, d), x_sample.reshape(ns, d),
                         jnp.zeros((n_tok - n_real, d), F32)], axis=0)

    w_in = w_in_even[0]
    w_main = jnp.concatenate([w_in[:, :col_a], w_in[:, col_a + rank:]], axis=1).astype(BF16)
    w_gate = jnp.pad(w_in[:, col_a:col_a + rank], ((0, 0), (0, LANES - rank))).astype(BF16)
    w_a2p = jnp.pad(w_a2[0], ((0, LANES - rank), (0, 0)))
    z, la = _norm_matmul_gate(x, row(norm_mix[0]), w_main, w_gate, w_a2p, row(b_a2[0]), 1024)

    gn = row(gla_norm_g[0])
    zs = z[n_prompt:n_real].astype(F32)
    las = la[n_prompt:n_real]
    oa, gla_s = _gla_sample(state_gla, _col_form(las, dk_a // H_A), _col_form(zs[:, dk_a:2 * dk_a], dk_a // H_A),
                            _col_form(zs[:, :dk_a], dk_a // H_A),
                            zs[:, 2 * dk_a:2 * dk_a + d_a], zs[:, 2 * dk_a + d_a:col_a], gn,
                            jnp.zeros((n_tok, d_a), F32), n_prompt)
    oa, gla_p = _gla_prompt(z, la, gn, bsz, t, oa)

    ub, cb_s = _conv_b_sample(state_conv_b[0].reshape(ns, -1), z, conv_b_w[0], row(conv_b_bias[0]),
                              n_prompt, col_a, d_b, jnp.zeros((n_tok, d_b), F32))
    ub, cb_p = _conv_b_prompt(z, conv_b_w[0], row(conv_b_bias[0]), bsz, t, col_a, d_b, ub)
    x = _res_matmul_ln(oa, ub, row(conv_b_ln_g[0]), row(conv_b_ln_b[0]), w_out_even[0].astype(BF16), x, d)
    wq_b = peer_wq.astype(BF16)
    x = _peer(x, row(norm_ffn[0]), wq_b[0], peer_keys[0], peer_u, peer_v, 0, row(norm_final))

    z = _norm_matmul(x, row(norm_mix[1]), w_in_odd[0].astype(BF16), 1024)
    gc, cc_s = _conv_c_sample(state_conv_c[0].reshape(ns, -1), z, conv_c_w[0], n_prompt, d,
                              jnp.zeros((n_tok, d), F32))
    gc, cc_p = _conv_c_prompt(z, conv_c_w[0], bsz, t, d, gc)
    x = _res_matmul(gc, w_out_odd[0].astype(BF16), x, d)
    y_prompt, y_sample = _peer(x, row(norm_ffn[1]), wq_b[1], peer_keys[1], peer_u, peer_v, 1, row(norm_final),
                               n_prompt, ns)
    return (y_prompt.reshape(bsz, t, d), y_sample.reshape(ns, 1, d), gla_p, cb_p, cc_p,
            gla_s, cb_s.reshape(state_conv_b.shape), cc_s.reshape(state_conv_c.shape))
```

```python
import functools
from typing import NamedTuple

import numpy as np
import jax
import jax.numpy as jnp
from jax import lax
from jax.experimental import pallas as pl
from jax.experimental.pallas import tpu as pltpu

F32 = jnp.float32
BF16 = jnp.bfloat16
HIGHEST = lax.Precision.HIGHEST

EPS = 1e-6
GATE_TAU = 16.0
H_A = 4
P_HEADS = 8
N_KEYS = 128
TOPK = 16
CONV_B_W = 31
CONV_C_W = 3

LANES = 128
SUBLANES = 8
VMEM_LIMIT = 56 * 1024 * 1024

TOK_BLOCK = 768
EXPERT_BLOCK = 1024
PEER_LANES = 256
TABLE_SLAB = 256
GLA_CHUNK = 128
GLA_SUB = 16
EXP_CLAMP = 80.0
CONV_ROWS = 64
SAMPLE_BLOCK = 8


def _cparams(sem):
    return pltpu.CompilerParams(dimension_semantics=sem, vmem_limit_bytes=VMEM_LIMIT)


def _rmsnorm(x, g):
    return x * lax.rsqrt(jnp.mean(x * x, axis=-1, keepdims=True) + EPS) * g


def _sigmoid(x):
    return jax.nn.sigmoid(x)


def _weight_spec(k, m, tn):
    if tn == m:
        return pl.BlockSpec((k, tn), lambda i, j: (0, 0), pipeline_mode=pl.Buffered(1))
    return pl.BlockSpec((k, tn), lambda i, j: (0, j))


def _log_sigmoid(x):
    return jnp.minimum(x, 0.0) - jnp.log(1.0 + jnp.exp(-jnp.abs(x)))


def _norm_matmul_body(x_ref, g_ref, w_ref, o_ref, xn_ref):
    @pl.when(pl.program_id(1) == 0)
    def _():
        xn_ref[...] = _rmsnorm(x_ref[...], g_ref[...]).astype(BF16)

    o_ref[...] = jnp.dot(xn_ref[...], w_ref[...], preferred_element_type=F32).astype(o_ref.dtype)


def _norm_matmul(x, g, w, tn):
    n, d = x.shape
    m = w.shape[1]
    return pl.pallas_call(
        _norm_matmul_body,
        grid=(n // TOK_BLOCK, m // tn),
        in_specs=[pl.BlockSpec((TOK_BLOCK, d), lambda i, j: (i, 0)),
                  pl.BlockSpec((1, d), lambda i, j: (0, 0)),
                  pl.BlockSpec((d, tn), lambda i, j: (0, j))],
        out_specs=pl.BlockSpec((TOK_BLOCK, tn), lambda i, j: (i, j)),
        out_shape=jax.ShapeDtypeStruct((n, m), BF16),
        scratch_shapes=[pltpu.VMEM((TOK_BLOCK, d), BF16)],
        compiler_params=_cparams(("parallel", "arbitrary")),
        name="norm_matmul",
    )(x, g, w)


def _norm_matmul_gate_body(x_ref, g_ref, w_ref, wa_ref, wa2_ref, ba2_ref, o_ref, la_ref, xn_ref):
    @pl.when(pl.program_id(1) == 0)
    def _():
        xn = _rmsnorm(x_ref[...], g_ref[...]).astype(BF16)
        xn_ref[...] = xn
        a_lr = jnp.dot(xn, wa_ref[...], preferred_element_type=F32)
        pre = jnp.dot(a_lr, wa2_ref[...], precision=HIGHEST, preferred_element_type=F32) + ba2_ref[...]
        la_ref[...] = _log_sigmoid(pre) / GATE_TAU

    o_ref[...] = jnp.dot(xn_ref[...], w_ref[...], preferred_element_type=F32).astype(o_ref.dtype)


def _norm_matmul_gate(x, g, w, wa, wa2, ba2, tn):
    n, d = x.shape
    m = w.shape[1]
    dk = wa2.shape[1]
    return pl.pallas_call(
        _norm_matmul_gate_body,
        grid=(n // TOK_BLOCK, m // tn),
        in_specs=[pl.BlockSpec((TOK_BLOCK, d), lambda i, j: (i, 0)),
                  pl.BlockSpec((1, d), lambda i, j: (0, 0)),
                  pl.BlockSpec((d, tn), lambda i, j: (0, j)),
                  pl.BlockSpec((d, LANES), lambda i, j: (0, 0)),
                  pl.BlockSpec((LANES, dk), lambda i, j: (0, 0)),
                  pl.BlockSpec((1, dk), lambda i, j: (0, 0))],
        out_specs=[pl.BlockSpec((TOK_BLOCK, tn), lambda i, j: (i, j)),
                   pl.BlockSpec((TOK_BLOCK, dk), lambda i, j: (i, 0))],
        out_shape=[jax.ShapeDtypeStruct((n, m), BF16), jax.ShapeDtypeStruct((n, dk), F32)],
        scratch_shapes=[pltpu.VMEM((TOK_BLOCK, d), BF16)],
        compiler_params=_cparams(("parallel", "arbitrary")),
        name="norm_matmul_gate",
    )(x, g, w, wa, wa2, ba2)


def _norm_matmul_t_body(x_ref, g_ref, w_ref, o_ref, xt_ref, xn_ref):
    @pl.when(pl.program_id(1) == 0)
    def _():
        xn = _rmsnorm(x_ref[...], g_ref[...])
        xn_ref[...] = xn.astype(BF16)
        xt_ref[...] = xn.T.astype(BF16)

    o_ref[...] = jnp.dot(xn_ref[...], w_ref[...], preferred_element_type=F32)


def _norm_matmul_t(x, g, w, tn):
    n, d = x.shape
    m = w.shape[1]
    return pl.pallas_call(
        _norm_matmul_t_body,
        grid=(n // TOK_BLOCK, m // tn),
        in_specs=[pl.BlockSpec((TOK_BLOCK, d), lambda i, j: (i, 0)),
                  pl.BlockSpec((1, d), lambda i, j: (0, 0)),
                  _weight_spec(d, m, tn)],
        out_specs=[pl.BlockSpec((TOK_BLOCK, tn), lambda i, j: (i, j)),
                   pl.BlockSpec((d, TOK_BLOCK), lambda i, j: (0, i))],
        out_shape=[jax.ShapeDtypeStruct((n, m), F32), jax.ShapeDtypeStruct((d, n), BF16)],
        scratch_shapes=[pltpu.VMEM((TOK_BLOCK, d), BF16)],
        compiler_params=_cparams(("parallel", "arbitrary")),
        name="norm_matmul_t",
    )(x, g, w)


def _res_matmul_body(a_ref, w_ref, x_ref, o_ref, ab_ref):
    @pl.when(pl.program_id(1) == 0)
    def _():
        ab_ref[...] = a_ref[...].astype(BF16)

    o_ref[...] = x_ref[...] + jnp.dot(ab_ref[...], w_ref[...], preferred_element_type=F32)


def _res_matmul(a, w, x, tn):
    n, k = a.shape
    m = w.shape[1]
    return pl.pallas_call(
        _res_matmul_body,
        grid=(n // TOK_BLOCK, m // tn),
        in_specs=[pl.BlockSpec((TOK_BLOCK, k), lambda i, j: (i, 0)),
                  _weight_spec(k, m, tn),
                  pl.BlockSpec((TOK_BLOCK, tn), lambda i, j: (i, j))],
        out_specs=pl.BlockSpec((TOK_BLOCK, tn), lambda i, j: (i, j)),
        out_shape=jax.ShapeDtypeStruct((n, m), F32),
        scratch_shapes=[pltpu.VMEM((TOK_BLOCK, k), BF16)],
        compiler_params=_cparams(("parallel", "arbitrary")),
        name="res_matmul",
    )(a, w, x)


def _res_matmul_ln_body(oa_ref, ub_ref, lg_ref, lb_ref, w_ref, x_ref, o_ref, ab_ref):
    da = oa_ref.shape[1]

    @pl.when(pl.program_id(1) == 0)
    def _():
        u = ub_ref[...]
        uc = u - jnp.mean(u, axis=-1, keepdims=True)
        y = uc * lax.rsqrt(jnp.mean(uc * uc, axis=-1, keepdims=True) + EPS) * lg_ref[...] + lb_ref[...]
        ab_ref[:, :da] = oa_ref[...].astype(BF16)
        ab_ref[:, da:] = (y * _sigmoid(y)).astype(BF16)

    o_ref[...] = x_ref[...] + jnp.dot(ab_ref[...], w_ref[...], preferred_element_type=F32)


def _res_matmul_ln(oa, ub, lg, lb, w, x, tn):
    n, da = oa.shape
    db = ub.shape[1]
    m = w.shape[1]
    return pl.pallas_call(
        _res_matmul_ln_body,
        grid=(n // TOK_BLOCK, m // tn),
        in_specs=[pl.BlockSpec((TOK_BLOCK, da), lambda i, j: (i, 0)),
                  pl.BlockSpec((TOK_BLOCK, db), lambda i, j: (i, 0)),
                  pl.BlockSpec((1, db), lambda i, j: (0, 0)),
                  pl.BlockSpec((1, db), lambda i, j: (0, 0)),
                  _weight_spec(da + db, m, tn),
                  pl.BlockSpec((TOK_BLOCK, tn), lambda i, j: (i, j))],
        out_specs=pl.BlockSpec((TOK_BLOCK, tn), lambda i, j: (i, j)),
        out_shape=jax.ShapeDtypeStruct((n, m), F32),
        scratch_shapes=[pltpu.VMEM((TOK_BLOCK, da + db), BF16)],
        compiler_params=_cparams(("parallel", "arbitrary")),
        name="res_matmul_ln",
    )(oa, ub, lg, lb, w, x)


def _dot_f32(a, b, dims):
    return lax.dot_general(a, b, (dims, ((), ())), precision=HIGHEST, preferred_element_type=F32)


def _dot_bf16(a, b, dims):
    return lax.dot_general(a.astype(BF16), b.astype(BF16), (dims, ((), ())), preferred_element_type=F32)


def _head_out(o, gn, g):
    y = o * lax.rsqrt(jnp.mean(o * o, axis=-1, keepdims=True) + EPS) * gn
    return y * (g * _sigmoid(g))


def _gla_prompt_body(q_ref, k_ref, v_ref, g_ref, la_ref, gn_ref, dst_ref, o_ref, s_out_ref, s_ref, acc_ref):
    del dst_ref
    c = pl.program_id(1)
    dk = GLA_CHUNK
    dv = s_ref.shape[2]

    @pl.when(c == 0)
    def _():
        s_ref[...] = jnp.zeros_like(s_ref)

    row = lax.broadcasted_iota(jnp.int32, (GLA_CHUNK, GLA_CHUNK), 0)
    col = lax.broadcasted_iota(jnp.int32, (GLA_CHUNK, GLA_CHUNK), 1)
    tri = (row >= col).astype(F32)
    sub_row = lax.broadcasted_iota(jnp.int32, (GLA_SUB, GLA_SUB), 0)
    sub_col = lax.broadcasted_iota(jnp.int32, (GLA_SUB, GLA_SUB), 1)
    causal = sub_row >= sub_col

    for h in range(H_A):
        kcols = slice(h * dk, (h + 1) * dk)
        vcols = slice(h * dv, (h + 1) * dv)
        la = la_ref[:, kcols]
        b = _dot_f32(tri, la, ((1,), (0,)))
        b_last = b[GLA_CHUNK - 1:GLA_CHUNK, :]
        q = q_ref[:, kcols].astype(F32) * (dk ** -0.5)
        k = k_ref[:, kcols].astype(F32)
        v_b = v_ref[:, vcols].astype(BF16)
        s = s_ref[h]

        acc_ref[h] = _dot_bf16(q * jnp.exp(b), s, ((1,), (0,)))
        for j in range(GLA_CHUNK // GLA_SUB):
            lo, hi = j * GLA_SUB, (j + 1) * GLA_SUB
            b_j = b[lo:hi, :]
            b_first = b[lo:lo + 1, :]
            b_end = b[hi - 1:hi, :]
            q_d = q[lo:hi, :] * jnp.exp(b_j - b_first)
            k_d = k[lo:hi, :] * jnp.exp(jnp.minimum(b_first - b_j, EXP_CLAMP))
            sc = jnp.where(causal, _dot_bf16(q_d, k_d, ((1,), (1,))), 0.0)
            acc_ref[h, lo:hi, :] += _dot_bf16(sc, v_b[lo:hi, :], ((1,), (0,)))
            if hi < GLA_CHUNK:
                k_p = k[lo:hi, :] * jnp.exp(b_end - b_j)
                q_p = q[hi:, :] * jnp.exp(b[hi:, :] - b_end)
                sc = _dot_bf16(q_p, k_p, ((1,), (1,)))
                acc_ref[h, hi:, :] += _dot_bf16(sc, v_b[lo:hi, :], ((1,), (0,)))

        decay = jnp.exp(jnp.broadcast_to(b_last, (dk, dk))).T
        k_e = k * jnp.exp(b_last - b)
        s_new = jnp.concatenate([decay] * (dv // dk), axis=1) * s + _dot_bf16(k_e.T, v_b, ((1,), (0,)))
        s_ref[h] = s_new
        o_ref[:, vcols] = _head_out(acc_ref[h], gn_ref[:, vcols], g_ref[:, vcols].astype(F32))

    @pl.when(c == pl.num_programs(1) - 1)
    def _():
        s_out_ref[0, 0] = s_ref[...]


def _gla_prompt(z, la, gn, bsz, t, dst):
    dk, dv = GLA_CHUNK, 2 * GLA_CHUNK
    d_k, d_v = H_A * dk, H_A * dv
    nc = t // GLA_CHUNK
    rows = lambda b, c: b * nc + c
    return pl.pallas_call(
        _gla_prompt_body,
        grid=(bsz, nc),
        in_specs=[pl.BlockSpec((GLA_CHUNK, d_k), lambda b, c: (rows(b, c), 0)),
                  pl.BlockSpec((GLA_CHUNK, d_k), lambda b, c: (rows(b, c), 1)),
                  pl.BlockSpec((GLA_CHUNK, d_v), lambda b, c: (rows(b, c), 2 * d_k // d_v)),
                  pl.BlockSpec((GLA_CHUNK, d_v), lambda b, c: (rows(b, c), 2 * d_k // d_v + 1)),
                  pl.BlockSpec((GLA_CHUNK, d_k), lambda b, c: (rows(b, c), 0)),
                  pl.BlockSpec((1, d_v), lambda b, c: (0, 0)),
                  pl.BlockSpec(memory_space=pl.ANY)],
        out_specs=[pl.BlockSpec((GLA_CHUNK, d_v), lambda b, c: (rows(b, c), 0)),
                   pl.BlockSpec((1, 1, H_A, dk, dv), lambda b, c: (0, b, 0, 0, 0))],
        out_shape=[jax.ShapeDtypeStruct(dst.shape, F32),
                   jax.ShapeDtypeStruct((1, bsz, H_A, dk, dv), F32)],
        input_output_aliases={6: 0},
        scratch_shapes=[pltpu.VMEM((H_A, dk, dv), F32), pltpu.VMEM((H_A, GLA_CHUNK, dv), F32)],
        compiler_params=_cparams(("parallel", "arbitrary")),
        name="gla_prompt",
    )(z, z, z, z, la, gn, dst)


def _tail_spec(n_rows, row0, cols):
    tail = n_rows - row0
    assert row0 % tail == 0 and tail % SUBLANES == 0
    return pl.BlockSpec((tail, cols), lambda i: (row0 // tail, 0))


def _zero_tail_rows(o_ref, n_valid):
    @pl.when(pl.program_id(0) == 0)
    def _():
        o_ref[n_valid:, :] = jnp.zeros((o_ref.shape[0] - n_valid, o_ref.shape[1]), F32)


def _gla_sample_body(n_valid, st_ref, lat_ref, kt_ref, qt_ref, v_ref, g_ref, gn_ref, dst_ref, o_ref, s_out_ref,
                     rows_ref):
    del dst_ref
    dv = st_ref.shape[-1]
    _zero_tail_rows(o_ref, n_valid)
    for i in range(SAMPLE_BLOCK):
        for h in range(H_A):
            decay = jnp.exp(lat_ref[0, h, :, i:i + 1])
            v_row = v_ref[i:i + 1, h * dv:(h + 1) * dv]
            s_new = decay * st_ref[0, i, h] + kt_ref[0, h, :, i:i + 1] * v_row
            s_out_ref[0, i, h] = s_new
            q_col = qt_ref[0, h, :, i:i + 1] * (st_ref.shape[-2] ** -0.5)
            o = jnp.sum(q_col * s_new, axis=0, keepdims=True)
            rows_ref[i:i + 1, h * dv:(h + 1) * dv] = _head_out(
                o, gn_ref[:, h * dv:(h + 1) * dv], g_ref[i:i + 1, h * dv:(h + 1) * dv])
    base = pl.multiple_of(pl.program_id(0) * SAMPLE_BLOCK, SAMPLE_BLOCK)
    o_ref[pl.ds(base, SAMPLE_BLOCK), :] = rows_ref[...]


def _gla_sample(state, lat, kt, qt, v, g, gn, dst, row0):
    _, ns, _, dk, dv = state.shape
    n_rows = dst.shape[0]
    steps = ns // SAMPLE_BLOCK
    col_spec = pl.BlockSpec((1, H_A, dk, SAMPLE_BLOCK), lambda i: (i, 0, 0, 0))
    row_spec = pl.BlockSpec((SAMPLE_BLOCK, H_A * dv), lambda i: (i, 0))
    st_spec = pl.BlockSpec((1, SAMPLE_BLOCK, H_A, dk, dv), lambda i: (0, i, 0, 0, 0))
    return pl.pallas_call(
        functools.partial(_gla_sample_body, ns),
        grid=(steps,),
        in_specs=[st_spec, col_spec, col_spec, col_spec, row_spec, row_spec,
                  pl.BlockSpec((1, H_A * dv), lambda i: (0, 0)),
                  pl.BlockSpec(memory_space=pl.ANY)],
        out_specs=[_tail_spec(n_rows, row0, H_A * dv), st_spec],
        out_shape=[jax.ShapeDtypeStruct(dst.shape, F32), jax.ShapeDtypeStruct(state.shape, F32)],
        input_output_aliases={7: 0},
        scratch_shapes=[pltpu.VMEM((SAMPLE_BLOCK, H_A * dv), F32)],
        compiler_params=_cparams(("arbitrary",)),
        name="gla_sample",
    )(state, lat, kt, qt, v, g, gn, dst)


def _fill_padded(pad_ref, hist, t, value):
    cols = pad_ref.shape[1]
    pad_ref[0:hist, :] = jnp.zeros((hist, cols), F32)
    pad_ref[hist:hist + t, :] = value
    pad_ref[hist + t:, :] = jnp.zeros((SUBLANES, cols), F32)


def _causal_taps(pad_ref, w_ref, base, hist, n_taps, acc):
    first = hist - (n_taps - 1)
    win = pad_ref[pl.ds(base, CONV_ROWS + hist + SUBLANES), :]
    for r in range(SUBLANES):
        taps = [w for w in range(n_taps) if (first + w) % SUBLANES == r]
        if not taps:
            continue
        shifted = win if r == 0 else pltpu.roll(win, win.shape[0] - r, axis=0)
        for w in taps:
            lo = (first + w) // SUBLANES * SUBLANES
            acc = acc + shifted[lo:lo + CONV_ROWS, :] * w_ref[w:w + 1, :]
    return acc


def _conv_b_prompt_body(a_ref, b_ref, w_ref, bias_ref, dst_ref, o_ref, buf_ref, pad_ref):
    del dst_ref
    t = a_ref.shape[0]
    hist = 4 * SUBLANES
    _fill_padded(pad_ref, hist, t, a_ref[...].astype(F32) * _sigmoid(b_ref[...].astype(F32)))
    buf_ref[0, 0] = pad_ref[hist + t - (CONV_B_W - 1):hist + t, :]

    def step(i, carry):
        base = pl.multiple_of(i * CONV_ROWS, CONV_ROWS)
        acc = jnp.zeros((CONV_ROWS, pad_ref.shape[1]), F32) + bias_ref[...]
        o_ref[pl.ds(base, CONV_ROWS), :] = _causal_taps(pad_ref, w_ref, base, hist, CONV_B_W, acc)
        return carry

    lax.fori_loop(0, t // CONV_ROWS, step, 0)


def _conv_b_prompt(z, w, bias, bsz, t, col0, d_b, dst):
    cb = 256
    nb = d_b // cb
    a0, b0 = col0 // cb, (col0 + d_b) // cb
    return pl.pallas_call(
        _conv_b_prompt_body,
        grid=(bsz, nb),
        in_specs=[pl.BlockSpec((t, cb), lambda b, j: (b, a0 + j)),
                  pl.BlockSpec((t, cb), lambda b, j: (b, b0 + j)),
                  pl.BlockSpec((CONV_B_W, cb), lambda b, j: (0, j)),
                  pl.BlockSpec((1, cb), lambda b, j: (0, j)),
                  pl.BlockSpec(memory_space=pl.ANY)],
        out_specs=[pl.BlockSpec((t, cb), lambda b, j: (b, j)),
                   pl.BlockSpec((1, 1, CONV_B_W - 1, cb), lambda b, j: (0, b, 0, j))],
        out_shape=[jax.ShapeDtypeStruct(dst.shape, F32),
                   jax.ShapeDtypeStruct((1, bsz, CONV_B_W - 1, d_b), F32)],
        input_output_aliases={4: 0},
        scratch_shapes=[pltpu.VMEM((t + 5 * SUBLANES, cb), F32)],
        compiler_params=_cparams(("parallel", "parallel")),
        name="conv_b_prompt",
    )(z, z, w, bias, dst)


def _conv_b_sample_body(n_valid, buf_ref, a_ref, b_ref, w_ref, bias_ref, dst_ref, o_ref, nbuf_ref):
    del dst_ref
    d_b = a_ref.shape[1]
    blk = a_ref.shape[0]
    _zero_tail_rows(o_ref, n_valid)
    glu = a_ref[...].astype(F32) * _sigmoid(b_ref[...].astype(F32))
    acc = bias_ref[...] + glu * w_ref[CONV_B_W - 1:CONV_B_W, :]
    for w in range(CONV_B_W - 1):
        acc = acc + buf_ref[:, w * d_b:(w + 1) * d_b] * w_ref[w:w + 1, :]
    o_ref[pl.ds(pl.multiple_of(pl.program_id(0) * blk, blk), blk), :] = acc
    nbuf_ref[:, :(CONV_B_W - 2) * d_b] = buf_ref[:, d_b:]
    nbuf_ref[:, (CONV_B_W - 2) * d_b:] = glu


def _conv_b_sample(buf, z, w, bias, row0, col0, d_b, dst):
    ns = buf.shape[0]
    blk = 2 * SAMPLE_BLOCK
    r0 = row0 // blk
    return pl.pallas_call(
        functools.partial(_conv_b_sample_body, ns),
        grid=(ns // blk,),
        in_specs=[pl.BlockSpec((blk, buf.shape[1]), lambda i: (i, 0)),
                  pl.BlockSpec((blk, d_b), lambda i: (r0 + i, col0 // d_b)),
                  pl.BlockSpec((blk, d_b), lambda i: (r0 + i, col0 // d_b + 1)),
                  pl.BlockSpec((CONV_B_W, d_b), lambda i: (0, 0)),
                  pl.BlockSpec((1, d_b), lambda i: (0, 0)),
                  pl.BlockSpec(memory_space=pl.ANY)],
        out_specs=[_tail_spec(z.shape[0], row0, d_b),
                   pl.BlockSpec((blk, buf.shape[1]), lambda i: (i, 0))],
        out_shape=[jax.ShapeDtypeStruct(dst.shape, F32), jax.ShapeDtypeStruct(buf.shape, F32)],
        input_output_aliases={5: 0},
        compiler_params=_cparams(("arbitrary",)),
        name="conv_b_sample",
    )(buf, z, z, w, bias, dst)


def _conv_c_prompt_body(gb_ref, gc_ref, hv_ref, w_ref, dst_ref, o_ref, buf_ref, pad_ref):
    del dst_ref
    t = gb_ref.shape[0]
    hist = SUBLANES
    _fill_padded(pad_ref, hist, t, gc_ref[...].astype(F32) * hv_ref[...].astype(F32))
    buf_ref[0, 0] = pad_ref[hist + t - (CONV_C_W - 1):hist + t, :]

    def step(i, carry):
        base = pl.multiple_of(i * CONV_ROWS, CONV_ROWS)
        acc = jnp.zeros((CONV_ROWS, pad_ref.shape[1]), F32)
        acc = _causal_taps(pad_ref, w_ref, base, hist, CONV_C_W, acc)
        o_ref[pl.ds(base, CONV_ROWS), :] = acc * gb_ref[pl.ds(base, CONV_ROWS), :].astype(F32)
        return carry

    lax.fori_loop(0, t // CONV_ROWS, step, 0)


def _conv_c_prompt(z, w, bsz, t, d_c, dst):
    cb = 512
    nb = d_c // cb
    return pl.pallas_call(
        _conv_c_prompt_body,
        grid=(bsz, nb),
        in_specs=[pl.BlockSpec((t, cb), lambda b, j: (b, j)),
                  pl.BlockSpec((t, cb), lambda b, j: (b, nb + j)),
                  pl.BlockSpec((t, cb), lambda b, j: (b, 2 * nb + j)),
                  pl.BlockSpec((CONV_C_W, cb), lambda b, j: (0, j)),
                  pl.BlockSpec(memory_space=pl.ANY)],
        out_specs=[pl.BlockSpec((t, cb), lambda b, j: (b, j)),
                   pl.BlockSpec((1, 1, CONV_C_W - 1, cb), lambda b, j: (0, b, 0, j))],
        out_shape=[jax.ShapeDtypeStruct(dst.shape, F32),
                   jax.ShapeDtypeStruct((1, bsz, CONV_C_W - 1, d_c), F32)],
        input_output_aliases={4: 0},
        scratch_shapes=[pltpu.VMEM((t + 2 * SUBLANES, cb), F32)],
        compiler_params=_cparams(("parallel", "parallel")),
        name="conv_c_prompt",
    )(z, z, z, w, dst)


def _conv_c_sample_body(n_valid, buf_ref, gb_ref, gc_ref, hv_ref, w_ref, dst_ref, o_ref, nbuf_ref):
    del dst_ref
    d_c = gb_ref.shape[1]
    blk = gb_ref.shape[0]
    _zero_tail_rows(o_ref, n_valid)
    u = gc_ref[...].astype(F32) * hv_ref[...].astype(F32)
    c = buf_ref[:, :d_c] * w_ref[0:1, :] + buf_ref[:, d_c:] * w_ref[1:2, :] + u * w_ref[2:3, :]
    o_ref[pl.ds(pl.multiple_of(pl.program_id(0) * blk, blk), blk), :] = gb_ref[...].astype(F32) * c
    nbuf_ref[:, :d_c] = buf_ref[:, d_c:]
    nbuf_ref[:, d_c:] = u


def _conv_c_sample(buf, z, w, row0, d_c, dst):
    ns = buf.shape[0]
    blk = 2 * SAMPLE_BLOCK
    r0 = row0 // blk
    return pl.pallas_call(
        functools.partial(_conv_c_sample_body, ns),
        grid=(ns // blk,),
        in_specs=[pl.BlockSpec((blk, buf.shape[1]), lambda i: (i, 0)),
                  pl.BlockSpec((blk, d_c), lambda i: (r0 + i, 0)),
                  pl.BlockSpec((blk, d_c), lambda i: (r0 + i, 1)),
                  pl.BlockSpec((blk, d_c), lambda i: (r0 + i, 2)),
                  pl.BlockSpec((CONV_C_W, d_c), lambda i: (0, 0)),
                  pl.BlockSpec(memory_space=pl.ANY)],
        out_specs=[_tail_spec(z.shape[0], row0, d_c),
                   pl.BlockSpec((blk, buf.shape[1]), lambda i: (i, 0))],
        out_shape=[jax.ShapeDtypeStruct(dst.shape, F32), jax.ShapeDtypeStruct(buf.shape, F32)],
        input_output_aliases={5: 0},
        compiler_params=_cparams(("arbitrary",)),
        name="conv_c_sample",
    )(buf, z, z, z, w, dst)


def _candidate_tables():
    pairs = [(a, b) for a in range(TOPK) for b in range(TOPK) if (a + 1) * (b + 1) <= TOPK]
    rows = -(-len(pairs) // SUBLANES) * SUBLANES
    p1 = np.zeros((rows, TOPK), np.float32)
    p2 = np.zeros((rows, TOPK), np.float32)
    neg = np.zeros((rows, LANES), np.float32)
    for r, (a, b) in enumerate(pairs):
        p1[r, a] = 1.0
        p2[r, b] = 1.0
    neg[len(pairs):] = -np.inf
    return p1, p2, neg, p1.T.copy()


def _take_top(works, index, count, tie_break):
    rows, lanes = works[0].shape
    slot = lax.broadcasted_iota(jnp.int32, (count, lanes), 0)
    works = list(works)
    vals = [jnp.zeros((count, lanes), F32) for _ in works]
    ranks = [jnp.full((rows, lanes), float(count), F32) for _ in works]
    for a in range(count):
        for i, work in enumerate(works):
            m = jnp.max(work, axis=0, keepdims=True)
            sel = work == m
            if tie_break:
                first = jnp.min(jnp.where(sel, index, float(rows)), axis=0, keepdims=True)
                sel = index == first
            ranks[i] = jnp.where(sel, float(a), ranks[i])
            works[i] = jnp.where(sel, -jnp.inf, work)
            vals[i] = jnp.where(slot == a, m, vals[i])
    return vals, ranks


def _sort_network(n):
    pairs, p = [], 1
    while p < n:
        k = p
        while k >= 1:
            for j in range(k % p, n - k, 2 * k):
                for i in range(min(k, n - j - k)):
                    if (i + j) // (2 * p) == (i + j + k) // (2 * p):
                        pairs.append((i + j, i + j + k))
            k //= 2
        p *= 2
    return pairs


def _top_values(scores, count):
    rows, lanes = scores[0].shape
    n = rows // SUBLANES
    assert n * SUBLANES == rows and count <= n
    cols = [[s[g * SUBLANES:(g + 1) * SUBLANES, :] for g in range(n)] for s in scores]
    for a, b in _sort_network(n):
        for w in cols:
            w[a], w[b] = jnp.maximum(w[a], w[b]), jnp.minimum(w[a], w[b])
    slot = lax.broadcasted_iota(jnp.int32, (count, lanes), 0)
    vals = [jnp.zeros((count, lanes), F32) for _ in scores]
    distinct = [jnp.ones((1, lanes), jnp.bool_) for _ in scores]
    last = [None for _ in scores]
    for r in range(count):
        for i, w in enumerate(cols):
            m = jnp.max(w[0], axis=0, keepdims=True)
            win = w[0] == m
            vals[i] = jnp.where(slot == r, m, vals[i])
            if last[i] is not None:
                distinct[i] = distinct[i] & (m < last[i])
            last[i] = m
            for g in range(count - r - 1):
                w[g] = jnp.where(win, w[g + 1], w[g])
    return vals, distinct


def _count(mask):
    return jnp.sum(mask.astype(F32), axis=0, keepdims=True)


def _route_heads(s1, s2, p1, p2, neg, p1t, key_idx, cand_idx, tie_break):
    n = len(s1)
    both = list(s1) + list(s2)
    if tie_break:
        vals, ranks = _take_top(both, key_idx, TOPK, True)
        rank_is = lambda i, a: ranks[i] == float(a)
        member = lambda i: ranks[i] < float(TOPK)
    else:
        vals, distinct = _top_values(both, TOPK)
        rank_is = lambda i, a: both[i] == vals[i][a:a + 1, :]
        member = lambda i: both[i] >= vals[i][TOPK - 1:TOPK, :]
    v1, v2 = vals[:n], vals[n:]
    cand = [_dot_f32(p1, v1[i], ((1,), (0,))) + _dot_f32(p2, v2[i], ((1,), (0,))) + neg for i in range(n)]
    _, crank = _take_top(cand, cand_idx, TOPK, tie_break)
    out = []
    for i in range(n):
        chosen = crank[i] < float(TOPK)
        cmax = v1[i][0:1, :] + v2[i][0:1, :]
        z = jnp.sum(jnp.where(chosen, jnp.exp(cand[i] - cmax), 0.0), axis=0, keepdims=True)
        width = jnp.dot(p1t, chosen.astype(F32), preferred_element_type=F32)
        b1 = jnp.zeros_like(s1[i])
        r2 = jnp.full(s2[i].shape, float(TOPK), F32)
        for a in range(TOPK):
            b1 = jnp.where(rank_is(i, a), width[a:a + 1, :], b1)
            r2 = jnp.where(rank_is(n + i, a), float(a), r2)
        in1, in2 = member(i), member(n + i)
        c1 = jnp.where(in1, jnp.exp(jnp.where(in1, s1[i] - v1[i][0:1, :], 0.0)) / z, 0.0)
        e2 = jnp.where(in2, jnp.exp(jnp.where(in2, s2[i] - v2[i][0:1, :], 0.0)), 0.0)
        clean = ((_count(in1) == float(TOPK)) & (_count(in2) == float(TOPK))
                 & (_count(chosen) == float(TOPK)))
        if not tie_break:
            clean = clean & distinct[i] & distinct[n + i]
        out.append((c1, b1, r2, e2, clean))
    return out


ROUTE_HEADS = 4


def _route_body(q_ref, keys_ref, p1_ref, p2_ref, neg_ref, p1t_ref, u_ref, v_ref,
                r2_ref, e2_ref, c1_ref, b1_ref, ub_ref, vt_ref, q3_ref):
    ub_ref[...] = u_ref[0].astype(BF16)
    vt_ref[...] = v_ref[0].T.astype(BF16)

    for hp in range(2 * P_HEADS):
        q3_ref[hp] = q_ref[:, hp * LANES:(hp + 1) * LANES]
    key_idx = lax.broadcasted_iota(jnp.int32, (N_KEYS, LANES), 0).astype(F32)
    cand_idx = lax.broadcasted_iota(jnp.int32, (p1_ref.shape[0], LANES), 0).astype(F32)

    def emit(heads, tie_break):
        s1 = [_dot_f32(keys_ref[h, 0], q3_ref[2 * h], ((1,), (1,))) for h in heads]
        s2 = [_dot_f32(keys_ref[h, 1], q3_ref[2 * h + 1], ((1,), (1,))) for h in heads]
        rows = _route_heads(s1, s2, p1_ref[...], p2_ref[...], neg_ref[...], p1t_ref[...],
                            key_idx, cand_idx, tie_break)
        all_clean = None
        for h, (c1, b1, r2, e2, clean) in zip(heads, rows):
            c1_ref[h] = c1
            b1_ref[h] = b1
            r2_ref[h] = r2.astype(r2_ref.dtype)
            e2_ref[h] = e2.astype(e2_ref.dtype)
            all_clean = clean if all_clean is None else all_clean & clean
        return all_clean

    def step(i, carry):
        heads = [i * ROUTE_HEADS + j for j in range(ROUTE_HEADS)]
        all_clean = emit(heads, False)

        @pl.when(jnp.min(all_clean.astype(F32)) < 0.5)
        def _():
            emit(heads, True)

        return carry

    lax.fori_loop(0, P_HEADS // ROUTE_HEADS, step, 0)


def _route(q, keys, table_u, table_v, layer):
    n = q.shape[0]
    _, n_exp, d = table_u.shape
    steps = n // LANES
    n_slabs = min(n_exp // TABLE_SLAB, 1 << (steps.bit_length() - 1))
    slab = n_exp // n_slabs
    assert n_slabs * slab == n_exp
    p1, p2, neg, p1t = _candidate_tables()
    whole = lambda a: pl.BlockSpec(a.shape, lambda i: (0,) * a.ndim)
    out_spec = pl.BlockSpec((P_HEADS, N_KEYS, LANES), lambda i: (0, 0, i))
    shape = lambda dt: jax.ShapeDtypeStruct((P_HEADS, N_KEYS, n), dt)
    slab_of = lambda i: jnp.minimum(i, n_slabs - 1)
    table_spec = pl.BlockSpec((1, slab, d), lambda i: (layer, slab_of(i), 0))
    return pl.pallas_call(
        _route_body,
        grid=(steps,),
        in_specs=[pl.BlockSpec((LANES, q.shape[1]), lambda i: (i, 0)), whole(keys),
                  whole(p1), whole(p2), whole(neg), whole(p1t), table_spec, table_spec],
        out_specs=[out_spec] * 4 + [pl.BlockSpec((slab, d), lambda i: (slab_of(i), 0)),
                                    pl.BlockSpec((d, slab), lambda i: (0, slab_of(i)))],
        out_shape=[shape(BF16), shape(BF16), shape(F32), shape(F32),
                   jax.ShapeDtypeStruct((n_exp, d), BF16), jax.ShapeDtypeStruct((d, n_exp), BF16)],
        scratch_shapes=[pltpu.VMEM((2 * P_HEADS, LANES, LANES), F32)],
        compiler_params=_cparams(("arbitrary",)),
        name="peer_route",
    )(q, keys, jnp.asarray(p1), jnp.asarray(p2), jnp.asarray(neg), jnp.asarray(p1t), table_u, table_v)


def _bf16_pair(c):
    hi = float(np.asarray(c, dtype=BF16))
    return hi, float(np.asarray(c - hi, dtype=BF16))


def _gelu_tanh(x):
    c0 = float(np.sqrt(2.0 / np.pi))
    c0_hi, c0_lo = _bf16_pair(c0)
    c1_hi, c1_lo = _bf16_pair(c0 * 0.044715)
    x2 = x * x
    inner = x * ((c1_hi * x2 + c0_hi) + (c1_lo * x2 + c0_lo))
    half = 0.5 * x
    return half + half * jnp.tanh(inner)


def _peer_dense_body(split, xt_ref, u_ref, vt_ref, r2_ref, e2_ref, c1_ref, b1_ref, x_ref, gf_ref, *rest):
    if split is None:
        o_ref, acc_ref = rest
    else:
        o_ref, ys_ref, acc_ref = rest
    e = pl.program_id(1)
    n_groups = xt_ref.shape[1] // PEER_LANES

    @pl.when(e == 0)
    def _():
        acc_ref[...] = jnp.zeros_like(acc_ref)

    def scores(j):
        cols = slice(j * PEER_LANES, (j + 1) * PEER_LANES)
        return jnp.dot(u_ref[...], xt_ref[:, cols], preferred_element_type=F32)

    def gates(j):
        cols = slice(j * PEER_LANES, (j + 1) * PEER_LANES)
        parts = []
        for r in range(EXPERT_BLOCK // N_KEYS):
            gate = jnp.zeros((N_KEYS, PEER_LANES), BF16)
            for h in range(P_HEADS):
                hit = r2_ref[h, :, cols] < b1_ref[h, r:r + 1, cols].astype(BF16)
                gate = gate + c1_ref[h, r:r + 1, cols].astype(BF16) * jnp.where(
                    hit, e2_ref[h, :, cols], jnp.zeros((), BF16))
            parts.append(gate)
        return jnp.concatenate(parts, axis=0)

    s_next = scores(0)
    for j in range(n_groups):
        s = s_next
        if j + 1 < n_groups:
            s_next = scores(j + 1)
        cols = slice(j * PEER_LANES, (j + 1) * PEER_LANES)
        weighted = gates(j) * _gelu_tanh(s.astype(BF16))
        acc_ref[:, cols] += jnp.dot(vt_ref[...], weighted, preferred_element_type=F32)

    @pl.when(e == pl.num_programs(1) - 1)
    def _():
        for j in range(n_groups):
            rows = slice(j * PEER_LANES, (j + 1) * PEER_LANES)
            out = x_ref[rows, :] + acc_ref[:, rows].T
            if split is None:
                o_ref[rows, :] = out
                continue
            y = _rmsnorm(out, gf_ref[...])
            o_ref[rows, :] = y
            if j == split.group:
                @pl.when(pl.program_id(0) == split.block)
                def _():
                    ys_ref[...] = y[:ys_ref.shape[0], :]


class _SampleSplit(NamedTuple):
    block: int
    group: int


def _peer_dense(xt, u_b, vt_b, r2, e2, c1, b1, x, g_final, n_prompt=None, n_sample=None):
    d, n = xt.shape
    n_exp = u_b.shape[0]
    keys_per_block = EXPERT_BLOCK // N_KEYS
    once = pl.Buffered(1)
    rank_spec = pl.BlockSpec((P_HEADS, N_KEYS, TOK_BLOCK), lambda i, e: (0, 0, i), pipeline_mode=once)
    first_spec = pl.BlockSpec((P_HEADS, keys_per_block, TOK_BLOCK), lambda i, e: (0, e, i))
    row_spec = pl.BlockSpec((TOK_BLOCK, d), lambda i, e: (i, 0), pipeline_mode=once)
    if n_prompt is None:
        split = None
        out_specs = row_spec
        out_shape = jax.ShapeDtypeStruct((n, d), F32)
    else:
        start = n_prompt % TOK_BLOCK
        assert start % PEER_LANES == 0 and n_sample <= PEER_LANES and n_prompt + n_sample <= n
        split = _SampleSplit(n_prompt // TOK_BLOCK, start // PEER_LANES)
        out_specs = [row_spec, pl.BlockSpec((n_sample, d), lambda i, e: (0, 0))]
        out_shape = [jax.ShapeDtypeStruct((n_prompt, d), F32), jax.ShapeDtypeStruct((n_sample, d), F32)]
    return pl.pallas_call(
        functools.partial(_peer_dense_body, split),
        grid=(n // TOK_BLOCK, n_exp // EXPERT_BLOCK),
        in_specs=[pl.BlockSpec((d, TOK_BLOCK), lambda i, e: (0, i), pipeline_mode=once),
                  pl.BlockSpec((EXPERT_BLOCK, d), lambda i, e: (e, 0)),
                  pl.BlockSpec((d, EXPERT_BLOCK), lambda i, e: (0, e)),
                  rank_spec, rank_spec, first_spec, first_spec, row_spec,
                  pl.BlockSpec((1, d), lambda i, e: (0, 0))],
        out_specs=out_specs,
        out_shape=out_shape,
        scratch_shapes=[pltpu.VMEM((d, TOK_BLOCK), F32)],
        compiler_params=_cparams(("arbitrary", "arbitrary")),
        name="peer_dense" if split is None else "peer_dense_final",
    )(xt, u_b, vt_b, r2, e2, c1, b1, x, g_final)


def _peer(x, g_norm, wq, keys, table_u, table_v, layer, g_final, n_prompt=None, n_sample=None):
    q, xt = _norm_matmul_t(x, g_norm, wq, wq.shape[1])
    r2, e2, c1, b1, u_b, vt_b = _route(q, keys, table_u, table_v, layer)
    return _peer_dense(xt, u_b, vt_b, r2, e2, c1, b1, x, g_final, n_prompt, n_sample)


def _col_form(a, n_dk):
    ns = a.shape[0]
    a = a.reshape(ns // SAMPLE_BLOCK, SAMPLE_BLOCK, H_A, n_dk)
    return a.transpose(0, 2, 3, 1)


def kernel(x_prompt, x_sample, state_gla, state_conv_b, state_conv_c, norm_mix, norm_ffn, norm_final, w_in_even, w_a2, b_a2, gla_norm_g, conv_b_w, conv_b_bias, conv_b_ln_g, conv_b_ln_b, w_out_even, w_in_odd, conv_c_w, w_out_odd, peer_wq, peer_keys, peer_u, peer_v):
    bsz, t, d = x_prompt.shape
    ns = x_sample.shape[0]
    n_prompt = bsz * t
    n_real = n_prompt + ns
    n_tok = -(-n_real // TOK_BLOCK) * TOK_BLOCK
    assert t % GLA_CHUNK == 0 and t % CONV_ROWS == 0 and ns % (2 * SAMPLE_BLOCK) == 0
    assert x_sample.shape[1] == 1 and n_prompt % (2 * SAMPLE_BLOCK) == 0

    d_a = d // 2
    dk_a = d_a // 2
    d_b = d - d_a
    rank = w_a2.shape[1]
    col_a = 2 * dk_a + 2 * d_a
    row = lambda a: a.reshape(1, -1)

    x = jnp.concatenate([x_prompt.reshape(n_prompt, d), x_sample.reshape(ns, d),
                         jnp.zeros((n_tok - n_real, d), F32)], axis=0)

    w_in = w_in_even[0]
    w_main = jnp.concatenate([w_in[:, :col_a], w_in[:, col_a + rank:]], axis=1).astype(BF16)
    w_gate = jnp.pad(w_in[:, col_a:col_a + rank], ((0, 0), (0, LANES - rank))).astype(BF16)
    w_a2p = jnp.pad(w_a2[0], ((0, LANES - rank), (0, 0)))
    z, la = _norm_matmul_gate(x, row(norm_mix[0]), w_main, w_gate, w_a2p, row(b_a2[0]), 1024)

    gn = row(gla_norm_g[0])
    zs = z[n_prompt:n_real].astype(F32)
    las = la[n_prompt:n_real]
    oa, gla_s = _gla_sample(state_gla, _col_form(las, dk_a // H_A), _col_form(zs[:, dk_a:2 * dk_a], dk_a // H_A),
                            _col_form(zs[:, :dk_a], dk_a // H_A),
                            zs[:, 2 * dk_a:2 * dk_a + d_a], zs[:, 2 * dk_a + d_a:col_a], gn,
                            jnp.zeros((n_tok, d_a), F32), n_prompt)
    oa, gla_p = _gla_prompt(z, la, gn, bsz, t, oa)

    ub, cb_s = _conv_b_sample(state_conv_b[0].reshape(ns, -1), z, conv_b_w[0], row(conv_b_bias[0]),
                              n_prompt, col_a, d_b, jnp.zeros((n_tok, d_b), F32))
    ub, cb_p = _conv_b_prompt(z, conv_b_w[0], row(conv_b_bias[0]), bsz, t, col_a, d_b, ub)
    x = _res_matmul_ln(oa, ub, row(conv_b_ln_g[0]), row(conv_b_ln_b[0]), w_out_even[0].astype(BF16), x, d)
    wq_b = peer_wq.astype(BF16)
    x = _peer(x, row(norm_ffn[0]), wq_b[0], peer_keys[0], peer_u, peer_v, 0, row(norm_final))

    z = _norm_matmul(x, row(norm_mix[1]), w_in_odd[0].astype(BF16), 1024)
    gc, cc_s = _conv_c_sample(state_conv_c[0].reshape(ns, -1), z, conv_c_w[0], n_prompt, d,
                              jnp.zeros((n_tok, d), F32))
    gc, cc_p = _conv_c_prompt(z, conv_c_w[0], bsz, t, d, gc)
    x = _res_matmul(gc, w_out_odd[0].astype(BF16), x, d)
    y_prompt, y_sample = _peer(x, row(norm_ffn[1]), wq_b[1], peer_keys[1], peer_u, peer_v, 1, row(norm_final),
                               n_prompt, ns)
    return (y_prompt.reshape(bsz, t, d), y_sample.reshape(ns, 1, d), gla_p, cb_p, cc_p,
            gla_s, cb_s.reshape(state_conv_b.shape), cc_s.reshape(state_conv_c.shape))
```

```python
import functools
from typing import NamedTuple

import numpy as np
import jax
import jax.numpy as jnp
from jax import lax
from jax.experimental import pallas as pl
from jax.experimental.pallas import tpu as pltpu

F32 = jnp.float32
BF16 = jnp.bfloat16
HIGHEST = lax.Precision.HIGHEST

EPS = 1e-6
GATE_TAU = 16.0
H_A = 4
P_HEADS = 8
N_KEYS = 128
TOPK = 16
CONV_B_W = 31
CONV_C_W = 3

LANES = 128
SUBLANES = 8
VMEM_LIMIT = 56 * 1024 * 1024

TOK_BLOCK = 768
EXPERT_BLOCK = 1024
PEER_LANES = 256
TABLE_SLAB = 256
GLA_CHUNK = 128
GLA_SUB = 16
EXP_CLAMP = 80.0
CONV_ROWS = 64
SAMPLE_BLOCK = 8


def _cparams(sem):
    return pltpu.CompilerParams(dimension_semantics=sem, vmem_limit_bytes=VMEM_LIMIT)


def _rmsnorm(x, g):
    return x * lax.rsqrt(jnp.mean(x * x, axis=-1, keepdims=True) + EPS) * g


def _sigmoid(x):
    return jax.nn.sigmoid(x)


def _weight_spec(k, m, tn):
    if tn == m:
        return pl.BlockSpec((k, tn), lambda i, j: (0, 0), pipeline_mode=pl.Buffered(1))
    return pl.BlockSpec((k, tn), lambda i, j: (0, j))


def _log_sigmoid(x):
    return jnp.minimum(x, 0.0) - jnp.log(1.0 + jnp.exp(-jnp.abs(x)))


def _norm_matmul_body(x_ref, g_ref, w_ref, o_ref, xn_ref):
    @pl.when(pl.program_id(1) == 0)
    def _():
        xn_ref[...] = _rmsnorm(x_ref[...], g_ref[...]).astype(BF16)

    o_ref[...] = jnp.dot(xn_ref[...], w_ref[...], preferred_element_type=F32).astype(o_ref.dtype)


def _norm_matmul(x, g, w, tn):
    n, d = x.shape
    m = w.shape[1]
    return pl.pallas_call(
        _norm_matmul_body,
        grid=(n // TOK_BLOCK, m // tn),
        in_specs=[pl.BlockSpec((TOK_BLOCK, d), lambda i, j: (i, 0)),
                  pl.BlockSpec((1, d), lambda i, j: (0, 0)),
                  pl.BlockSpec((d, tn), lambda i, j: (0, j))],
        out_specs=pl.BlockSpec((TOK_BLOCK, tn), lambda i, j: (i, j)),
        out_shape=jax.ShapeDtypeStruct((n, m), BF16),
        scratch_shapes=[pltpu.VMEM((TOK_BLOCK, d), BF16)],
        compiler_params=_cparams(("parallel", "arbitrary")),
        name="norm_matmul",
    )(x, g, w)


def _norm_matmul_gate_body(x_ref, g_ref, w_ref, wa_ref, wa2_ref, ba2_ref, o_ref, la_ref, xn_ref):
    @pl.when(pl.program_id(1) == 0)
    def _():
        xn = _rmsnorm(x_ref[...], g_ref[...]).astype(BF16)
        xn_ref[...] = xn
        a_lr = jnp.dot(xn, wa_ref[...], preferred_element_type=F32)
        pre = jnp.dot(a_lr, wa2_ref[...], precision=HIGHEST, preferred_element_type=F32) + ba2_ref[...]
        la_ref[...] = _log_sigmoid(pre) / GATE_TAU

    o_ref[...] = jnp.dot(xn_ref[...], w_ref[...], preferred_element_type=F32).astype(o_ref.dtype)


def _norm_matmul_gate(x, g, w, wa, wa2, ba2, tn):
    n, d = x.shape
    m = w.shape[1]
    dk = wa2.shape[1]
    return pl.pallas_call(
        _norm_matmul_gate_body,
        grid=(n // TOK_BLOCK, m // tn),
        in_specs=[pl.BlockSpec((TOK_BLOCK, d), lambda i, j: (i, 0)),
                  pl.BlockSpec((1, d), lambda i, j: (0, 0)),
                  pl.BlockSpec((d, tn), lambda i, j: (0, j)),
                  pl.BlockSpec((d, LANES), lambda i, j: (0, 0)),
                  pl.BlockSpec((LANES, dk), lambda i, j: (0, 0)),
                  pl.BlockSpec((1, dk), lambda i, j: (0, 0))],
        out_specs=[pl.BlockSpec((TOK_BLOCK, tn), lambda i, j: (i, j)),
                   pl.BlockSpec((TOK_BLOCK, dk), lambda i, j: (i, 0))],
        out_shape=[jax.ShapeDtypeStruct((n, m), BF16), jax.ShapeDtypeStruct((n, dk), F32)],
        scratch_shapes=[pltpu.VMEM((TOK_BLOCK, d), BF16)],
        compiler_params=_cparams(("parallel", "arbitrary")),
        name="norm_matmul_gate",
    )(x, g, w, wa, wa2, ba2)


def _norm_matmul_t_body(x_ref, g_ref, w_ref, o_ref, xt_ref, xn_ref):
    @pl.when(pl.program_id(1) == 0)
    def _():
        xn = _rmsnorm(x_ref[...], g_ref[...])
        xn_ref[...] = xn.astype(BF16)
        xt_ref[...] = xn.T.astype(BF16)

    o_ref[...] = jnp.dot(xn_ref[...], w_ref[...], preferred_element_type=F32)


def _norm_matmul_t(x, g, w, tn):
    n, d = x.shape
    m = w.shape[1]
    return pl.pallas_call(
        _norm_matmul_t_body,
        grid=(n // TOK_BLOCK, m // tn),
        in_specs=[pl.BlockSpec((TOK_BLOCK, d), lambda i, j: (i, 0)),
                  pl.BlockSpec((1, d), lambda i, j: (0, 0)),
                  _weight_spec(d, m, tn)],
        out_specs=[pl.BlockSpec((TOK_BLOCK, tn), lambda i, j: (i, j)),
                   pl.BlockSpec((d, TOK_BLOCK), lambda i, j: (0, i))],
        out_shape=[jax.ShapeDtypeStruct((n, m), F32), jax.ShapeDtypeStruct((d, n), BF16)],
        scratch_shapes=[pltpu.VMEM((TOK_BLOCK, d), BF16)],
        compiler_params=_cparams(("parallel", "arbitrary")),
        name="norm_matmul_t",
    )(x, g, w)


def _res_matmul_body(a_ref, w_ref, x_ref, o_ref, ab_ref):
    @pl.when(pl.program_id(1) == 0)
    def _():
        ab_ref[...] = a_ref[...].astype(BF16)

    o_ref[...] = x_ref[...] + jnp.dot(ab_ref[...], w_ref[...], preferred_element_type=F32)


def _res_matmul(a, w, x, tn):
    n, k = a.shape
    m = w.shape[1]
    return pl.pallas_call(
        _res_matmul_body,
        grid=(n // TOK_BLOCK, m // tn),
        in_specs=[pl.BlockSpec((TOK_BLOCK, k), lambda i, j: (i, 0)),
                  _weight_spec(k, m, tn),
                  pl.BlockSpec((TOK_BLOCK, tn), lambda i, j: (i, j))],
        out_specs=pl.BlockSpec((TOK_BLOCK, tn), lambda i, j: (i, j)),
        out_shape=jax.ShapeDtypeStruct((n, m), F32),
        scratch_shapes=[pltpu.VMEM((TOK_BLOCK, k), BF16)],
        compiler_params=_cparams(("parallel", "arbitrary")),
        name="res_matmul",
    )(a, w, x)


def _res_matmul_ln_body(oa_ref, ub_ref, lg_ref, lb_ref, w_ref, x_ref, o_ref, ab_ref):
    da = oa_ref.shape[1]

    @pl.when(pl.program_id(1) == 0)
    def _():
        u = ub_ref[...]
        uc = u - jnp.mean(u, axis=-1, keepdims=True)
        y = uc * lax.rsqrt(jnp.mean(uc * uc, axis=-1, keepdims=True) + EPS) * lg_ref[...] + lb_ref[...]
        ab_ref[:, :da] = oa_ref[...].astype(BF16)
        ab_ref[:, da:] = (y * _sigmoid(y)).astype(BF16)

    o_ref[...] = x_ref[...] + jnp.dot(ab_ref[...], w_ref[...], preferred_element_type=F32)


def _res_matmul_ln(oa, ub, lg, lb, w, x, tn):
    n, da = oa.shape
    db = ub.shape[1]
    m = w.shape[1]
    return pl.pallas_call(
        _res_matmul_ln_body,
        grid=(n // TOK_BLOCK, m // tn),
        in_specs=[pl.BlockSpec((TOK_BLOCK, da), lambda i, j: (i, 0)),
                  pl.BlockSpec((TOK_BLOCK, db), lambda i, j: (i, 0)),
                  pl.BlockSpec((1, db), lambda i, j: (0, 0)),
                  pl.BlockSpec((1, db), lambda i, j: (0, 0)),
                  _weight_spec(da + db, m, tn),
                  pl.BlockSpec((TOK_BLOCK, tn), lambda i, j: (i, j))],
        out_specs=pl.BlockSpec((TOK_BLOCK, tn), lambda i, j: (i, j)),
        out_shape=jax.ShapeDtypeStruct((n, m), F32),
        scratch_shapes=[pltpu.VMEM((TOK_BLOCK, da + db), BF16)],
        compiler_params=_cparams(("parallel", "arbitrary")),
        name="res_matmul_ln",
    )(oa, ub, lg, lb, w, x)


def _dot_f32(a, b, dims):
    return lax.dot_general(a, b, (dims, ((), ())), precision=HIGHEST, preferred_element_type=F32)


def _dot_bf16(a, b, dims):
    return lax.dot_general(a.astype(BF16), b.astype(BF16), (dims, ((), ())), preferred_element_type=F32)


def _head_out(o, gn, g):
    y = o * lax.rsqrt(jnp.mean(o * o, axis=-1, keepdims=True) + EPS) * gn
    return y * (g * _sigmoid(g))


def _gla_prompt_body(q_ref, k_ref, v_ref, g_ref, la_ref, gn_ref, dst_ref, o_ref, s_out_ref, s_ref, acc_ref):
    del dst_ref
    c = pl.program_id(1)
    dk = GLA_CHUNK
    dv = s_ref.shape[2]

    @pl.when(c == 0)
    def _():
        s_ref[...] = jnp.zeros_like(s_ref)

    row = lax.broadcasted_iota(jnp.int32, (GLA_CHUNK, GLA_CHUNK), 0)
    col = lax.broadcasted_iota(jnp.int32, (GLA_CHUNK, GLA_CHUNK), 1)
    tri = (row >= col).astype(F32)
    sub_row = lax.broadcasted_iota(jnp.int32, (GLA_SUB, GLA_SUB), 0)
    sub_col = lax.broadcasted_iota(jnp.int32, (GLA_SUB, GLA_SUB), 1)
    causal = sub_row >= sub_col

    for h in range(H_A):
        kcols = slice(h * dk, (h + 1) * dk)
        vcols = slice(h * dv, (h + 1) * dv)
        la = la_ref[:, kcols]
        b = _dot_f32(tri, la, ((1,), (0,)))
        b_last = b[GLA_CHUNK - 1:GLA_CHUNK, :]
        q = q_ref[:, kcols].astype(F32) * (dk ** -0.5)
        k = k_ref[:, kcols].astype(F32)
        v_b = v_ref[:, vcols].astype(BF16)
        s = s_ref[h]

        acc_ref[h] = _dot_bf16(q * jnp.exp(b), s, ((1,), (0,)))
        for j in range(GLA_CHUNK // GLA_SUB):
            lo, hi = j * GLA_SUB, (j + 1) * GLA_SUB
            b_j = b[lo:hi, :]
            b_first = b[lo:lo + 1, :]
            b_end = b[hi - 1:hi, :]
            q_d = q[lo:hi, :] * jnp.exp(b_j - b_first)
            k_d = k[lo:hi, :] * jnp.exp(jnp.minimum(b_first - b_j, EXP_CLAMP))
            sc = jnp.where(causal, _dot_bf16(q_d, k_d, ((1,), (1,))), 0.0)
            acc_ref[h, lo:hi, :] += _dot_bf16(sc, v_b[lo:hi, :], ((1,), (0,)))
            if hi < GLA_CHUNK:
                k_p = k[lo:hi, :] * jnp.exp(b_end - b_j)
                q_p = q[hi:, :] * jnp.exp(b[hi:, :] - b_end)
                sc = _dot_bf16(q_p, k_p, ((1,), (1,)))
                acc_ref[h, hi:, :] += _dot_bf16(sc, v_b[lo:hi, :], ((1,), (0,)))

        decay = jnp.exp(jnp.broadcast_to(b_last, (dk, dk))).T
        k_e = k * jnp.exp(b_last - b)
        s_new = jnp.concatenate([decay] * (dv // dk), axis=1) * s + _dot_bf16(k_e.T, v_b, ((1,), (0,)))
        s_ref[h] = s_new
        o_ref[:, vcols] = _head_out(acc_ref[h], gn_ref[:, vcols], g_ref[:, vcols].astype(F32))

    @pl.when(c == pl.num_programs(1) - 1)
    def _():
        s_out_ref[0, 0] = s_ref[...]


def _gla_prompt(z, la, gn, bsz, t, dst):
    dk, dv = GLA_CHUNK, 2 * GLA_CHUNK
    d_k, d_v = H_A * dk, H_A * dv
    nc = t // GLA_CHUNK
    rows = lambda b, c: b * nc + c
    return pl.pallas_call(
        _gla_prompt_body,
        grid=(bsz, nc),
        in_specs=[pl.BlockSpec((GLA_CHUNK, d_k), lambda b, c: (rows(b, c), 0)),
                  pl.BlockSpec((GLA_CHUNK, d_k), lambda b, c: (rows(b, c), 1)),
                  pl.BlockSpec((GLA_CHUNK, d_v), lambda b, c: (rows(b, c), 2 * d_k // d_v)),
                  pl.BlockSpec((GLA_CHUNK, d_v), lambda b, c: (rows(b, c), 2 * d_k // d_v + 1)),
                  pl.BlockSpec((GLA_CHUNK, d_k), lambda b, c: (rows(b, c), 0)),
                  pl.BlockSpec((1, d_v), lambda b, c: (0, 0)),
                  pl.BlockSpec(memory_space=pl.ANY)],
        out_specs=[pl.BlockSpec((GLA_CHUNK, d_v), lambda b, c: (rows(b, c), 0)),
                   pl.BlockSpec((1, 1, H_A, dk, dv), lambda b, c: (0, b, 0, 0, 0))],
        out_shape=[jax.ShapeDtypeStruct(dst.shape, F32),
                   jax.ShapeDtypeStruct((1, bsz, H_A, dk, dv), F32)],
        input_output_aliases={6: 0},
        scratch_shapes=[pltpu.VMEM((H_A, dk, dv), F32), pltpu.VMEM((H_A, GLA_CHUNK, dv), F32)],
        compiler_params=_cparams(("parallel", "arbitrary")),
        name="gla_prompt",
    )(z, z, z, z, la, gn, dst)


def _tail_spec(n_rows, row0, cols):
    tail = n_rows - row0
    assert row0 % tail == 0 and tail % SUBLANES == 0
    return pl.BlockSpec((tail, cols), lambda i: (row0 // tail, 0))


def _zero_tail_rows(o_ref, n_valid):
    @pl.when(pl.program_id(0) == 0)
    def _():
        o_ref[n_valid:, :] = jnp.zeros((o_ref.shape[0] - n_valid, o_ref.shape[1]), F32)


def _gla_sample_body(n_valid, st_ref, lat_ref, kt_ref, qt_ref, v_ref, g_ref, gn_ref, dst_ref, o_ref, s_out_ref,
                     rows_ref):
    del dst_ref
    dv = st_ref.shape[-1]
    _zero_tail_rows(o_ref, n_valid)
    for i in range(SAMPLE_BLOCK):
        for h in range(H_A):
            decay = jnp.exp(lat_ref[0, h, :, i:i + 1])
            v_row = v_ref[i:i + 1, h * dv:(h + 1) * dv]
            s_new = decay * st_ref[0, i, h] + kt_ref[0, h, :, i:i + 1] * v_row
            s_out_ref[0, i, h] = s_new
            q_col = qt_ref[0, h, :, i:i + 1] * (st_ref.shape[-2] ** -0.5)
            o = jnp.sum(q_col * s_new, axis=0, keepdims=True)
            rows_ref[i:i + 1, h * dv:(h + 1) * dv] = _head_out(
                o, gn_ref[:, h * dv:(h + 1) * dv], g_ref[i:i + 1, h * dv:(h + 1) * dv])
    base = pl.multiple_of(pl.program_id(0) * SAMPLE_BLOCK, SAMPLE_BLOCK)
    o_ref[pl.ds(base, SAMPLE_BLOCK), :] = rows_ref[...]


def _gla_sample(state, lat, kt, qt, v, g, gn, dst, row0):
    _, ns, _, dk, dv = state.shape
    n_rows = dst.shape[0]
    steps = ns // SAMPLE_BLOCK
    col_spec = pl.BlockSpec((1, H_A, dk, SAMPLE_BLOCK), lambda i: (i, 0, 0, 0))
    row_spec = pl.BlockSpec((SAMPLE_BLOCK, H_A * dv), lambda i: (i, 0))
    st_spec = pl.BlockSpec((1, SAMPLE_BLOCK, H_A, dk, dv), lambda i: (0, i, 0, 0, 0))
    return pl.pallas_call(
        functools.partial(_gla_sample_body, ns),
        grid=(steps,),
        in_specs=[st_spec, col_spec, col_spec, col_spec, row_spec, row_spec,
                  pl.BlockSpec((1, H_A * dv), lambda i: (0, 0)),
                  pl.BlockSpec(memory_space=pl.ANY)],
        out_specs=[_tail_spec(n_rows, row0, H_A * dv), st_spec],
        out_shape=[jax.ShapeDtypeStruct(dst.shape, F32), jax.ShapeDtypeStruct(state.shape, F32)],
        input_output_aliases={7: 0},
        scratch_shapes=[pltpu.VMEM((SAMPLE_BLOCK, H_A * dv), F32)],
        compiler_params=_cparams(("arbitrary",)),
        name="gla_sample",
    )(state, lat, kt, qt, v, g, gn, dst)


def _fill_padded(pad_ref, hist, t, value):
    cols = pad_ref.shape[1]
    pad_ref[0:hist, :] = jnp.zeros((hist, cols), F32)
    pad_ref[hist:hist + t, :] = value
    pad_ref[hist + t:, :] = jnp.zeros((SUBLANES, cols), F32)


def _causal_taps(pad_ref, w_ref, base, hist, n_taps, acc):
    first = hist - (n_taps - 1)
    win = pad_ref[pl.ds(base, CONV_ROWS + hist + SUBLANES), :]
    for r in range(SUBLANES):
        taps = [w for w in range(n_taps) if (first + w) % SUBLANES == r]
        if not taps:
            continue
        shifted = win if r == 0 else pltpu.roll(win, win.shape[0] - r, axis=0)
        for w in taps:
            lo = (first + w) // SUBLANES * SUBLANES
            acc = acc + shifted[lo:lo + CONV_ROWS, :] * w_ref[w:w + 1, :]
    return acc


def _conv_b_prompt_body(a_ref, b_ref, w_ref, bias_ref, dst_ref, o_ref, buf_ref, pad_ref):
    del dst_ref
    t = a_ref.shape[0]
    hist = 4 * SUBLANES
    _fill_padded(pad_ref, hist, t, a_ref[...].astype(F32) * _sigmoid(b_ref[...].astype(F32)))
    buf_ref[0, 0] = pad_ref[hist + t - (CONV_B_W - 1):hist + t, :]

    def step(i, carry):
        base = pl.multiple_of(i * CONV_ROWS, CONV_ROWS)
        acc = jnp.zeros((CONV_ROWS, pad_ref.shape[1]), F32) + bias_ref[...]
        o_ref[pl.ds(base, CONV_ROWS), :] = _causal_taps(pad_ref, w_ref, base, hist, CONV_B_W, acc)
        return carry

    lax.fori_loop(0, t // CONV_ROWS, step, 0)


def _conv_b_prompt(z, w, bias, bsz, t, col0, d_b, dst):
    cb = 256
    nb = d_b // cb
    a0, b0 = col0 // cb, (col0 + d_b) // cb
    return pl.pallas_call(
        _conv_b_prompt_body,
        grid=(bsz, nb),
        in_specs=[pl.BlockSpec((t, cb), lambda b, j: (b, a0 + j)),
                  pl.BlockSpec((t, cb), lambda b, j: (b, b0 + j)),
                  pl.BlockSpec((CONV_B_W, cb), lambda b, j: (0, j)),
                  pl.BlockSpec((1, cb), lambda b, j: (0, j)),
                  pl.BlockSpec(memory_space=pl.ANY)],
        out_specs=[pl.BlockSpec((t, cb), lambda b, j: (b, j)),
                   pl.BlockSpec((1, 1, CONV_B_W - 1, cb), lambda b, j: (0, b, 0, j))],
        out_shape=[jax.ShapeDtypeStruct(dst.shape, F32),
                   jax.ShapeDtypeStruct((1, bsz, CONV_B_W - 1, d_b), F32)],
        input_output_aliases={4: 0},
        scratch_shapes=[pltpu.VMEM((t + 5 * SUBLANES, cb), F32)],
        compiler_params=_cparams(("parallel", "parallel")),
        name="conv_b_prompt",
    )(z, z, w, bias, dst)


def _conv_b_sample_body(n_valid, buf_ref, a_ref, b_ref, w_ref, bias_ref, dst_ref, o_ref, nbuf_ref):
    del dst_ref
    d_b = a_ref.shape[1]
    blk = a_ref.shape[0]
    _zero_tail_rows(o_ref, n_valid)
    glu = a_ref[...].astype(F32) * _sigmoid(b_ref[...].astype(F32))
    acc = bias_ref[...] + glu * w_ref[CONV_B_W - 1:CONV_B_W, :]
    for w in range(CONV_B_W - 1):
        acc = acc + buf_ref[:, w * d_b:(w + 1) * d_b] * w_ref[w:w + 1, :]
    o_ref[pl.ds(pl.multiple_of(pl.program_id(0) * blk, blk), blk), :] = acc
    nbuf_ref[:, :(CONV_B_W - 2) * d_b] = buf_ref[:, d_b:]
    nbuf_ref[:, (CONV_B_W - 2) * d_b:] = glu


def _conv_b_sample(buf, z, w, bias, row0, col0, d_b, dst):
    ns = buf.shape[0]
    blk = 2 * SAMPLE_BLOCK
    r0 = row0 // blk
    return pl.pallas_call(
        functools.partial(_conv_b_sample_body, ns),
        grid=(ns // blk,),
        in_specs=[pl.BlockSpec((blk, buf.shape[1]), lambda i: (i, 0)),
                  pl.BlockSpec((blk, d_b), lambda i: (r0 + i, col0 // d_b)),
                  pl.BlockSpec((blk, d_b), lambda i: (r0 + i, col0 // d_b + 1)),
                  pl.BlockSpec((CONV_B_W, d_b), lambda i: (0, 0)),
                  pl.BlockSpec((1, d_b), lambda i: (0, 0)),
                  pl.BlockSpec(memory_space=pl.ANY)],
        out_specs=[_tail_spec(z.shape[0], row0, d_b),
                   pl.BlockSpec((blk, buf.shape[1]), lambda i: (i, 0))],
        out_shape=[jax.ShapeDtypeStruct(dst.shape, F32), jax.ShapeDtypeStruct(buf.shape, F32)],
        input_output_aliases={5: 0},
        compiler_params=_cparams(("arbitrary",)),
        name="conv_b_sample",
    )(buf, z, z, w, bias, dst)


def _conv_c_prompt_body(gb_ref, gc_ref, hv_ref, w_ref, dst_ref, o_ref, buf_ref, pad_ref):
    del dst_ref
    t = gb_ref.shape[0]
    hist = SUBLANES
    _fill_padded(pad_ref, hist, t, gc_ref[...].astype(F32) * hv_ref[...].astype(F32))
    buf_ref[0, 0] = pad_ref[hist + t - (CONV_C_W - 1):hist + t, :]

    def step(i, carry):
        base = pl.multiple_of(i * CONV_ROWS, CONV_ROWS)
        acc = jnp.zeros((CONV_ROWS, pad_ref.shape[1]), F32)
        acc = _causal_taps(pad_ref, w_ref, base, hist, CONV_C_W, acc)
        o_ref[pl.ds(base, CONV_ROWS), :] = acc * gb_ref[pl.ds(base, CONV_ROWS), :].astype(F32)
        return carry

    lax.fori_loop(0, t // CONV_ROWS, step, 0)


def _conv_c_prompt(z, w, bsz, t, d_c, dst):
    cb = 512
    nb = d_c // cb
    return pl.pallas_call(
        _conv_c_prompt_body,
        grid=(bsz, nb),
        in_specs=[pl.BlockSpec((t, cb), lambda b, j: (b, j)),
                  pl.BlockSpec((t, cb), lambda b, j: (b, nb + j)),
                  pl.BlockSpec((t, cb), lambda b, j: (b, 2 * nb + j)),
                  pl.BlockSpec((CONV_C_W, cb), lambda b, j: (0, j)),
                  pl.BlockSpec(memory_space=pl.ANY)],
        out_specs=[pl.BlockSpec((t, cb), lambda b, j: (b, j)),
                   pl.BlockSpec((1, 1, CONV_C_W - 1, cb), lambda b, j: (0, b, 0, j))],
        out_shape=[jax.ShapeDtypeStruct(dst.shape, F32),
                   jax.ShapeDtypeStruct((1, bsz, CONV_C_W - 1, d_c), F32)],
        input_output_aliases={4: 0},
        scratch_shapes=[pltpu.VMEM((t + 2 * SUBLANES, cb), F32)],
        compiler_params=_cparams(("parallel", "parallel")),
        name="conv_c_prompt",
    )(z, z, z, w, dst)


def _conv_c_sample_body(n_valid, buf_ref, gb_ref, gc_ref, hv_ref, w_ref, dst_ref, o_ref, nbuf_ref):
    del dst_ref
    d_c = gb_ref.shape[1]
    blk = gb_ref.shape[0]
    _zero_tail_rows(o_ref, n_valid)
    u = gc_ref[...].astype(F32) * hv_ref[...].astype(F32)
    c = buf_ref[:, :d_c] * w_ref[0:1, :] + buf_ref[:, d_c:] * w_ref[1:2, :] + u * w_ref[2:3, :]
    o_ref[pl.ds(pl.multiple_of(pl.program_id(0) * blk, blk), blk), :] = gb_ref[...].astype(F32) * c
    nbuf_ref[:, :d_c] = buf_ref[:, d_c:]
    nbuf_ref[:, d_c:] = u


def _conv_c_sample(buf, z, w, row0, d_c, dst):
    ns = buf.shape[0]
    blk = 2 * SAMPLE_BLOCK
    r0 = row0 // blk
    return pl.pallas_call(
        functools.partial(_conv_c_sample_body, ns),
        grid=(ns // blk,),
        in_specs=[pl.BlockSpec((blk, buf.shape[1]), lambda i: (i, 0)),
                  pl.BlockSpec((blk, d_c), lambda i: (r0 + i, 0)),
                  pl.BlockSpec((blk, d_c), lambda i: (r0 + i, 1)),
                  pl.BlockSpec((blk, d_c), lambda i: (r0 + i, 2)),
                  pl.BlockSpec((CONV_C_W, d_c), lambda i: (0, 0)),
                  pl.BlockSpec(memory_space=pl.ANY)],
        out_specs=[_tail_spec(z.shape[0], row0, d_c),
                   pl.BlockSpec((blk, buf.shape[1]), lambda i: (i, 0))],
        out_shape=[jax.ShapeDtypeStruct(dst.shape, F32), jax.ShapeDtypeStruct(buf.shape, F32)],
        input_output_aliases={5: 0},
        compiler_params=_cparams(("arbitrary",)),
        name="conv_c_sample",
    )(buf, z, z, z, w, dst)


def _candidate_tables():
    pairs = [(a, b) for a in range(TOPK) for b in range(TOPK) if (a + 1) * (b + 1) <= TOPK]
    rows = SUBLANES
    while rows < len(pairs):
        rows *= 2
    p1 = np.zeros((rows, TOPK), np.float32)
    p2 = np.zeros((rows, TOPK), np.float32)
    neg = np.zeros((rows, LANES), np.float32)
    for r, (a, b) in enumerate(pairs):
        p1[r, a] = 1.0
        p2[r, b] = 1.0
    neg[len(pairs):] = -np.inf
    return p1, p2, neg, p1.T.copy()


def _take_top(works, index, count, tie_break):
    rows, lanes = works[0].shape
    slot = lax.broadcasted_iota(jnp.int32, (count, lanes), 0)
    works = list(works)
    vals = [jnp.zeros((count, lanes), F32) for _ in works]
    ranks = [jnp.full((rows, lanes), float(count), F32) for _ in works]
    for a in range(count):
        for i, work in enumerate(works):
            m = jnp.max(work, axis=0, keepdims=True)
            sel = work == m
            if tie_break:
                first = jnp.min(jnp.where(sel, index, float(rows)), axis=0, keepdims=True)
                sel = index == first
            ranks[i] = jnp.where(sel, float(a), ranks[i])
            works[i] = jnp.where(sel, -jnp.inf, work)
            vals[i] = jnp.where(slot == a, m, vals[i])
    return vals, ranks


def _sort_network(n):
    pairs, p = [], 1
    while p < n:
        k = p
        while k >= 1:
            for j in range(k % p, n - k, 2 * k):
                for i in range(min(k, n - j - k)):
                    if (i + j) // (2 * p) == (i + j + k) // (2 * p):
                        pairs.append((i + j, i + j + k))
            k //= 2
        p *= 2
    return pairs


def _top_values(scores, count):
    rows, lanes = scores[0].shape
    n = rows // SUBLANES
    assert n * SUBLANES == rows and n & (n - 1) == 0
    low = jnp.full((SUBLANES, lanes), -jnp.inf, F32)
    cols = [[s[g * SUBLANES:(g + 1) * SUBLANES, :] for g in range(n)] for s in scores]
    for a, b in _sort_network(n):
        for w in cols:
            w[a], w[b] = jnp.maximum(w[a], w[b]), jnp.minimum(w[a], w[b])
    slot = lax.broadcasted_iota(jnp.int32, (count, lanes), 0)
    vals = [jnp.zeros((count, lanes), F32) for _ in scores]
    distinct = [jnp.ones((1, lanes), jnp.bool_) for _ in scores]
    last = [None for _ in scores]
    for r in range(count):
        for i, w in enumerate(cols):
            m = jnp.max(w[0], axis=0, keepdims=True)
            win = w[0] == m
            vals[i] = jnp.where(slot == r, m, vals[i])
            if last[i] is not None:
                distinct[i] = distinct[i] & (m < last[i])
            last[i] = m
            for g in range(min(count - r - 1, n)):
                w[g] = jnp.where(win, w[g + 1] if g + 1 < n else low, w[g])
    return vals, distinct


def _count(mask):
    return jnp.sum(mask.astype(F32), axis=0, keepdims=True)


def _route_heads(s1, s2, p1, p2, neg, p1t, key_idx, cand_idx, tie_break):
    n = len(s1)
    both = list(s1) + list(s2)
    if tie_break:
        vals, ranks = _take_top(both, key_idx, TOPK, True)
        rank_is = lambda i, a: ranks[i] == float(a)
        member = lambda i: ranks[i] < float(TOPK)
    else:
        vals, distinct = _top_values(both, TOPK)
        rank_is = lambda i, a: both[i] == vals[i][a:a + 1, :]
        member = lambda i: both[i] >= vals[i][TOPK - 1:TOPK, :]
    v1, v2 = vals[:n], vals[n:]
    cand = [_dot_f32(p1, v1[i], ((1,), (0,))) + _dot_f32(p2, v2[i], ((1,), (0,))) + neg for i in range(n)]
    if tie_break:
        _, crank = _take_top(cand, cand_idx, TOPK, True)
        pick = lambda i: crank[i] < float(TOPK)
    else:
        cvals, cdistinct = _top_values(cand, TOPK)
        pick = lambda i: cand[i] >= cvals[i][TOPK - 1:TOPK, :]
    out = []
    for i in range(n):
        chosen = pick(i)
        cmax = v1[i][0:1, :] + v2[i][0:1, :]
        z = jnp.sum(jnp.where(chosen, jnp.exp(cand[i] - cmax), 0.0), axis=0, keepdims=True)
        width = jnp.dot(p1t, chosen.astype(F32), preferred_element_type=F32)
        b1 = jnp.zeros_like(s1[i])
        r2 = jnp.full(s2[i].shape, float(TOPK), F32)
        for a in range(TOPK):
            b1 = jnp.where(rank_is(i, a), width[a:a + 1, :], b1)
            r2 = jnp.where(rank_is(n + i, a), float(a), r2)
        in1, in2 = member(i), member(n + i)
        c1 = jnp.where(in1, jnp.exp(jnp.where(in1, s1[i] - v1[i][0:1, :], 0.0)) / z, 0.0)
        e2 = jnp.where(in2, jnp.exp(jnp.where(in2, s2[i] - v2[i][0:1, :], 0.0)), 0.0)
        clean = ((_count(in1) == float(TOPK)) & (_count(in2) == float(TOPK))
                 & (_count(chosen) == float(TOPK)))
        if not tie_break:
            clean = clean & distinct[i] & distinct[n + i] & cdistinct[i]
        out.append((c1, b1, r2, e2, clean))
    return out


ROUTE_HEADS = 4


def _route_body(q_ref, keys_ref, p1_ref, p2_ref, neg_ref, p1t_ref, u_ref, v_ref,
                r2_ref, e2_ref, c1_ref, b1_ref, ub_ref, vt_ref, q3_ref):
    ub_ref[...] = u_ref[0].astype(BF16)
    vt_ref[...] = v_ref[0].T.astype(BF16)

    for hp in range(2 * P_HEADS):
        q3_ref[hp] = q_ref[:, hp * LANES:(hp + 1) * LANES]
    key_idx = lax.broadcasted_iota(jnp.int32, (N_KEYS, LANES), 0).astype(F32)
    cand_idx = lax.broadcasted_iota(jnp.int32, (p1_ref.shape[0], LANES), 0).astype(F32)

    def emit(heads, tie_break):
        s1 = [_dot_f32(keys_ref[h, 0], q3_ref[2 * h], ((1,), (1,))) for h in heads]
        s2 = [_dot_f32(keys_ref[h, 1], q3_ref[2 * h + 1], ((1,), (1,))) for h in heads]
        rows = _route_heads(s1, s2, p1_ref[...], p2_ref[...], neg_ref[...], p1t_ref[...],
                            key_idx, cand_idx, tie_break)
        all_clean = None
        for h, (c1, b1, r2, e2, clean) in zip(heads, rows):
            c1_ref[h] = c1
            b1_ref[h] = b1
            r2_ref[h] = r2.astype(r2_ref.dtype)
            e2_ref[h] = e2.astype(e2_ref.dtype)
            all_clean = clean if all_clean is None else all_clean & clean
        return all_clean

    def step(i, carry):
        heads = [i * ROUTE_HEADS + j for j in range(ROUTE_HEADS)]
        all_clean = emit(heads, False)

        @pl.when(jnp.min(all_clean.astype(F32)) < 0.5)
        def _():
            emit(heads, True)

        return carry

    lax.fori_loop(0, P_HEADS // ROUTE_HEADS, step, 0)


def _route(q, keys, table_u, table_v, layer):
    n = q.shape[0]
    _, n_exp, d = table_u.shape
    steps = n // LANES
    n_slabs = min(n_exp // TABLE_SLAB, 1 << (steps.bit_length() - 1))
    slab = n_exp // n_slabs
    assert n_slabs * slab == n_exp
    p1, p2, neg, p1t = _candidate_tables()
    whole = lambda a: pl.BlockSpec(a.shape, lambda i: (0,) * a.ndim)
    out_spec = pl.BlockSpec((P_HEADS, N_KEYS, LANES), lambda i: (0, 0, i))
    shape = lambda dt: jax.ShapeDtypeStruct((P_HEADS, N_KEYS, n), dt)
    slab_of = lambda i: jnp.minimum(i, n_slabs - 1)
    table_spec = pl.BlockSpec((1, slab, d), lambda i: (layer, slab_of(i), 0))
    return pl.pallas_call(
        _route_body,
        grid=(steps,),
        in_specs=[pl.BlockSpec((LANES, q.shape[1]), lambda i: (i, 0)), whole(keys),
                  whole(p1), whole(p2), whole(neg), whole(p1t), table_spec, table_spec],
        out_specs=[out_spec] * 4 + [pl.BlockSpec((slab, d), lambda i: (slab_of(i), 0)),
                                    pl.BlockSpec((d, slab), lambda i: (0, slab_of(i)))],
        out_shape=[shape(BF16), shape(BF16), shape(F32), shape(F32),
                   jax.ShapeDtypeStruct((n_exp, d), BF16), jax.ShapeDtypeStruct((d, n_exp), BF16)],
        scratch_shapes=[pltpu.VMEM((2 * P_HEADS, LANES, LANES), F32)],
        compiler_params=_cparams(("arbitrary",)),
        name="peer_route",
    )(q, keys, jnp.asarray(p1), jnp.asarray(p2), jnp.asarray(neg), jnp.asarray(p1t), table_u, table_v)


def _bf16_pair(c):
    hi = float(np.asarray(c, dtype=BF16))
    return hi, float(np.asarray(c - hi, dtype=BF16))


def _gelu_tanh(x):
    c0 = float(np.sqrt(2.0 / np.pi))
    c0_hi, c0_lo = _bf16_pair(c0)
    c1_hi, c1_lo = _bf16_pair(c0 * 0.044715)
    x2 = x * x
    inner = x * ((c1_hi * x2 + c0_hi) + (c1_lo * x2 + c0_lo))
    half = 0.5 * x
    return half + half * jnp.tanh(inner)


def _peer_dense_body(split, xt_ref, u_ref, vt_ref, r2_ref, e2_ref, c1_ref, b1_ref, x_ref, gf_ref, *rest):
    if split is None:
        o_ref, acc_ref = rest
    else:
        o_ref, ys_ref, acc_ref = rest
    e = pl.program_id(1)
    n_groups = xt_ref.shape[1] // PEER_LANES

    @pl.when(e == 0)
    def _():
        acc_ref[...] = jnp.zeros_like(acc_ref)

    def scores(j):
        cols = slice(j * PEER_LANES, (j + 1) * PEER_LANES)
        return jnp.dot(u_ref[...], xt_ref[:, cols], preferred_element_type=F32)

    def gates(j):
        cols = slice(j * PEER_LANES, (j + 1) * PEER_LANES)
        parts = []
        for r in range(EXPERT_BLOCK // N_KEYS):
            gate = jnp.zeros((N_KEYS, PEER_LANES), BF16)
            for h in range(P_HEADS):
                hit = r2_ref[h, :, cols] < b1_ref[h, r:r + 1, cols].astype(BF16)
                gate = gate + c1_ref[h, r:r + 1, cols].astype(BF16) * jnp.where(
                    hit, e2_ref[h, :, cols], jnp.zeros((), BF16))
            parts.append(gate)
        return jnp.concatenate(parts, axis=0)

    s_next = scores(0)
    for j in range(n_groups):
        s = s_next
        if j + 1 < n_groups:
            s_next = scores(j + 1)
        cols = slice(j * PEER_LANES, (j + 1) * PEER_LANES)
        weighted = gates(j) * _gelu_tanh(s.astype(BF16))
        acc_ref[:, cols] += jnp.dot(vt_ref[...], weighted, preferred_element_type=F32)

    @pl.when(e == pl.num_programs(1) - 1)
    def _():
        for j in range(n_groups):
            rows = slice(j * PEER_LANES, (j + 1) * PEER_LANES)
            out = x_ref[rows, :] + acc_ref[:, rows].T
            if split is None:
                o_ref[rows, :] = out
                continue
            y = _rmsnorm(out, gf_ref[...])
            o_ref[rows, :] = y
            if j == split.group:
                @pl.when(pl.program_id(0) == split.block)
                def _():
                    ys_ref[...] = y[:ys_ref.shape[0], :]


class _SampleSplit(NamedTuple):
    block: int
    group: int


def _peer_dense(xt, u_b, vt_b, r2, e2, c1, b1, x, g_final, n_prompt=None, n_sample=None):
    d, n = xt.shape
    n_exp = u_b.shape[0]
    keys_per_block = EXPERT_BLOCK // N_KEYS
    once = pl.Buffered(1)
    rank_spec = pl.BlockSpec((P_HEADS, N_KEYS, TOK_BLOCK), lambda i, e: (0, 0, i), pipeline_mode=once)
    first_spec = pl.BlockSpec((P_HEADS, keys_per_block, TOK_BLOCK), lambda i, e: (0, e, i))
    row_spec = pl.BlockSpec((TOK_BLOCK, d), lambda i, e: (i, 0), pipeline_mode=once)
    if n_prompt is None:
        split = None
        out_specs = row_spec
        out_shape = jax.ShapeDtypeStruct((n, d), F32)
    else:
        start = n_prompt % TOK_BLOCK
        assert start % PEER_LANES == 0 and n_sample <= PEER_LANES and n_prompt + n_sample <= n
        split = _SampleSplit(n_prompt // TOK_BLOCK, start // PEER_LANES)
        out_specs = [row_spec, pl.BlockSpec((n_sample, d), lambda i, e: (0, 0))]
        out_shape = [jax.ShapeDtypeStruct((n_prompt, d), F32), jax.ShapeDtypeStruct((n_sample, d), F32)]
    return pl.pallas_call(
        functools.partial(_peer_dense_body, split),
        grid=(n // TOK_BLOCK, n_exp // EXPERT_BLOCK),
        in_specs=[pl.BlockSpec((d, TOK_BLOCK), lambda i, e: (0, i), pipeline_mode=once),
                  pl.BlockSpec((EXPERT_BLOCK, d), lambda i, e: (e, 0)),
                  pl.BlockSpec((d, EXPERT_BLOCK), lambda i, e: (0, e)),
                  rank_spec, rank_spec, first_spec, first_spec, row_spec,
                  pl.BlockSpec((1, d), lambda i, e: (0, 0))],
        out_specs=out_specs,
        out_shape=out_shape,
        scratch_shapes=[pltpu.VMEM((d, TOK_BLOCK), F32)],
        compiler_params=_cparams(("arbitrary", "arbitrary")),
        name="peer_dense" if split is None else "peer_dense_final",
    )(xt, u_b, vt_b, r2, e2, c1, b1, x, g_final)


def _peer(x, g_norm, wq, keys, table_u, table_v, layer, g_final, n_prompt=None, n_sample=None):
    q, xt = _norm_matmul_t(x, g_norm, wq, wq.shape[1])
    r2, e2, c1, b1, u_b, vt_b = _route(q, keys, table_u, table_v, layer)
    return _peer_dense(xt, u_b, vt_b, r2, e2, c1, b1, x, g_final, n_prompt, n_sample)


def _col_form(a, n_dk):
    ns = a.shape[0]
    a = a.reshape(ns // SAMPLE_BLOCK, SAMPLE_BLOCK, H_A, n_dk)
    return a.transpose(0, 2, 3, 1)


def kernel(x_prompt, x_sample, state_gla, state_conv_b, state_conv_c, norm_mix, norm_ffn, norm_final, w_in_even, w_a2, b_a2, gla_norm_g, conv_b_w, conv_b_bias, conv_b_ln_g, conv_b_ln_b, w_out_even, w_in_odd, conv_c_w, w_out_odd, peer_wq, peer_keys, peer_u, peer_v):
    bsz, t, d = x_prompt.shape
    ns = x_sample.shape[0]
    n_prompt = bsz * t
    n_real = n_prompt + ns
    n_tok = -(-n_real // TOK_BLOCK) * TOK_BLOCK
    assert t % GLA_CHUNK == 0 and t % CONV_ROWS == 0 and ns % (2 * SAMPLE_BLOCK) == 0
    assert x_sample.shape[1] == 1 and n_prompt % (2 * SAMPLE_BLOCK) == 0

    d_a = d // 2
    dk_a = d_a // 2
    d_b = d - d_a
    rank = w_a2.shape[1]
    col_a = 2 * dk_a + 2 * d_a
    row = lambda a: a.reshape(1, -1)

    x = jnp.concatenate([x_prompt.reshape(n_prompt, d), x_sample.reshape(ns, d),
                         jnp.zeros((n_tok - n_real, d), F32)], axis=0)

    w_in = w_in_even[0]
    w_main = jnp.concatenate([w_in[:, :col_a], w_in[:, col_a + rank:]], axis=1).astype(BF16)
    w_gate = jnp.pad(w_in[:, col_a:col_a + rank], ((0, 0), (0, LANES - rank))).astype(BF16)
    w_a2p = jnp.pad(w_a2[0], ((0, LANES - rank), (0, 0)))
    z, la = _norm_matmul_gate(x, row(norm_mix[0]), w_main, w_gate, w_a2p, row(b_a2[0]), 1024)

    gn = row(gla_norm_g[0])
    zs = z[n_prompt:n_real].astype(F32)
    las = la[n_prompt:n_real]
    oa, gla_s = _gla_sample(state_gla, _col_form(las, dk_a // H_A), _col_form(zs[:, dk_a:2 * dk_a], dk_a // H_A),
                            _col_form(zs[:, :dk_a], dk_a // H_A),
                            zs[:, 2 * dk_a:2 * dk_a + d_a], zs[:, 2 * dk_a + d_a:col_a], gn,
                            jnp.zeros((n_tok, d_a), F32), n_prompt)
    oa, gla_p = _gla_prompt(z, la, gn, bsz, t, oa)

    ub, cb_s = _conv_b_sample(state_conv_b[0].reshape(ns, -1), z, conv_b_w[0], row(conv_b_bias[0]),
                              n_prompt, col_a, d_b, jnp.zeros((n_tok, d_b), F32))
    ub, cb_p = _conv_b_prompt(z, conv_b_w[0], row(conv_b_bias[0]), bsz, t, col_a, d_b, ub)
    x = _res_matmul_ln(oa, ub, row(conv_b_ln_g[0]), row(conv_b_ln_b[0]), w_out_even[0].astype(BF16), x, d)
    wq_b = peer_wq.astype(BF16)
    x = _peer(x, row(norm_ffn[0]), wq_b[0], peer_keys[0], peer_u, peer_v, 0, row(norm_final))

    z = _norm_matmul(x, row(norm_mix[1]), w_in_odd[0].astype(BF16), 1024)
    gc, cc_s = _conv_c_sample(state_conv_c[0].reshape(ns, -1), z, conv_c_w[0], n_prompt, d,
                              jnp.zeros((n_tok, d), F32))
    gc, cc_p = _conv_c_prompt(z, conv_c_w[0], bsz, t, d, gc)
    x = _res_matmul(gc, w_out_odd[0].astype(BF16), x, d)
    y_prompt, y_sample = _peer(x, row(norm_ffn[1]), wq_b[1], peer_keys[1], peer_u, peer_v, 1, row(norm_final),
                               n_prompt, ns)
    return (y_prompt.reshape(bsz, t, d), y_sample.reshape(ns, 1, d), gla_p, cb_p, cc_p,
            gla_s, cb_s.reshape(state_conv_b.shape), cc_s.reshape(state_conv_c.shape))
```
